```python
import math
import jax, jax.numpy as jnp
from jax import lax
import numpy as np

D_MODEL = 1024
BATCH = 4
SEQ = 4096
DEPTH = 1

N_HEADS = 8
HEAD_DIM = D_MODEL // (2 * N_HEADS)
ATTN_WIDTH = N_HEADS * 2 * HEAD_DIM
ROT_DIM = HEAD_DIM // 4
ROPE_THETA = 500000.0
Q_BLOCK = 128
POOL_WINDOWS = (2, 4, 8, 16)
N_POOL_GROUPS = len(POOL_WINDOWS)
POOL_WIDTH = D_MODEL
POOL_GROUP_DIM = POOL_WIDTH // N_POOL_GROUPS
IN_COLS = POOL_WIDTH + 3 * ATTN_WIDTH + 2 * D_MODEL
N_EXPERTS = 16
EC_CAPACITY_FACTOR = 2
D_FF = ((8 * D_MODEL // 3 + 255) // 256) * 256
RMS_EPS = 1e-6
POS_OFFSET_MAX = 1024

kernel_name = "hybrid_pool_diffattn_ec_moe_encoder"


def rms_norm(x, g):
    xf = x.astype(jnp.float32)
    y = xf * lax.rsqrt(jnp.mean(xf * xf, axis=-1, keepdims=True) + RMS_EPS)
    return y.astype(x.dtype) * g


def pool_mixer(u, w_mix, scale):
    B, S, _ = u.shape
    ug = u.reshape(B, S, N_POOL_GROUPS, POOL_GROUP_DIM)
    cs = jnp.cumsum(ug.astype(jnp.float32), axis=1)
    cs = jnp.pad(cs, ((0, 0), (1, 0), (0, 0), (0, 0)))
    t = jnp.arange(S)
    outs = []
    for g, w in enumerate(POOL_WINDOWS):
        lo = jnp.clip(t - w // 2, 0, S)
        hi = jnp.clip(t + w // 2, 0, S)
        csg = cs[:, :, g]
        win = jnp.take(csg, hi, axis=1) - jnp.take(csg, lo, axis=1)
        cnt = (hi - lo).astype(jnp.float32)[None, :, None]
        pooled = (win / cnt).astype(u.dtype) - ug[:, :, g]
        outs.append(jnp.einsum('bsc,cd->bsd', pooled, w_mix[g]))
    return jnp.concatenate(outs, axis=-1) * scale


def rotary(x, pos):
    half = ROT_DIM // 2
    inv = ROPE_THETA ** (-jnp.arange(half, dtype=jnp.float32) * 2.0 / ROT_DIM)
    ang = pos.astype(jnp.float32)[..., None] * inv
    cos = jnp.cos(ang)[:, :, None, None, :]
    sin = jnp.sin(ang)[:, :, None, None, :]
    x1 = x[..., :half].astype(jnp.float32)
    x2 = x[..., half:ROT_DIM].astype(jnp.float32)
    r = jnp.concatenate([x1 * cos - x2 * sin, x2 * cos + x1 * sin], axis=-1).astype(x.dtype)
    return jnp.concatenate([r, x[..., ROT_DIM:]], axis=-1)


def diff_attention(q, k, v, lam):
    B, S = q.shape[:2]
    nb = S // Q_BLOCK
    qb = q.reshape(B, nb, Q_BLOCK, N_HEADS, 2, HEAD_DIM).transpose(1, 0, 2, 3, 4, 5)
    scale = HEAD_DIM ** -0.5

    def one_block(qblk):
        s = jnp.einsum('bqhmd,bkhmd->bhmqk', qblk, k, preferred_element_type=jnp.float32) * scale
        p = jax.nn.softmax(s, axis=-1)
        a = p[:, :, 0] - lam * p[:, :, 1]
        return jnp.einsum('bhqk,bkhe->bqhe', a.astype(v.dtype), v)

    o = lax.map(one_block, qb)
    return o.transpose(1, 0, 2, 3, 4).reshape(B, S, N_HEADS, 2 * HEAD_DIM)


def expert_choice_ffn(h, w_router, w_gate, w_up, w_down):
    B, S, D = h.shape
    cap = EC_CAPACITY_FACTOR * S // N_EXPERTS
    logits = jnp.einsum('bsd,de->bse', h, w_router, preferred_element_type=jnp.float32)
    aff = jax.nn.softmax(logits, axis=-1)
    vals, idx = lax.top_k(aff.transpose(0, 2, 1), cap)
    xs = jax.vmap(lambda hb, ib: hb[ib])(h, idx.reshape(B, -1)).reshape(B, N_EXPERTS, cap, D)
    g = jnp.einsum('becd,edf->becf', xs, w_gate)
    u = jnp.einsum('becd,edf->becf', xs, w_up)
    y = jnp.einsum('becf,efd->becd', jax.nn.silu(g) * u, w_down) * vals[..., None].astype(h.dtype)
    flat = (jnp.arange(B)[:, None, None] * S + idx).reshape(-1)
    out = jax.ops.segment_sum(y.reshape(-1, D), flat, num_segments=B * S)
    return out.reshape(B, S, D)


def setup_inputs(seed: int = 0) -> dict:
    key = jax.random.key(seed)
    ks = jax.random.split(key, 24)
    f32 = jnp.float32
    nrm = lambda k, shape, fan_in: jax.random.normal(k, shape, f32) * (fan_in ** -0.5)
    x = jax.random.normal(ks[0], (BATCH, SEQ, D_MODEL), f32)
    offsets = jax.random.randint(ks[1], (BATCH, 1), 0, POS_OFFSET_MAX, dtype=jnp.int32)
    positions = offsets + jnp.arange(SEQ, dtype=jnp.int32)[None, :]
    return {
        "x": x,
        "positions": positions,
        "g_mix": 1.0 + 0.02 * jax.random.normal(ks[2], (DEPTH, D_MODEL), f32),
        "w_in": nrm(ks[3], (DEPTH, D_MODEL, IN_COLS), D_MODEL),
        "w_pool_mix": nrm(ks[4], (DEPTH, N_POOL_GROUPS, POOL_GROUP_DIM, POOL_GROUP_DIM), POOL_GROUP_DIM),
        "pool_scale": 1.0 + 0.02 * jax.random.normal(ks[5], (DEPTH, POOL_WIDTH), f32),
        "w_pool_out": nrm(ks[6], (DEPTH, POOL_WIDTH, D_MODEL), POOL_WIDTH),
        "lam_q1": 0.1 * jax.random.normal(ks[7], (DEPTH, HEAD_DIM), f32),
        "lam_k1": 0.1 * jax.random.normal(ks[8], (DEPTH, HEAD_DIM), f32),
        "lam_q2": 0.1 * jax.random.normal(ks[9], (DEPTH, HEAD_DIM), f32),
        "lam_k2": 0.1 * jax.random.normal(ks[10], (DEPTH, HEAD_DIM), f32),
        "g_subln": 1.0 + 0.02 * jax.random.normal(ks[11], (DEPTH, 2 * HEAD_DIM), f32),
        "w_attn_out": nrm(ks[12], (DEPTH, ATTN_WIDTH, D_MODEL), ATTN_WIDTH),
        "w_out": nrm(ks[13], (DEPTH, D_MODEL, D_MODEL), D_MODEL),
        "g_ffn": 1.0 + 0.02 * jax.random.normal(ks[14], (DEPTH, D_MODEL), f32),
        "w_router": nrm(ks[15], (DEPTH, D_MODEL, N_EXPERTS), D_MODEL),
        "w_gate": nrm(ks[16], (DEPTH, N_EXPERTS, D_MODEL, D_FF), D_MODEL),
        "w_up": nrm(ks[17], (DEPTH, N_EXPERTS, D_MODEL, D_FF), D_MODEL),
        "w_down": nrm(ks[18], (DEPTH, N_EXPERTS, D_FF, D_MODEL), D_FF),
        "g_final": 1.0 + 0.02 * jax.random.normal(ks[19], (D_MODEL,), f32),
    }


def reference(x, positions, g_mix, w_in, w_pool_mix, pool_scale, w_pool_out, lam_q1, lam_k1,
              lam_q2, lam_k2, g_subln, w_attn_out, w_out, g_ffn, w_router, w_gate, w_up, w_down,
              g_final):
    B, S, _ = x.shape
    c_q = POOL_WIDTH
    c_k = c_q + ATTN_WIDTH
    c_v = c_k + ATTN_WIDTH
    c_gp = c_v + ATTN_WIDTH
    c_ga = c_gp + D_MODEL
    for l in range(DEPTH):
        lam_init = 0.8 - 0.6 * math.exp(-0.3 * l)
        h = rms_norm(x, g_mix[l])
        z = jnp.einsum('bsd,dc->bsc', h, w_in[l])
        u_pool = z[..., :c_q]
        q = z[..., c_q:c_k].reshape(B, S, N_HEADS, 2, HEAD_DIM)
        k = z[..., c_k:c_v].reshape(B, S, N_HEADS, 2, HEAD_DIM)
        v = z[..., c_v:c_gp].reshape(B, S, N_HEADS, 2 * HEAD_DIM)
        gate_pool = jax.nn.sigmoid(z[..., c_gp:c_ga])
        gate_attn = jax.nn.sigmoid(z[..., c_ga:])
        y_pool = jnp.einsum('bsc,cd->bsd', pool_mixer(u_pool, w_pool_mix[l], pool_scale[l]), w_pool_out[l])
        q = rotary(q, positions)
        k = rotary(k, positions)
        lam = (jnp.exp(jnp.sum(lam_q1[l].astype(jnp.float32) * lam_k1[l].astype(jnp.float32)))
               - jnp.exp(jnp.sum(lam_q2[l].astype(jnp.float32) * lam_k2[l].astype(jnp.float32)))
               + lam_init)
        o = diff_attention(q, k, v, lam)
        o = rms_norm(o, g_subln[l]) * (1.0 - lam_init)
        y_attn = jnp.einsum('bsc,cd->bsd', o.reshape(B, S, ATTN_WIDTH), w_attn_out[l])
        merged = gate_pool * y_pool + gate_attn * y_attn
        x = x + jnp.einsum('bsd,de->bse', merged, w_out[l])
        x = x + expert_choice_ffn(rms_norm(x, g_ffn[l]), w_router[l], w_gate[l], w_up[l], w_down[l])
    return rms_norm(x, g_final)
```

```python
import functools
import math

import jax
import jax.numpy as jnp
from jax import lax
from jax.experimental import pallas as pl
from jax.experimental.pallas import tpu as pltpu

F32 = jnp.float32
BF16 = jnp.bfloat16
I32 = jnp.int32

D_MODEL = 1024
N_HEADS = 8
HEAD_DIM = 64
HEAD_WIDTH = 2 * HEAD_DIM
ROT_DIM = HEAD_DIM // 4
ROT_HALF = ROT_DIM // 2
ROPE_THETA = 500000.0
POOL_WINDOWS = (2, 4, 8, 16)
POOL_GROUP_DIM = D_MODEL // len(POOL_WINDOWS)
POOL_HALO = 16
N_EXPERTS = 16
EC_CAPACITY_FACTOR = 2
RMS_EPS = 1e-6
LANES = 128

VMEM_LIMIT = 56 * 1024 * 1024


def _params(sem):
    return pltpu.CompilerParams(dimension_semantics=sem, vmem_limit_bytes=VMEM_LIMIT)


def _rms(x, g):
    return (x * lax.rsqrt(jnp.mean(x * x, axis=-1, keepdims=True) + RMS_EPS)) * g


def _dot(a, b):
    return jnp.dot(a, b, preferred_element_type=F32)


def _rotary(z, cos, sin, first_half):
    up = pltpu.roll(z, LANES - ROT_HALF, 1)
    down = pltpu.roll(z, ROT_HALF, 1)
    partner = jnp.where(first_half, -up, down)
    return z * cos + partner * sin


def _inproj_kernel(x_ref, pos_ref, g_ref, inv_ref, w_ref,
                   u_ref, q_ref, k1_ref, k2_ref, v_ref, gp_ref, ga_ref):
    hb = _rms(x_ref[...], g_ref[...]).astype(BF16)
    ang = pos_ref[...] * inv_ref[...]
    cos = jnp.cos(ang)
    sin = jnp.sin(ang)
    lane = lax.broadcasted_iota(I32, ang.shape, 1)
    first_half = (lane % HEAD_DIM) < ROT_HALF
    map1 = lane < HEAD_DIM

    def col(j):
        return _dot(hb, w_ref[:, j * D_MODEL:(j + 1) * D_MODEL])

    u_ref[...] = col(0)
    zq = col(1)
    zk = col(2)
    for h in range(N_HEADS):
        sl = slice(h * HEAD_WIDTH, (h + 1) * HEAD_WIDTH)
        q_ref[:, sl] = (_rotary(zq[:, sl], cos, sin, first_half) * (HEAD_DIM ** -0.5)).astype(BF16)
        kr = _rotary(zk[:, sl], cos, sin, first_half)
        k1_ref[:, sl] = jnp.where(map1, kr, 0.0).astype(BF16)
        k2_ref[:, sl] = jnp.where(map1, 0.0, kr).astype(BF16)
    v_ref[...] = col(3).astype(BF16)
    gp_ref[...] = 1.0 / (1.0 + jnp.exp(-col(4)))
    ga_ref[...] = 1.0 / (1.0 + jnp.exp(-col(5)))


def _inproj(x2d, posb, g_mix, inv_tab, w_in_bf, tm=256):
    t = x2d.shape[0]
    n_cols = w_in_bf.shape[1]
    row = lambda i: (i, 0)
    const = lambda i: (0, 0)
    tile = pl.BlockSpec((tm, D_MODEL), row)
    out_shapes = [jax.ShapeDtypeStruct((t, D_MODEL), dt) for dt in (F32, BF16, BF16, BF16, BF16, F32, F32)]
    return pl.pallas_call(
        _inproj_kernel,
        grid=(t // tm,),
        in_specs=[tile, pl.BlockSpec((tm, LANES), row), pl.BlockSpec((1, D_MODEL), const),
                  pl.BlockSpec((1, LANES), const), pl.BlockSpec((D_MODEL, n_cols), const)],
        out_specs=[tile] * 7,
        out_shape=out_shapes,
        compiler_params=_params(("parallel",)),
        name="inproj",
    )(x2d, posb, g_mix, inv_tab, w_in_bf)


def _attn_kernel(lam_init, q_ref, k1_ref, k2_ref, v_ref, lamv_ref, g_ref, o_ref):
    lv = lamv_ref[...]
    lam = (jnp.exp(jnp.sum(lv[0:1] * lv[1:2], axis=-1, keepdims=True))
           - jnp.exp(jnp.sum(lv[2:3] * lv[3:4], axis=-1, keepdims=True)) + lam_init)
    q = q_ref[0]
    nt = (((1,), (1,)), ((), ()))
    s1 = lax.dot_general(q, k1_ref[0], nt, preferred_element_type=F32)
    s2 = lax.dot_general(q, k2_ref[0], nt, preferred_element_type=F32)
    e1 = jnp.exp(s1 - jnp.max(s1, axis=-1, keepdims=True))
    e2 = jnp.exp(s2 - jnp.max(s2, axis=-1, keepdims=True))
    r1 = 1.0 / jnp.sum(e1, axis=-1, keepdims=True)
    r2 = lam / jnp.sum(e2, axis=-1, keepdims=True)
    a = (e1 * r1 - e2 * r2).astype(BF16)
    o = _dot(a, v_ref[0])
    o_ref[0] = (_rms(o, g_ref[...]) * (1.0 - lam_init)).astype(BF16)


def _attention(q, k1, k2, v, lamv, g_subln, lam_init, tq=256):
    b, s, _ = q.shape
    qspec = pl.BlockSpec((1, tq, HEAD_WIDTH), lambda bi, h, qi: (bi, qi, h))
    kspec = pl.BlockSpec((1, s, HEAD_WIDTH), lambda bi, h, qi: (bi, 0, h))
    const = lambda bi, h, qi: (0, 0)
    return pl.pallas_call(
        functools.partial(_attn_kernel, lam_init),
        grid=(b, N_HEADS, s // tq),
        in_specs=[qspec, kspec, kspec, kspec,
                  pl.BlockSpec((8, LANES), const), pl.BlockSpec((1, HEAD_WIDTH), const)],
        out_specs=qspec,
        out_shape=jax.ShapeDtypeStruct(q.shape, BF16),
        compiler_params=_params(("parallel", "parallel", "arbitrary")),
        name="attn",
    )(q, k1, k2, v, lamv, g_subln)


def _mix_kernel(seq, u_ref, up_ref, un_ref, o_ref, gp_ref, ga_ref, x_ref,
                wmix_ref, ps_ref, wpo_ref, wao_ref, wo_ref, gffn_ref, wr_ref,
                x1_ref, hn_ref, lg_ref, upad_ref):
    tm = u_ref.shape[1]
    i = pl.program_id(1)
    nt = pl.num_programs(1)
    main = u_ref[0]
    upad_ref[0:POOL_HALO, :] = jnp.where(i > 0, up_ref[0], 0.0)
    upad_ref[POOL_HALO:POOL_HALO + tm, :] = main
    upad_ref[POOL_HALO + tm:, :] = jnp.where(i < nt - 1, un_ref[0], 0.0)
    t = i * tm + lax.broadcasted_iota(I32, (tm, 1), 0)
    mixed = []
    for g, w in enumerate(POOL_WINDOWS):
        cols = slice(g * POOL_GROUP_DIM, (g + 1) * POOL_GROUP_DIM)
        win = upad_ref[POOL_HALO - w // 2:POOL_HALO - w // 2 + tm, cols]
        for d in range(-w // 2 + 1, w // 2):
            win = win + upad_ref[POOL_HALO + d:POOL_HALO + d + tm, cols]
        cnt = (jnp.minimum(t + w // 2, seq) - jnp.maximum(t - w // 2, 0)).astype(F32)
        pooled = win / cnt - main[:, cols]
        mixed.append((_dot(pooled.astype(BF16), wmix_ref[g]) * ps_ref[:, cols]).astype(BF16))
    y_pool = _dot(jnp.concatenate(mixed, axis=-1), wpo_ref[...])
    y_attn = _dot(o_ref[0], wao_ref[...])
    merged = gp_ref[0] * y_pool + ga_ref[0] * y_attn
    x1 = x_ref[0] + _dot(merged.astype(BF16), wo_ref[...])
    x1_ref[0] = x1
    hn = _rms(x1, gffn_ref[...]).astype(BF16)
    hn_ref[0] = hn
    lg_ref[0] = _dot(hn, wr_ref[...])


def _mix(u, o, gp, ga, x, wmix, ps, wpo, wao, wo, gffn, wr, tm=256):
    b, s, d = x.shape
    hb = tm // POOL_HALO
    n_halo = s // POOL_HALO
    tile = pl.BlockSpec((1, tm, d), lambda bi, i: (bi, i, 0))
    prev = pl.BlockSpec((1, POOL_HALO, d), lambda bi, i: (bi, jnp.maximum(i * hb - 1, 0), 0))
    nxt = pl.BlockSpec((1, POOL_HALO, d), lambda bi, i: (bi, jnp.minimum((i + 1) * hb, n_halo - 1), 0))
    c2 = lambda bi, i: (0, 0)
    c3 = lambda bi, i: (0, 0, 0)
    return pl.pallas_call(
        functools.partial(_mix_kernel, s),
        grid=(b, s // tm),
        in_specs=[tile, prev, nxt, tile, tile, tile, tile,
                  pl.BlockSpec(wmix.shape, c3), pl.BlockSpec((1, d), c2),
                  pl.BlockSpec((d, d), c2), pl.BlockSpec((d, d), c2), pl.BlockSpec((d, d), c2),
                  pl.BlockSpec((1, d), c2), pl.BlockSpec((d, LANES), c2)],
        out_specs=[tile, tile, pl.BlockSpec((1, tm, LANES), lambda bi, i: (bi, i, 0))],
        out_shape=[jax.ShapeDtypeStruct((b, s, d), F32), jax.ShapeDtypeStruct((b, s, d), BF16),
                   jax.ShapeDtypeStruct((b, s, LANES), F32)],
        scratch_shapes=[pltpu.VMEM((tm + 2 * POOL_HALO, d), F32)],
        compiler_params=_params(("parallel", "parallel")),
        name="mix",
    )(u, u, u, o, gp, ga, x, wmix, ps, wpo, wao, wo, gffn, wr)


PREFIX_BLOCK = 256


def _prefix_rows(mask, tri):
    out = []
    carry = jnp.zeros((1, mask.shape[1]), F32)
    for blk in range(mask.shape[0] // PREFIX_BLOCK):
        m = mask[blk * PREFIX_BLOCK:(blk + 1) * PREFIX_BLOCK]
        p = _dot(tri, m.astype(BF16)) + carry
        carry = p[PREFIX_BLOCK - 1:PREFIX_BLOCK]
        out.append(p)
    return jnp.concatenate(out, axis=0)


def _route_kernel(cap, lg_ref, slot_ref, aff_ref):
    lg = lg_ref[0]
    lane = lax.broadcasted_iota(I32, lg.shape, 1)
    valid = lane < N_EXPERTS
    lg = jnp.where(valid, lg, -jnp.inf)
    ex = jnp.exp(lg - jnp.max(lg, axis=-1, keepdims=True))
    aff = ex / jnp.sum(ex, axis=-1, keepdims=True)
    aff_ref[0] = aff
    bits = lax.bitcast_convert_type(aff, I32)

    def step(k, ans):
        cand = ans | jnp.left_shift(jnp.int32(1), 30 - k)
        cnt = jnp.sum(jnp.where(bits >= cand, 1.0, 0.0), axis=0, keepdims=True)
        return jnp.where(cnt >= cap, cand, ans)

    thr = lax.fori_loop(0, 31, step, jnp.zeros((1, LANES), I32))
    gt = jnp.where(bits > thr, 1.0, 0.0)
    eq = jnp.where(bits == thr, 1.0, 0.0)
    r, c = lax.broadcasted_iota(I32, (PREFIX_BLOCK, PREFIX_BLOCK), 0), lax.broadcasted_iota(I32, (PREFIX_BLOCK, PREFIX_BLOCK), 1)
    tri = jnp.where(c <= r, 1.0, 0.0).astype(BF16)
    room = cap - jnp.sum(gt, axis=0, keepdims=True)
    eq_before = _prefix_rows(eq, tri) - eq
    sel = jnp.maximum(gt, jnp.where(eq_before < room, eq, 0.0))
    slot = _prefix_rows(sel, tri) - 1.0
    slot_ref[0] = jnp.where(valid & (sel > 0.0), slot, -1.0).astype(I32)


def _route(logits, cap):
    b, s, _ = logits.shape
    spec = pl.BlockSpec((1, s, LANES), lambda bi: (bi, 0, 0))
    return pl.pallas_call(
        functools.partial(_route_kernel, cap),
        grid=(b,),
        in_specs=[spec],
        out_specs=[spec, spec],
        out_shape=[jax.ShapeDtypeStruct((b, s, LANES), I32), jax.ShapeDtypeStruct((b, s, LANES), F32)],
        compiler_params=_params(("parallel",)),
        name="route",
    )(logits)


def _gather_kernel(cap, slot_ref, aff_ref, h_ref, xs_ref, val_ref):
    sid = slot_ref[0, 0]
    s = sid.shape[1]
    hit = sid == lax.broadcasted_iota(I32, (cap, s), 0)
    xs_ref[0] = _dot(jnp.where(hit, 1.0, 0.0).astype(BF16), h_ref[0]).astype(BF16)
    vals = jnp.sum(jnp.where(hit, aff_ref[0, 0], 0.0), axis=-1, keepdims=True)
    val_ref[0] = jnp.broadcast_to(vals, (cap, LANES))


def _gather(slot_row, aff_row, hn, cap):
    b, s, d = hn.shape
    rspec = pl.BlockSpec((1, 1, 1, s), lambda bi, e: (bi, e, 0, 0))
    return pl.pallas_call(
        functools.partial(_gather_kernel, cap),
        grid=(b, N_EXPERTS),
        in_specs=[rspec, rspec, pl.BlockSpec((1, s, d), lambda bi, e: (bi, 0, 0))],
        out_specs=[pl.BlockSpec((1, cap, d), lambda bi, e: (e, bi, 0)),
                   pl.BlockSpec((1, cap, LANES), lambda bi, e: (e, bi, 0))],
        out_shape=[jax.ShapeDtypeStruct((N_EXPERTS, b * cap, d), BF16),
                   jax.ShapeDtypeStruct((N_EXPERTS, b * cap, LANES), F32)],
        compiler_params=_params(("parallel", "arbitrary")),
        name="gather",
    )(slot_row, aff_row, hn)


def _ffn_kernel(xs_ref, wg_ref, wu_ref, wd_ref, val_ref, y_ref, acc_ref):
    f = pl.program_id(1)
    xs = xs_ref[0]
    g = _dot(xs, wg_ref[0].astype(BF16))
    u = _dot(xs, wu_ref[0].astype(BF16))
    act = (g / (1.0 + jnp.exp(-g)) * u).astype(BF16)
    part = _dot(act, wd_ref[0].astype(BF16))

    @pl.when(f == 0)
    def _():
        acc_ref[...] = part

    @pl.when(f > 0)
    def _():
        acc_ref[...] += part

    @pl.when(f == pl.num_programs(1) - 1)
    def _():
        y_ref[0] = (acc_ref[...] * val_ref[0][:, 0:1]).astype(BF16)


def _ffn(xs, w_gate, w_up, w_down, vals, tf=256):
    e, m, d = xs.shape
    ff = w_gate.shape[2]
    row = lambda ei, f: (ei, 0, 0)
    return pl.pallas_call(
        _ffn_kernel,
        grid=(e, ff // tf),
        in_specs=[pl.BlockSpec((1, m, d), row),
                  pl.BlockSpec((1, d, tf), lambda ei, f: (ei, 0, f)),
                  pl.BlockSpec((1, d, tf), lambda ei, f: (ei, 0, f)),
                  pl.BlockSpec((1, tf, d), lambda ei, f: (ei, f, 0)),
                  pl.BlockSpec((1, m, LANES), row)],
        out_specs=pl.BlockSpec((1, m, d), row),
        out_shape=jax.ShapeDtypeStruct((e, m, d), BF16),
        scratch_shapes=[pltpu.VMEM((m, d), F32)],
        compiler_params=_params(("parallel", "arbitrary")),
        name="ffn",
    )(xs, w_gate, w_up, w_down, vals)


def _combine_kernel(cap, x1_ref, slot_ref, y_ref, g_ref, o_ref):
    slot = slot_ref[0]
    tt = slot.shape[0]
    cidx = lax.broadcasted_iota(I32, (tt, cap), 1)
    acc = x1_ref[0]
    for e in range(N_EXPERTS):
        hit = slot[:, e:e + 1] == cidx
        acc = acc + _dot(jnp.where(hit, 1.0, 0.0).astype(BF16), y_ref[e])
    o_ref[0] = _rms(acc, g_ref[...])


def _combine(x1, slot_t, y, g_final, cap, tt=256):
    b, s, d = x1.shape
    tile = lambda bi, i: (bi, i, 0)
    return pl.pallas_call(
        functools.partial(_combine_kernel, cap),
        grid=(b, s // tt),
        in_specs=[pl.BlockSpec((1, tt, d), tile), pl.BlockSpec((1, tt, LANES), tile),
                  pl.BlockSpec((N_EXPERTS, cap, d), lambda bi, i: (0, bi, 0)),
                  pl.BlockSpec((1, d), lambda bi, i: (0, 0))],
        out_specs=pl.BlockSpec((1, tt, d), tile),
        out_shape=jax.ShapeDtypeStruct((b, s, d), F32),
        compiler_params=_params(("parallel", "arbitrary")),
        name="combine",
    )(x1, slot_t, y, g_final)


def _layer(x, positions, g_mix, w_in, w_pool_mix, pool_scale, w_pool_out, lam_q1, lam_k1, lam_q2,
           lam_k2, g_subln, w_attn_out, w_out, g_ffn, w_router, w_gate, w_up, w_down, lam_init):
    b, s, d = x.shape
    t = b * s
    cap = EC_CAPACITY_FACTOR * s // N_EXPERTS

    posb = jnp.broadcast_to(positions.reshape(t, 1).astype(F32), (t, LANES))
    inv = ROPE_THETA ** (-jnp.arange(ROT_HALF, dtype=F32) * 2.0 / ROT_DIM)
    lane = jnp.arange(LANES)
    inv_tab = jnp.where((lane % HEAD_DIM) < ROT_DIM, inv[lane % ROT_HALF], 0.0).reshape(1, LANES).astype(F32)

    u, q, k1, k2, v, gp, ga = _inproj(x.reshape(t, d), posb, g_mix.reshape(1, d), inv_tab, w_in.astype(BF16))
    sh = lambda a: a.reshape(b, s, d)

    lamv = jnp.zeros((8, LANES), F32).at[0:4, 0:HEAD_DIM].set(jnp.stack([lam_q1, lam_k1, lam_q2, lam_k2]).astype(F32))
    o = _attention(sh(q), sh(k1), sh(k2), sh(v), lamv, g_subln.reshape(1, HEAD_WIDTH), lam_init)

    wr = jnp.zeros((d, LANES), BF16).at[:, :N_EXPERTS].set(w_router.astype(BF16))
    x1, hn, logits = _mix(sh(u), o, sh(gp), sh(ga), x, w_pool_mix.astype(BF16), pool_scale.reshape(1, d),
                          w_pool_out.astype(BF16), w_attn_out.astype(BF16), w_out.astype(BF16),
                          g_ffn.reshape(1, d), wr)

    slot_t, aff_t = _route(logits, cap)
    to_rows = lambda a: jnp.transpose(a[:, :, :N_EXPERTS], (0, 2, 1)).reshape(b, N_EXPERTS, 1, s)
    xs, vals = _gather(to_rows(slot_t), to_rows(aff_t), hn, cap)
    y = _ffn(xs, w_gate, w_up, w_down, vals)
    return x1, slot_t, y, cap


def kernel(x, positions, g_mix, w_in, w_pool_mix, pool_scale, w_pool_out, lam_q1, lam_k1, lam_q2, lam_k2,
           g_subln, w_attn_out, w_out, g_ffn, w_router, w_gate, w_up, w_down, g_final):
    depth = g_mix.shape[0]
    assert depth == 1, "the final norm is fused into the last layer's combine step"
    l = 0
    lam_init = 0.8 - 0.6 * math.exp(-0.3 * l)
    x1, slot_t, y, cap = _layer(x, positions, g_mix[l], w_in[l], w_pool_mix[l], pool_scale[l], w_pool_out[l],
                                lam_q1[l], lam_k1[l], lam_q2[l], lam_k2[l], g_subln[l], w_attn_out[l],
                                w_out[l], g_ffn[l], w_router[l], w_gate[l], w_up[l], w_down[l], lam_init)
    return _combine(x1, slot_t, y, g_final.reshape(1, D_MODEL), cap)
```

```python
import functools
import math

import jax
import jax.numpy as jnp
from jax import lax
from jax.experimental import pallas as pl
from jax.experimental.pallas import tpu as pltpu

F32 = jnp.float32
BF16 = jnp.bfloat16
I32 = jnp.int32

D_MODEL = 1024
N_HEADS = 8
HEAD_DIM = 64
HEAD_WIDTH = 2 * HEAD_DIM
ROT_DIM = HEAD_DIM // 4
ROT_HALF = ROT_DIM // 2
ROPE_THETA = 500000.0
POOL_WINDOWS = (2, 4, 8, 16)
POOL_GROUP_DIM = D_MODEL // len(POOL_WINDOWS)
POOL_HALO = 16
N_EXPERTS = 16
EC_CAPACITY_FACTOR = 2
RMS_EPS = 1e-6
Q_SCALE = HEAD_DIM ** -0.5 * math.log2(math.e)
LANES = 128

VMEM_LIMIT = 56 * 1024 * 1024


def _params(sem):
    return pltpu.CompilerParams(dimension_semantics=sem, vmem_limit_bytes=VMEM_LIMIT)


def _rms(x, g):
    return (x * lax.rsqrt(jnp.mean(x * x, axis=-1, keepdims=True) + RMS_EPS)) * g


def _dot(a, b):
    return jnp.dot(a, b, preferred_element_type=F32)


def _rotary(z, cos, sin, first_half):
    up = pltpu.roll(z, LANES - ROT_HALF, 1)
    down = pltpu.roll(z, ROT_HALF, 1)
    partner = jnp.where(first_half, -up, down)
    return z * cos + partner * sin


def _inproj_kernel(x_ref, pos_ref, g_ref, inv_ref, w_ref,
                   u_ref, q_ref, k1_ref, k2_ref, v_ref, gp_ref, ga_ref):
    hb = _rms(x_ref[...], g_ref[...]).astype(BF16)
    ang = pos_ref[...] * inv_ref[...]
    cos = jnp.cos(ang)
    sin = jnp.sin(ang)
    lane = lax.broadcasted_iota(I32, ang.shape, 1)
    first_half = (lane % HEAD_DIM) < ROT_HALF
    map1 = lane < HEAD_DIM

    def col(j):
        return _dot(hb, w_ref[:, j * D_MODEL:(j + 1) * D_MODEL])

    u_ref[...] = col(0)
    zq = col(1)
    zk = col(2)
    for h in range(N_HEADS):
        sl = slice(h * HEAD_WIDTH, (h + 1) * HEAD_WIDTH)
        q_ref[:, sl] = (_rotary(zq[:, sl], cos, sin, first_half) * Q_SCALE).astype(BF16)
        kr = _rotary(zk[:, sl], cos, sin, first_half)
        k1_ref[:, sl] = jnp.where(map1, kr, 0.0).astype(BF16)
        k2_ref[:, sl] = jnp.where(map1, 0.0, kr).astype(BF16)
    v_ref[...] = col(3).astype(BF16)
    gp_ref[...] = 1.0 / (1.0 + jnp.exp(-col(4)))
    ga_ref[...] = 1.0 / (1.0 + jnp.exp(-col(5)))


def _inproj(x2d, posb, g_mix, inv_tab, w_in_bf, tm=256):
    t = x2d.shape[0]
    n_cols = w_in_bf.shape[1]
    row = lambda i: (i, 0)
    const = lambda i: (0, 0)
    tile = pl.BlockSpec((tm, D_MODEL), row)
    out_shapes = [jax.ShapeDtypeStruct((t, D_MODEL), dt) for dt in (F32, BF16, BF16, BF16, BF16, F32, F32)]
    return pl.pallas_call(
        _inproj_kernel,
        grid=(t // tm,),
        in_specs=[tile, pl.BlockSpec((tm, LANES), row), pl.BlockSpec((1, D_MODEL), const),
                  pl.BlockSpec((1, LANES), const), pl.BlockSpec((D_MODEL, n_cols), const)],
        out_specs=[tile] * 7,
        out_shape=out_shapes,
        compiler_params=_params(("parallel",)),
        name="inproj",
    )(x2d, posb, g_mix, inv_tab, w_in_bf)


def _attn_kernel(lam_init, tk, q_ref, k1_ref, k2_ref, vt_ref, lamv_ref, g_ref, o_ref, st_ref, acc_ref):
    lv = lamv_ref[...]
    lam = (jnp.exp(jnp.sum(lv[0:1] * lv[1:2], axis=-1, keepdims=True))
           - jnp.exp(jnp.sum(lv[2:3] * lv[3:4], axis=-1, keepdims=True)) + lam_init)
    q = q_ref[0]
    tq = q.shape[0]
    n_chunks = k1_ref.shape[1] // tk
    nt = (((1,), (1,)), ((), ()))
    k_refs = (k1_ref, k2_ref)

    def scores(c):
        for mp in range(2):
            st_ref[c % 2, mp] = lax.dot_general(k_refs[mp][0, c * tk:(c + 1) * tk, :], q, nt,
                                                preferred_element_type=F32)

    def accumulate(c, m, l):
        out_m, out_l = [], []
        for mp in range(2):
            st = st_ref[c % 2, mp]
            m_new = jnp.maximum(m[mp], jnp.max(st, axis=0, keepdims=True))
            alpha = jnp.exp2(m[mp] - m_new)
            et = jnp.exp2(st - m_new)
            out_l.append(l[mp] * alpha + jnp.sum(et, axis=0, keepdims=True))
            pv = _dot(vt_ref[0, :, c * tk:(c + 1) * tk], et.astype(BF16))
            acc_ref[mp] = pv if c == 0 else acc_ref[mp] * alpha + pv
            out_m.append(m_new)
        return out_m, out_l

    m = [jnp.full((1, tq), -jnp.inf, F32)] * 2
    l = [jnp.zeros((1, tq), F32)] * 2
    scores(0)
    for c in range(n_chunks):
        if c + 1 < n_chunks:
            scores(c + 1)
        m, l = accumulate(c, m, l)
    ot = acc_ref[0] * (1.0 / l[0]) - acc_ref[1] * (lam / l[1])
    o_ref[0] = (_rms(ot.T, g_ref[...]) * (1.0 - lam_init)).astype(BF16)


def _attention(q, k1, k2, vt, lamv, g_subln, lam_init, tq=512, tk=512):
    b, s, _ = q.shape
    qspec = pl.BlockSpec((1, tq, HEAD_WIDTH), lambda bi, h, qi: (bi, qi, h))
    kspec = pl.BlockSpec((1, s, HEAD_WIDTH), lambda bi, h, qi: (bi, 0, h))
    vspec = pl.BlockSpec((1, HEAD_WIDTH, s), lambda bi, h, qi: (bi, h, 0))
    const = lambda bi, h, qi: (0, 0)
    return pl.pallas_call(
        functools.partial(_attn_kernel, lam_init, tk),
        grid=(b, N_HEADS, s // tq),
        in_specs=[qspec, kspec, kspec, vspec,
                  pl.BlockSpec((8, LANES), const), pl.BlockSpec((1, HEAD_WIDTH), const)],
        out_specs=qspec,
        out_shape=jax.ShapeDtypeStruct(q.shape, BF16),
        scratch_shapes=[pltpu.VMEM((2, 2, tk, tq), F32), pltpu.VMEM((2, HEAD_WIDTH, tq), F32)],
        compiler_params=_params(("parallel", "parallel", "arbitrary")),
        name="attn",
    )(q, k1, k2, vt, lamv, g_subln)


def _mix_kernel(seq, u_ref, up_ref, un_ref, o_ref, gp_ref, ga_ref, x_ref,
                wmix_ref, ps_ref, wpo_ref, wao_ref, wo_ref, gffn_ref, wr_ref,
                x1_ref, hn_ref, lg_ref, upad_ref):
    tm = u_ref.shape[1]
    i = pl.program_id(1)
    nt = pl.num_programs(1)
    main = u_ref[0]
    upad_ref[0:POOL_HALO, :] = jnp.where(i > 0, up_ref[0], 0.0)
    upad_ref[POOL_HALO:POOL_HALO + tm, :] = main
    upad_ref[POOL_HALO + tm:, :] = jnp.where(i < nt - 1, un_ref[0], 0.0)
    t = i * tm + lax.broadcasted_iota(I32, (tm, 1), 0)
    mixed = []
    for g, w in enumerate(POOL_WINDOWS):
        cols = slice(g * POOL_GROUP_DIM, (g + 1) * POOL_GROUP_DIM)
        win = upad_ref[POOL_HALO - w // 2:POOL_HALO - w // 2 + tm, cols]
        for d in range(-w // 2 + 1, w // 2):
            win = win + upad_ref[POOL_HALO + d:POOL_HALO + d + tm, cols]
        cnt = (jnp.minimum(t + w // 2, seq) - jnp.maximum(t - w // 2, 0)).astype(F32)
        pooled = win / cnt - main[:, cols]
        mixed.append((_dot(pooled.astype(BF16), wmix_ref[g]) * ps_ref[:, cols]).astype(BF16))
    y_pool = _dot(jnp.concatenate(mixed, axis=-1), wpo_ref[...])
    y_attn = _dot(o_ref[0], wao_ref[...])
    merged = gp_ref[0] * y_pool + ga_ref[0] * y_attn
    x1 = x_ref[0] + _dot(merged.astype(BF16), wo_ref[...])
    x1_ref[0] = x1
    hn = _rms(x1, gffn_ref[...]).astype(BF16)
    hn_ref[0] = hn
    lg_ref[0] = _dot(hn, wr_ref[...])


def _mix(u, o, gp, ga, x, wmix, ps, wpo, wao, wo, gffn, wr, tm=256):
    b, s, d = x.shape
    hb = tm // POOL_HALO
    n_halo = s // POOL_HALO
    tile = pl.BlockSpec((1, tm, d), lambda bi, i: (bi, i, 0))
    prev = pl.BlockSpec((1, POOL_HALO, d), lambda bi, i: (bi, jnp.maximum(i * hb - 1, 0), 0))
    nxt = pl.BlockSpec((1, POOL_HALO, d), lambda bi, i: (bi, jnp.minimum((i + 1) * hb, n_halo - 1), 0))
    c2 = lambda bi, i: (0, 0)
    c3 = lambda bi, i: (0, 0, 0)
    return pl.pallas_call(
        functools.partial(_mix_kernel, s),
        grid=(b, s // tm),
        in_specs=[tile, prev, nxt, tile, tile, tile, tile,
                  pl.BlockSpec(wmix.shape, c3), pl.BlockSpec((1, d), c2),
                  pl.BlockSpec((d, d), c2), pl.BlockSpec((d, d), c2), pl.BlockSpec((d, d), c2),
                  pl.BlockSpec((1, d), c2), pl.BlockSpec((d, LANES), c2)],
        out_specs=[tile, tile, pl.BlockSpec((1, tm, LANES), lambda bi, i: (bi, i, 0))],
        out_shape=[jax.ShapeDtypeStruct((b, s, d), F32), jax.ShapeDtypeStruct((b, s, d), BF16),
                   jax.ShapeDtypeStruct((b, s, LANES), F32)],
        scratch_shapes=[pltpu.VMEM((tm + 2 * POOL_HALO, d), F32)],
        compiler_params=_params(("parallel", "parallel")),
        name="mix",
    )(u, u, u, o, gp, ga, x, wmix, ps, wpo, wao, wo, gffn, wr)


PREFIX_BLOCK = 256


def _prefix_rows(mask, tri):
    out = []
    carry = jnp.zeros((1, mask.shape[1]), F32)
    for blk in range(mask.shape[0] // PREFIX_BLOCK):
        m = mask[blk * PREFIX_BLOCK:(blk + 1) * PREFIX_BLOCK]
        p = _dot(tri, m.astype(BF16)) + carry
        carry = p[PREFIX_BLOCK - 1:PREFIX_BLOCK]
        out.append(p)
    return jnp.concatenate(out, axis=0)


def _route_kernel(cap, lg_ref, slot_ref, aff_ref):
    lg = lg_ref[0]
    lane = lax.broadcasted_iota(I32, lg.shape, 1)
    valid = lane < N_EXPERTS
    lg = jnp.where(valid, lg, -jnp.inf)
    ex = jnp.exp(lg - jnp.max(lg, axis=-1, keepdims=True))
    aff = ex / jnp.sum(ex, axis=-1, keepdims=True)
    aff_ref[0] = aff
    bits = lax.bitcast_convert_type(aff, I32)

    def step(k, ans):
        cand = ans | jnp.left_shift(jnp.int32(1), 30 - k)
        cnt = jnp.sum(jnp.where(bits >= cand, 1.0, 0.0), axis=0, keepdims=True)
        return jnp.where(cnt >= cap, cand, ans)

    thr = lax.fori_loop(0, 31, step, jnp.zeros((1, LANES), I32))
    gt = jnp.where(bits > thr, 1.0, 0.0)
    eq = jnp.where(bits == thr, 1.0, 0.0)
    r, c = lax.broadcasted_iota(I32, (PREFIX_BLOCK, PREFIX_BLOCK), 0), lax.broadcasted_iota(I32, (PREFIX_BLOCK, PREFIX_BLOCK), 1)
    tri = jnp.where(c <= r, 1.0, 0.0).astype(BF16)
    room = cap - jnp.sum(gt, axis=0, keepdims=True)
    eq_before = _prefix_rows(eq, tri) - eq
    sel = jnp.maximum(gt, jnp.where(eq_before < room, eq, 0.0))
    slot = _prefix_rows(sel, tri) - 1.0
    slot_ref[0] = jnp.where(valid & (sel > 0.0), slot, -1.0).astype(I32)


def _route(logits, cap):
    b, s, _ = logits.shape
    spec = pl.BlockSpec((1, s, LANES), lambda bi: (bi, 0, 0))
    return pl.pallas_call(
        functools.partial(_route_kernel, cap),
        grid=(b,),
        in_specs=[spec],
        out_specs=[spec, spec],
        out_shape=[jax.ShapeDtypeStruct((b, s, LANES), I32), jax.ShapeDtypeStruct((b, s, LANES), F32)],
        compiler_params=_params(("parallel",)),
        name="route",
    )(logits)


def _gather_kernel(cap, slot_ref, aff_ref, h_ref, xs_ref, val_ref):
    sid = slot_ref[0, 0]
    s = sid.shape[1]
    hit = sid == lax.broadcasted_iota(I32, (cap, s), 0)
    xs_ref[0] = _dot(jnp.where(hit, 1.0, 0.0).astype(BF16), h_ref[0]).astype(BF16)
    vals = jnp.sum(jnp.where(hit, aff_ref[0, 0], 0.0), axis=-1, keepdims=True)
    val_ref[0] = jnp.broadcast_to(vals, (cap, LANES))


def _gather(slot_row, aff_row, hn, cap):
    b, s, d = hn.shape
    rspec = pl.BlockSpec((1, 1, 1, s), lambda bi, e: (bi, e, 0, 0))
    return pl.pallas_call(
        functools.partial(_gather_kernel, cap),
        grid=(b, N_EXPERTS),
        in_specs=[rspec, rspec, pl.BlockSpec((1, s, d), lambda bi, e: (bi, 0, 0))],
        out_specs=[pl.BlockSpec((1, cap, d), lambda bi, e: (e, bi, 0)),
                   pl.BlockSpec((1, cap, LANES), lambda bi, e: (e, bi, 0))],
        out_shape=[jax.ShapeDtypeStruct((N_EXPERTS, b * cap, d), BF16),
                   jax.ShapeDtypeStruct((N_EXPERTS, b * cap, LANES), F32)],
        compiler_params=_params(("parallel", "arbitrary")),
        name="gather",
    )(slot_row, aff_row, hn)


def _ffn_kernel(xs_ref, wg_ref, wu_ref, wd_ref, val_ref, y_ref, acc_ref):
    f = pl.program_id(1)
    xs = xs_ref[0]
    g = _dot(xs, wg_ref[0].astype(BF16))
    u = _dot(xs, wu_ref[0].astype(BF16))
    act = (g / (1.0 + jnp.exp(-g)) * u).astype(BF16)
    part = _dot(act, wd_ref[0].astype(BF16))

    @pl.when(f == 0)
    def _():
        acc_ref[...] = part

    @pl.when(f > 0)
    def _():
        acc_ref[...] += part

    @pl.when(f == pl.num_programs(1) - 1)
    def _():
        y_ref[0] = (acc_ref[...] * val_ref[0][:, 0:1]).astype(BF16)


def _ffn(xs, w_gate, w_up, w_down, vals, tf=256):
    e, m, d = xs.shape
    ff = w_gate.shape[2]
    row = lambda ei, f: (ei, 0, 0)
    return pl.pallas_call(
        _ffn_kernel,
        grid=(e, ff // tf),
        in_specs=[pl.BlockSpec((1, m, d), row),
                  pl.BlockSpec((1, d, tf), lambda ei, f: (ei, 0, f)),
                  pl.BlockSpec((1, d, tf), lambda ei, f: (ei, 0, f)),
                  pl.BlockSpec((1, tf, d), lambda ei, f: (ei, f, 0)),
                  pl.BlockSpec((1, m, LANES), row)],
        out_specs=pl.BlockSpec((1, m, d), row),
        out_shape=jax.ShapeDtypeStruct((e, m, d), BF16),
        scratch_shapes=[pltpu.VMEM((m, d), F32)],
        compiler_params=_params(("parallel", "arbitrary")),
        name="ffn",
    )(xs, w_gate, w_up, w_down, vals)


def _combine_kernel(cap, x1_ref, slot_ref, y_ref, g_ref, o_ref):
    slot = slot_ref[0]
    tt = slot.shape[0]
    cidx = lax.broadcasted_iota(I32, (tt, cap), 1)
    acc = x1_ref[0]
    for e in range(N_EXPERTS):
        hit = slot[:, e:e + 1] == cidx
        acc = acc + _dot(jnp.where(hit, 1.0, 0.0).astype(BF16), y_ref[e])
    o_ref[0] = _rms(acc, g_ref[...])


def _combine(x1, slot_t, y, g_final, cap, tt=256):
    b, s, d = x1.shape
    tile = lambda bi, i: (bi, i, 0)
    return pl.pallas_call(
        functools.partial(_combine_kernel, cap),
        grid=(b, s // tt),
        in_specs=[pl.BlockSpec((1, tt, d), tile), pl.BlockSpec((1, tt, LANES), tile),
                  pl.BlockSpec((N_EXPERTS, cap, d), lambda bi, i: (0, bi, 0)),
                  pl.BlockSpec((1, d), lambda bi, i: (0, 0))],
        out_specs=pl.BlockSpec((1, tt, d), tile),
        out_shape=jax.ShapeDtypeStruct((b, s, d), F32),
        compiler_params=_params(("parallel", "arbitrary")),
        name="combine",
    )(x1, slot_t, y, g_final)


def _layer(x, positions, g_mix, w_in, w_pool_mix, pool_scale, w_pool_out, lam_q1, lam_k1, lam_q2,
           lam_k2, g_subln, w_attn_out, w_out, g_ffn, w_router, w_gate, w_up, w_down, lam_init):
    b, s, d = x.shape
    t = b * s
    cap = EC_CAPACITY_FACTOR * s // N_EXPERTS

    posb = jnp.broadcast_to(positions.reshape(t, 1).astype(F32), (t, LANES))
    inv = ROPE_THETA ** (-jnp.arange(ROT_HALF, dtype=F32) * 2.0 / ROT_DIM)
    lane = jnp.arange(LANES)
    inv_tab = jnp.where((lane % HEAD_DIM) < ROT_DIM, inv[lane % ROT_HALF], 0.0).reshape(1, LANES).astype(F32)

    u, q, k1, k2, v, gp, ga = _inproj(x.reshape(t, d), posb, g_mix.reshape(1, d), inv_tab, w_in.astype(BF16))
    sh = lambda a: a.reshape(b, s, d)

    lamv = jnp.zeros((8, LANES), F32).at[0:4, 0:HEAD_DIM].set(jnp.stack([lam_q1, lam_k1, lam_q2, lam_k2]).astype(F32))
    vt = jnp.transpose(sh(v), (0, 2, 1))
    o = _attention(sh(q), sh(k1), sh(k2), vt, lamv, g_subln.reshape(1, HEAD_WIDTH), lam_init)

    wr = jnp.zeros((d, LANES), BF16).at[:, :N_EXPERTS].set(w_router.astype(BF16))
    x1, hn, logits = _mix(sh(u), o, sh(gp), sh(ga), x, w_pool_mix.astype(BF16), pool_scale.reshape(1, d),
                          w_pool_out.astype(BF16), w_attn_out.astype(BF16), w_out.astype(BF16),
                          g_ffn.reshape(1, d), wr)

    slot_t, aff_t = _route(logits, cap)
    to_rows = lambda a: jnp.transpose(a[:, :, :N_EXPERTS], (0, 2, 1)).reshape(b, N_EXPERTS, 1, s)
    xs, vals = _gather(to_rows(slot_t), to_rows(aff_t), hn, cap)
    y = _ffn(xs, w_gate, w_up, w_down, vals)
    return x1, slot_t, y, cap


def kernel(x, positions, g_mix, w_in, w_pool_mix, pool_scale, w_pool_out, lam_q1, lam_k1, lam_q2, lam_k2,
           g_subln, w_attn_out, w_out, g_ffn, w_router, w_gate, w_up, w_down, g_final):
    depth = g_mix.shape[0]
    assert depth == 1, "the final norm is fused into the last layer's combine step"
    l = 0
    lam_init = 0.8 - 0.6 * math.exp(-0.3 * l)
    x1, slot_t, y, cap = _layer(x, positions, g_mix[l], w_in[l], w_pool_mix[l], pool_scale[l], w_pool_out[l],
                                lam_q1[l], lam_k1[l], lam_q2[l], lam_k2[l], g_subln[l], w_attn_out[l],
                                w_out[l], g_ffn[l], w_router[l], w_gate[l], w_up[l], w_down[l], lam_init)
    return _combine(x1, slot_t, y, g_final.reshape(1, D_MODEL), cap)
```

```python
import functools
import math

import jax
import jax.numpy as jnp
from jax import lax
from jax.experimental import pallas as pl
from jax.experimental.pallas import tpu as pltpu

F32 = jnp.float32
BF16 = jnp.bfloat16
I32 = jnp.int32

D_MODEL = 1024
N_HEADS = 8
HEAD_DIM = 64
HEAD_WIDTH = 2 * HEAD_DIM
ROT_DIM = HEAD_DIM // 4
ROT_HALF = ROT_DIM // 2
ROPE_THETA = 500000.0
POOL_WINDOWS = (2, 4, 8, 16)
POOL_GROUP_DIM = D_MODEL // len(POOL_WINDOWS)
POOL_HALO = 16
N_EXPERTS = 16
EC_CAPACITY_FACTOR = 2
RMS_EPS = 1e-6
Q_SCALE = HEAD_DIM ** -0.5 * math.log2(math.e)
LANES = 128

VMEM_LIMIT = 56 * 1024 * 1024


def _params(sem):
    return pltpu.CompilerParams(dimension_semantics=sem, vmem_limit_bytes=VMEM_LIMIT)


def _rms(x, g):
    return (x * lax.rsqrt(jnp.mean(x * x, axis=-1, keepdims=True) + RMS_EPS)) * g


def _dot(a, b):
    return jnp.dot(a, b, preferred_element_type=F32)


def _rotary(z, cos, sin, first_half):
    up = pltpu.roll(z, LANES - ROT_HALF, 1)
    down = pltpu.roll(z, ROT_HALF, 1)
    partner = jnp.where(first_half, -up, down)
    return z * cos + partner * sin


def _inproj_kernel(x_ref, pos_ref, g_ref, inv_ref, w_ref,
                   u_ref, q_ref, k1_ref, k2_ref, v_ref, gp_ref, ga_ref):
    hb = _rms(x_ref[...], g_ref[...]).astype(BF16)
    ang = pos_ref[...] * inv_ref[...]
    cos = jnp.cos(ang)
    sin = jnp.sin(ang)
    lane = lax.broadcasted_iota(I32, ang.shape, 1)
    first_half = (lane % HEAD_DIM) < ROT_HALF
    map1 = lane < HEAD_DIM

    def col(j):
        return _dot(hb, w_ref[:, j * D_MODEL:(j + 1) * D_MODEL])

    u_ref[...] = col(0)
    zq = col(1)
    zk = col(2)
    for h in range(N_HEADS):
        sl = slice(h * HEAD_WIDTH, (h + 1) * HEAD_WIDTH)
        q_ref[:, sl] = (_rotary(zq[:, sl], cos, sin, first_half) * Q_SCALE).astype(BF16)
        kr = _rotary(zk[:, sl], cos, sin, first_half)
        k1_ref[:, sl] = jnp.where(map1, kr, 0.0).astype(BF16)
        k2_ref[:, sl] = jnp.where(map1, 0.0, kr).astype(BF16)
    v_ref[...] = col(3).astype(BF16)
    gp_ref[...] = 1.0 / (1.0 + jnp.exp(-col(4)))
    ga_ref[...] = 1.0 / (1.0 + jnp.exp(-col(5)))


def _inproj(x2d, posb, g_mix, inv_tab, w_in_bf, tm=256):
    t = x2d.shape[0]
    n_cols = w_in_bf.shape[1]
    row = lambda i: (i, 0)
    const = lambda i: (0, 0)
    tile = pl.BlockSpec((tm, D_MODEL), row)
    out_shapes = [jax.ShapeDtypeStruct((t, D_MODEL), dt) for dt in (F32, BF16, BF16, BF16, BF16, F32, F32)]
    return pl.pallas_call(
        _inproj_kernel,
        grid=(t // tm,),
        in_specs=[tile, pl.BlockSpec((tm, LANES), row), pl.BlockSpec((1, D_MODEL), const),
                  pl.BlockSpec((1, LANES), const), pl.BlockSpec((D_MODEL, n_cols), const)],
        out_specs=[tile] * 7,
        out_shape=out_shapes,
        compiler_params=_params(("parallel",)),
        name="inproj",
    )(x2d, posb, g_mix, inv_tab, w_in_bf)


SCORE_SLOTS = 2
SUM_ROWS = 16


def _attn_kernel(lam_init, tk, q_ref, k1_ref, k2_ref, vt_ref, lamv_ref, g_ref, o_ref, st_ref, acc_ref):
    lv = lamv_ref[...]
    lam = (jnp.exp(jnp.sum(lv[0:1] * lv[1:2], axis=-1, keepdims=True))
           - jnp.exp(jnp.sum(lv[2:3] * lv[3:4], axis=-1, keepdims=True)) + lam_init)
    q = q_ref[0]
    tq = q.shape[0]
    n_chunks = k1_ref.shape[1] // tk
    nt = (((1,), (1,)), ((), ()))
    k_refs = (k1_ref, k2_ref)

    def scores(c):
        for mp in range(2):
            st_ref[c % SCORE_SLOTS, mp] = lax.dot_general(k_refs[mp][0, c * tk:(c + 1) * tk, :], q, nt,
                                                preferred_element_type=F32)

    ones_rows = jnp.ones((SUM_ROWS, tk), BF16)

    def accumulate(c, m):
        out_m = []
        for mp in range(2):
            st = st_ref[c % SCORE_SLOTS, mp]
            m_new = jnp.maximum(m[mp], jnp.max(st, axis=0, keepdims=True))
            alpha = jnp.exp2(m[mp] - m_new)
            et = jnp.exp2(st - m_new).astype(BF16)
            vt_ext = jnp.concatenate([vt_ref[0, :, c * tk:(c + 1) * tk], ones_rows], axis=0)
            pv = _dot(vt_ext, et)
            acc_ref[mp] = pv if c == 0 else acc_ref[mp] * alpha + pv
            out_m.append(m_new)
        return out_m

    m = [jnp.full((1, tq), -jnp.inf, F32)] * 2
    scores(0)
    for c in range(n_chunks):
        if c + 1 < n_chunks:
            scores(c + 1)
        m = accumulate(c, m)
    a1, a2 = acc_ref[0], acc_ref[1]
    r1 = 1.0 / a1[HEAD_WIDTH:HEAD_WIDTH + 1]
    r2 = lam / a2[HEAD_WIDTH:HEAD_WIDTH + 1]
    ot = a1[:HEAD_WIDTH] * r1 - a2[:HEAD_WIDTH] * r2
    o_ref[0] = (_rms(ot.T, g_ref[...]) * (1.0 - lam_init)).astype(BF16)


def _attention(q, k1, k2, vt, lamv, g_subln, lam_init, tq=512, tk=512):
    b, s, _ = q.shape
    qspec = pl.BlockSpec((1, tq, HEAD_WIDTH), lambda bi, h, qi: (bi, qi, h))
    kspec = pl.BlockSpec((1, s, HEAD_WIDTH), lambda bi, h, qi: (bi, 0, h))
    vspec = pl.BlockSpec((1, HEAD_WIDTH, s), lambda bi, h, qi: (bi, h, 0))
    const = lambda bi, h, qi: (0, 0)
    return pl.pallas_call(
        functools.partial(_attn_kernel, lam_init, tk),
        grid=(b, N_HEADS, s // tq),
        in_specs=[qspec, kspec, kspec, vspec,
                  pl.BlockSpec((8, LANES), const), pl.BlockSpec((1, HEAD_WIDTH), const)],
        out_specs=qspec,
        out_shape=jax.ShapeDtypeStruct(q.shape, BF16),
        scratch_shapes=[pltpu.VMEM((SCORE_SLOTS, 2, tk, tq), F32), pltpu.VMEM((2, HEAD_WIDTH + SUM_ROWS, tq), F32)],
        compiler_params=_params(("parallel", "parallel", "arbitrary")),
        name="attn",
    )(q, k1, k2, vt, lamv, g_subln)


def _mix_kernel(seq, u_ref, up_ref, un_ref, o_ref, gp_ref, ga_ref, x_ref,
                wmix_ref, ps_ref, wpo_ref, wao_ref, wo_ref, gffn_ref, wr_ref,
                x1_ref, hn_ref, lg_ref, upad_ref):
    tm = u_ref.shape[1]
    i = pl.program_id(1)
    nt = pl.num_programs(1)
    main = u_ref[0]
    upad_ref[0:POOL_HALO, :] = jnp.where(i > 0, up_ref[0], 0.0)
    upad_ref[POOL_HALO:POOL_HALO + tm, :] = main
    upad_ref[POOL_HALO + tm:, :] = jnp.where(i < nt - 1, un_ref[0], 0.0)
    t = i * tm + lax.broadcasted_iota(I32, (tm, 1), 0)
    mixed = []
    for g, w in enumerate(POOL_WINDOWS):
        cols = slice(g * POOL_GROUP_DIM, (g + 1) * POOL_GROUP_DIM)
        win = upad_ref[POOL_HALO - w // 2:POOL_HALO - w // 2 + tm, cols]
        for d in range(-w // 2 + 1, w // 2):
            win = win + upad_ref[POOL_HALO + d:POOL_HALO + d + tm, cols]
        cnt = (jnp.minimum(t + w // 2, seq) - jnp.maximum(t - w // 2, 0)).astype(F32)
        pooled = win / cnt - main[:, cols]
        mixed.append((_dot(pooled.astype(BF16), wmix_ref[g]) * ps_ref[:, cols]).astype(BF16))
    y_pool = _dot(jnp.concatenate(mixed, axis=-1), wpo_ref[...])
    y_attn = _dot(o_ref[0], wao_ref[...])
    merged = gp_ref[0] * y_pool + ga_ref[0] * y_attn
    x1 = x_ref[0] + _dot(merged.astype(BF16), wo_ref[...])
    x1_ref[0] = x1
    hn = _rms(x1, gffn_ref[...]).astype(BF16)
    hn_ref[0] = hn
    lg_ref[0] = _dot(hn, wr_ref[...])


def _mix(u, o, gp, ga, x, wmix, ps, wpo, wao, wo, gffn, wr, tm=256):
    b, s, d = x.shape
    hb = tm // POOL_HALO
    n_halo = s // POOL_HALO
    tile = pl.BlockSpec((1, tm, d), lambda bi, i: (bi, i, 0))
    prev = pl.BlockSpec((1, POOL_HALO, d), lambda bi, i: (bi, jnp.maximum(i * hb - 1, 0), 0))
    nxt = pl.BlockSpec((1, POOL_HALO, d), lambda bi, i: (bi, jnp.minimum((i + 1) * hb, n_halo - 1), 0))
    c2 = lambda bi, i: (0, 0)
    c3 = lambda bi, i: (0, 0, 0)
    return pl.pallas_call(
        functools.partial(_mix_kernel, s),
        grid=(b, s // tm),
        in_specs=[tile, prev, nxt, tile, tile, tile, tile,
                  pl.BlockSpec(wmix.shape, c3), pl.BlockSpec((1, d), c2),
                  pl.BlockSpec((d, d), c2), pl.BlockSpec((d, d), c2), pl.BlockSpec((d, d), c2),
                  pl.BlockSpec((1, d), c2), pl.BlockSpec((d, LANES), c2)],
        out_specs=[tile, tile, pl.BlockSpec((1, tm, LANES), lambda bi, i: (bi, i, 0))],
        out_shape=[jax.ShapeDtypeStruct((b, s, d), F32), jax.ShapeDtypeStruct((b, s, d), BF16),
                   jax.ShapeDtypeStruct((b, s, LANES), F32)],
        scratch_shapes=[pltpu.VMEM((tm + 2 * POOL_HALO, d), F32)],
        compiler_params=_params(("parallel", "parallel")),
        name="mix",
    )(u, u, u, o, gp, ga, x, wmix, ps, wpo, wao, wo, gffn, wr)


PREFIX_BLOCK = 256


def _prefix_rows(mask, tri):
    out = []
    carry = jnp.zeros((1, mask.shape[1]), F32)
    for blk in range(mask.shape[0] // PREFIX_BLOCK):
        m = mask[blk * PREFIX_BLOCK:(blk + 1) * PREFIX_BLOCK]
        p = _dot(tri, m.astype(BF16)) + carry
        carry = p[PREFIX_BLOCK - 1:PREFIX_BLOCK]
        out.append(p)
    return jnp.concatenate(out, axis=0)


def _route_kernel(cap, lg_ref, slot_ref, aff_ref):
    lg = lg_ref[0]
    lane = lax.broadcasted_iota(I32, lg.shape, 1)
    valid = lane < N_EXPERTS
    lg = jnp.where(valid, lg, -jnp.inf)
    ex = jnp.exp(lg - jnp.max(lg, axis=-1, keepdims=True))
    aff = ex / jnp.sum(ex, axis=-1, keepdims=True)
    aff_ref[0] = aff
    bits = lax.bitcast_convert_type(aff, I32)

    def step(k, ans):
        cand = ans | jnp.left_shift(jnp.int32(1), 30 - k)
        cnt = jnp.sum(jnp.where(bits >= cand, 1.0, 0.0), axis=0, keepdims=True)
        return jnp.where(cnt >= cap, cand, ans)

    thr = lax.fori_loop(0, 31, step, jnp.zeros((1, LANES), I32))
    gt = jnp.where(bits > thr, 1.0, 0.0)
    eq = jnp.where(bits == thr, 1.0, 0.0)
    r, c = lax.broadcasted_iota(I32, (PREFIX_BLOCK, PREFIX_BLOCK), 0), lax.broadcasted_iota(I32, (PREFIX_BLOCK, PREFIX_BLOCK), 1)
    tri = jnp.where(c <= r, 1.0, 0.0).astype(BF16)
    room = cap - jnp.sum(gt, axis=0, keepdims=True)
    eq_before = _prefix_rows(eq, tri) - eq
    sel = jnp.maximum(gt, jnp.where(eq_before < room, eq, 0.0))
    slot = _prefix_rows(sel, tri) - 1.0
    slot_ref[0] = jnp.where(valid & (sel > 0.0), slot, -1.0).astype(I32)


def _route(logits, cap):
    b, s, _ = logits.shape
    spec = pl.BlockSpec((1, s, LANES), lambda bi: (bi, 0, 0))
    return pl.pallas_call(
        functools.partial(_route_kernel, cap),
        grid=(b,),
        in_specs=[spec],
        out_specs=[spec, spec],
        out_shape=[jax.ShapeDtypeStruct((b, s, LANES), I32), jax.ShapeDtypeStruct((b, s, LANES), F32)],
        compiler_params=_params(("parallel",)),
        name="route",
    )(logits)


def _gather_kernel(cap, slot_ref, aff_ref, h_ref, xs_ref, val_ref):
    sid = slot_ref[0, 0]
    s = sid.shape[1]
    hit = sid == lax.broadcasted_iota(I32, (cap, s), 0)
    xs_ref[0] = _dot(jnp.where(hit, 1.0, 0.0).astype(BF16), h_ref[0]).astype(BF16)
    vals = jnp.sum(jnp.where(hit, aff_ref[0, 0], 0.0), axis=-1, keepdims=True)
    val_ref[0] = jnp.broadcast_to(vals, (cap, LANES))


def _gather(slot_row, aff_row, hn, cap):
    b, s, d = hn.shape
    rspec = pl.BlockSpec((1, 1, 1, s), lambda bi, e: (bi, e, 0, 0))
    return pl.pallas_call(
        functools.partial(_gather_kernel, cap),
        grid=(b, N_EXPERTS),
        in_specs=[rspec, rspec, pl.BlockSpec((1, s, d), lambda bi, e: (bi, 0, 0))],
        out_specs=[pl.BlockSpec((1, cap, d), lambda bi, e: (e, bi, 0)),
                   pl.BlockSpec((1, cap, LANES), lambda bi, e: (e, bi, 0))],
        out_shape=[jax.ShapeDtypeStruct((N_EXPERTS, b * cap, d), BF16),
                   jax.ShapeDtypeStruct((N_EXPERTS, b * cap, LANES), F32)],
        compiler_params=_params(("parallel", "arbitrary")),
        name="gather",
    )(slot_row, aff_row, hn)


def _ffn_kernel(xs_ref, wg_ref, wu_ref, wd_ref, val_ref, y_ref, acc_ref):
    f = pl.program_id(1)

    @pl.when(f == 0)
    def _():
        acc_ref[...] = jnp.zeros_like(acc_ref)

    xs = xs_ref[0]
    g = _dot(xs, wg_ref[0].astype(BF16))
    u = _dot(xs, wu_ref[0].astype(BF16))
    act = (g / (1.0 + jnp.exp(-g)) * u).astype(BF16)
    acc_ref[...] += _dot(act, wd_ref[0].astype(BF16))

    @pl.when(f == pl.num_programs(1) - 1)
    def _():
        y_ref[0] = (acc_ref[...] * val_ref[0][:, 0:1]).astype(BF16)


def _ffn(xs, w_gate, w_up, w_down, vals, tf=256):
    e, m, d = xs.shape
    ff = w_gate.shape[2]
    row = lambda ei, f: (ei, 0, 0)
    return pl.pallas_call(
        _ffn_kernel,
        grid=(e, ff // tf),
        in_specs=[pl.BlockSpec((1, m, d), row),
                  pl.BlockSpec((1, d, tf), lambda ei, f: (ei, 0, f)),
                  pl.BlockSpec((1, d, tf), lambda ei, f: (ei, 0, f)),
                  pl.BlockSpec((1, tf, d), lambda ei, f: (ei, f, 0)),
                  pl.BlockSpec((1, m, LANES), row)],
        out_specs=pl.BlockSpec((1, m, d), row),
        out_shape=jax.ShapeDtypeStruct((e, m, d), BF16),
        scratch_shapes=[pltpu.VMEM((m, d), F32)],
        compiler_params=_params(("parallel", "arbitrary")),
        name="ffn",
    )(xs, w_gate, w_up, w_down, vals)


def _combine_kernel(cap, x1_ref, slot_ref, y_ref, g_ref, o_ref):
    slot = slot_ref[0]
    tt = slot.shape[0]
    cidx = lax.broadcasted_iota(I32, (tt, cap), 1)
    acc = x1_ref[0]
    for e in range(N_EXPERTS):
        hit = slot[:, e:e + 1] == cidx
        acc = acc + _dot(jnp.where(hit, 1.0, 0.0).astype(BF16), y_ref[e])
    o_ref[0] = _rms(acc, g_ref[...])


def _combine(x1, slot_t, y, g_final, cap, tt=256):
    b, s, d = x1.shape
    tile = lambda bi, i: (bi, i, 0)
    return pl.pallas_call(
        functools.partial(_combine_kernel, cap),
        grid=(b, s // tt),
        in_specs=[pl.BlockSpec((1, tt, d), tile), pl.BlockSpec((1, tt, LANES), tile),
                  pl.BlockSpec((N_EXPERTS, cap, d), lambda bi, i: (0, bi, 0)),
                  pl.BlockSpec((1, d), lambda bi, i: (0, 0))],
        out_specs=pl.BlockSpec((1, tt, d), tile),
        out_shape=jax.ShapeDtypeStruct((b, s, d), F32),
        compiler_params=_params(("parallel", "arbitrary")),
        name="combine",
    )(x1, slot_t, y, g_final)


def _layer(x, positions, g_mix, w_in, w_pool_mix, pool_scale, w_pool_out, lam_q1, lam_k1, lam_q2,
           lam_k2, g_subln, w_attn_out, w_out, g_ffn, w_router, w_gate, w_up, w_down, lam_init):
    b, s, d = x.shape
    t = b * s
    cap = EC_CAPACITY_FACTOR * s // N_EXPERTS

    posb = jnp.broadcast_to(positions.reshape(t, 1).astype(F32), (t, LANES))
    inv = ROPE_THETA ** (-jnp.arange(ROT_HALF, dtype=F32) * 2.0 / ROT_DIM)
    lane = jnp.arange(LANES)
    inv_tab = jnp.where((lane % HEAD_DIM) < ROT_DIM, inv[lane % ROT_HALF], 0.0).reshape(1, LANES).astype(F32)

    u, q, k1, k2, v, gp, ga = _inproj(x.reshape(t, d), posb, g_mix.reshape(1, d), inv_tab, w_in.astype(BF16))
    sh = lambda a: a.reshape(b, s, d)

    lamv = jnp.zeros((8, LANES), F32).at[0:4, 0:HEAD_DIM].set(jnp.stack([lam_q1, lam_k1, lam_q2, lam_k2]).astype(F32))
    vt = jnp.transpose(sh(v), (0, 2, 1))
    o = _attention(sh(q), sh(k1), sh(k2), vt, lamv, g_subln.reshape(1, HEAD_WIDTH), lam_init)

    wr = jnp.zeros((d, LANES), BF16).at[:, :N_EXPERTS].set(w_router.astype(BF16))
    x1, hn, logits = _mix(sh(u), o, sh(gp), sh(ga), x, w_pool_mix.astype(BF16), pool_scale.reshape(1, d),
                          w_pool_out.astype(BF16), w_attn_out.astype(BF16), w_out.astype(BF16),
                          g_ffn.reshape(1, d), wr)

    slot_t, aff_t = _route(logits, cap)
    to_rows = lambda a: jnp.transpose(a[:, :, :N_EXPERTS], (0, 2, 1)).reshape(b, N_EXPERTS, 1, s)
    xs, vals = _gather(to_rows(slot_t), to_rows(aff_t), hn, cap)
    y = _ffn(xs, w_gate, w_up, w_down, vals)
    return x1, slot_t, y, cap


def kernel(x, positions, g_mix, w_in, w_pool_mix, pool_scale, w_pool_out, lam_q1, lam_k1, lam_q2, lam_k2,
           g_subln, w_attn_out, w_out, g_ffn, w_router, w_gate, w_up, w_down, g_final):
    depth = g_mix.shape[0]
    assert depth == 1, "the final norm is fused into the last layer's combine step"
    l = 0
    lam_init = 0.8 - 0.6 * math.exp(-0.3 * l)
    x1, slot_t, y, cap = _layer(x, positions, g_mix[l], w_in[l], w_pool_mix[l], pool_scale[l], w_pool_out[l],
                                lam_q1[l], lam_k1[l], lam_q2[l], lam_k2[l], g_subln[l], w_attn_out[l],
                                w_out[l], g_ffn[l], w_router[l], w_gate[l], w_up[l], w_down[l], lam_init)
    return _combine(x1, slot_t, y, g_final.reshape(1, D_MODEL), cap)
```

```python
import functools
import math

import jax
import jax.numpy as jnp
from jax import lax
from jax.experimental import pallas as pl
from jax.experimental.pallas import tpu as pltpu

F32 = jnp.float32
BF16 = jnp.bfloat16
I32 = jnp.int32

D_MODEL = 1024
N_HEADS = 8
HEAD_DIM = 64
HEAD_WIDTH = 2 * HEAD_DIM
ROT_DIM = HEAD_DIM // 4
ROT_HALF = ROT_DIM // 2
ROPE_THETA = 500000.0
POOL_WINDOWS = (2, 4, 8, 16)
POOL_GROUP_DIM = D_MODEL // len(POOL_WINDOWS)
POOL_HALO = 16
N_EXPERTS = 16
EC_CAPACITY_FACTOR = 2
RMS_EPS = 1e-6
Q_SCALE = HEAD_DIM ** -0.5 * math.log2(math.e)
LANES = 128
SUBLANES = 8

VMEM_LIMIT = 56 * 1024 * 1024


def _params(sem):
    return pltpu.CompilerParams(dimension_semantics=sem, vmem_limit_bytes=VMEM_LIMIT)


def _rms(x, g):
    return (x * lax.rsqrt(jnp.mean(x * x, axis=-1, keepdims=True) + RMS_EPS)) * g


def _dot(a, b):
    return jnp.dot(a, b, preferred_element_type=F32)


def _rotary(z, cos, sin, first_half):
    up = pltpu.roll(z, LANES - ROT_HALF, 1)
    down = pltpu.roll(z, ROT_HALF, 1)
    partner = jnp.where(first_half, -up, down)
    return z * cos + partner * sin


def _inproj_kernel(x_ref, pos_ref, g_ref, inv_ref, w_ref,
                   u_ref, q_ref, k1_ref, k2_ref, v_ref, gp_ref, ga_ref):
    hb = _rms(x_ref[...], g_ref[...]).astype(BF16)
    ang = pos_ref[...] * inv_ref[...]
    cos = jnp.cos(ang)
    sin = jnp.sin(ang)
    lane = lax.broadcasted_iota(I32, ang.shape, 1)
    first_half = (lane % HEAD_DIM) < ROT_HALF
    map1 = lane < HEAD_DIM

    def col(j):
        return _dot(hb, w_ref[:, j * D_MODEL:(j + 1) * D_MODEL])

    u_ref[...] = col(0)
    zq = col(1)
    zk = col(2)
    for h in range(N_HEADS):
        sl = slice(h * HEAD_WIDTH, (h + 1) * HEAD_WIDTH)
        q_ref[:, sl] = (_rotary(zq[:, sl], cos, sin, first_half) * Q_SCALE).astype(BF16)
        kr = _rotary(zk[:, sl], cos, sin, first_half)
        k1_ref[:, sl] = jnp.where(map1, kr, 0.0).astype(BF16)
        k2_ref[:, sl] = jnp.where(map1, 0.0, kr).astype(BF16)
    v_ref[...] = col(3).astype(BF16)
    gp_ref[...] = 1.0 / (1.0 + jnp.exp(-col(4)))
    ga_ref[...] = 1.0 / (1.0 + jnp.exp(-col(5)))


def _inproj(x2d, posb, g_mix, inv_tab, w_in_bf, tm=256):
    t = x2d.shape[0]
    n_cols = w_in_bf.shape[1]
    row = lambda i: (i, 0)
    const = lambda i: (0, 0)
    tile = pl.BlockSpec((tm, D_MODEL), row)
    out_shapes = [jax.ShapeDtypeStruct((t, D_MODEL), dt) for dt in (F32, BF16, BF16, BF16, BF16, F32, F32)]
    return pl.pallas_call(
        _inproj_kernel,
        grid=(t // tm,),
        in_specs=[tile, pl.BlockSpec((tm, LANES), row), pl.BlockSpec((1, D_MODEL), const),
                  pl.BlockSpec((1, LANES), const), pl.BlockSpec((D_MODEL, n_cols), const)],
        out_specs=[tile] * 7,
        out_shape=out_shapes,
        compiler_params=_params(("parallel",)),
        name="inproj",
    )(x2d, posb, g_mix, inv_tab, w_in_bf)


SCORE_SLOTS = 2
SUM_ROWS = 16


def _attn_kernel(lam_init, tk, q_ref, k1_ref, k2_ref, vt_ref, lamv_ref, g_ref, o_ref, st_ref, acc_ref):
    lv = lamv_ref[...]
    lam = (jnp.exp(jnp.sum(lv[0:1] * lv[1:2], axis=-1, keepdims=True))
           - jnp.exp(jnp.sum(lv[2:3] * lv[3:4], axis=-1, keepdims=True)) + lam_init)
    q = q_ref[0]
    tq = q.shape[0]
    n_chunks = k1_ref.shape[1] // tk
    nt = (((1,), (1,)), ((), ()))
    k_refs = (k1_ref, k2_ref)

    def scores(c):
        for mp in range(2):
            st_ref[c % SCORE_SLOTS, mp] = lax.dot_general(k_refs[mp][0, c * tk:(c + 1) * tk, :], q, nt,
                                                          preferred_element_type=F32)

    ones_rows = jnp.ones((SUM_ROWS, tk), BF16)

    def accumulate(c, m):
        out_m = []
        for mp in range(2):
            st = st_ref[c % SCORE_SLOTS, mp]
            m_new = jnp.maximum(m[mp], jnp.max(st, axis=0, keepdims=True))
            alpha = jnp.exp2(m[mp] - m_new)
            et = jnp.exp2(st - m_new).astype(BF16)
            vt_ext = jnp.concatenate([vt_ref[0, :, c * tk:(c + 1) * tk], ones_rows], axis=0)
            pv = _dot(vt_ext, et)
            acc_ref[mp] = pv if c == 0 else acc_ref[mp] * alpha + pv
            out_m.append(m_new)
        return out_m

    m = [jnp.full((1, tq), -jnp.inf, F32)] * 2
    scores(0)
    for c in range(n_chunks):
        if c + 1 < n_chunks:
            scores(c + 1)
        m = accumulate(c, m)
    a1, a2 = acc_ref[0], acc_ref[1]
    r1 = 1.0 / a1[HEAD_WIDTH:HEAD_WIDTH + 1]
    r2 = lam / a2[HEAD_WIDTH:HEAD_WIDTH + 1]
    ot = a1[:HEAD_WIDTH] * r1 - a2[:HEAD_WIDTH] * r2
    o_ref[0] = (_rms(ot.T, g_ref[...]) * (1.0 - lam_init)).astype(BF16)


def _attention(q, k1, k2, vt, lamv, g_subln, lam_init, tq=512, tk=512):
    b, s, _ = q.shape
    qspec = pl.BlockSpec((1, tq, HEAD_WIDTH), lambda bi, h, qi: (bi, qi, h))
    kspec = pl.BlockSpec((1, s, HEAD_WIDTH), lambda bi, h, qi: (bi, 0, h))
    vspec = pl.BlockSpec((1, HEAD_WIDTH, s), lambda bi, h, qi: (bi, h, 0))
    const = lambda bi, h, qi: (0, 0)
    return pl.pallas_call(
        functools.partial(_attn_kernel, lam_init, tk),
        grid=(b, N_HEADS, s // tq),
        in_specs=[qspec, kspec, kspec, vspec,
                  pl.BlockSpec((8, LANES), const), pl.BlockSpec((1, HEAD_WIDTH), const)],
        out_specs=qspec,
        out_shape=jax.ShapeDtypeStruct(q.shape, BF16),
        scratch_shapes=[pltpu.VMEM((SCORE_SLOTS, 2, tk, tq), F32), pltpu.VMEM((2, HEAD_WIDTH + SUM_ROWS, tq), F32)],
        compiler_params=_params(("parallel", "parallel", "arbitrary")),
        name="attn",
    )(q, k1, k2, vt, lamv, g_subln)


def _mix_kernel(seq, u_ref, up_ref, un_ref, o_ref, gp_ref, ga_ref, x_ref,
                wmix_ref, ps_ref, wpo_ref, wao_ref, wo_ref, gffn_ref, wr_ref,
                x1_ref, lg_ref, upad_ref):
    tm = u_ref.shape[1]
    i = pl.program_id(1)
    nt = pl.num_programs(1)
    main = u_ref[0]
    upad_ref[0:POOL_HALO, :] = jnp.where(i > 0, up_ref[0], 0.0)
    upad_ref[POOL_HALO:POOL_HALO + tm, :] = main
    upad_ref[POOL_HALO + tm:, :] = jnp.where(i < nt - 1, un_ref[0], 0.0)
    t = i * tm + lax.broadcasted_iota(I32, (tm, 1), 0)
    mixed = []
    for g, w in enumerate(POOL_WINDOWS):
        cols = slice(g * POOL_GROUP_DIM, (g + 1) * POOL_GROUP_DIM)
        win = upad_ref[POOL_HALO - w // 2:POOL_HALO - w // 2 + tm, cols]
        for d in range(-w // 2 + 1, w // 2):
            win = win + upad_ref[POOL_HALO + d:POOL_HALO + d + tm, cols]
        cnt = (jnp.minimum(t + w // 2, seq) - jnp.maximum(t - w // 2, 0)).astype(F32)
        pooled = win / cnt - main[:, cols]
        mixed.append((_dot(pooled.astype(BF16), wmix_ref[g]) * ps_ref[:, cols]).astype(BF16))
    y_pool = _dot(jnp.concatenate(mixed, axis=-1), wpo_ref[...])
    y_attn = _dot(o_ref[0], wao_ref[...])
    merged = gp_ref[0] * y_pool + ga_ref[0] * y_attn
    x1 = x_ref[0] + _dot(merged.astype(BF16), wo_ref[...])
    x1_ref[0] = x1
    lg_ref[0] = _dot(_rms(x1, gffn_ref[...]).astype(BF16), wr_ref[...])


def _mix(u, o, gp, ga, x, wmix, ps, wpo, wao, wo, gffn, wr, tm=256):
    b, s, d = x.shape
    hb = tm // POOL_HALO
    n_halo = s // POOL_HALO
    tile = pl.BlockSpec((1, tm, d), lambda bi, i: (bi, i, 0))
    prev = pl.BlockSpec((1, POOL_HALO, d), lambda bi, i: (bi, jnp.maximum(i * hb - 1, 0), 0))
    nxt = pl.BlockSpec((1, POOL_HALO, d), lambda bi, i: (bi, jnp.minimum((i + 1) * hb, n_halo - 1), 0))
    c2 = lambda bi, i: (0, 0)
    c3 = lambda bi, i: (0, 0, 0)
    return pl.pallas_call(
        functools.partial(_mix_kernel, s),
        grid=(b, s // tm),
        in_specs=[tile, prev, nxt, tile, tile, tile, tile,
                  pl.BlockSpec(wmix.shape, c3), pl.BlockSpec((1, d), c2),
                  pl.BlockSpec((d, d), c2), pl.BlockSpec((d, d), c2), pl.BlockSpec((d, d), c2),
                  pl.BlockSpec((1, d), c2), pl.BlockSpec((d, LANES), c2)],
        out_specs=[tile, pl.BlockSpec((1, tm, LANES), lambda bi, i: (bi, i, 0))],
        out_shape=[jax.ShapeDtypeStruct((b, s, d), F32), jax.ShapeDtypeStruct((b, s, LANES), F32)],
        scratch_shapes=[pltpu.VMEM((tm + 2 * POOL_HALO, d), F32)],
        compiler_params=_params(("parallel", "parallel")),
        name="mix",
    )(u, u, u, o, gp, ga, x, wmix, ps, wpo, wao, wo, gffn, wr)


PREFIX_BLOCK = 256


def _prefix_rows(mask, tri):
    out = []
    carry = jnp.zeros((1, mask.shape[1]), F32)
    for blk in range(mask.shape[0] // PREFIX_BLOCK):
        m = mask[blk * PREFIX_BLOCK:(blk + 1) * PREFIX_BLOCK]
        p = _dot(tri, m.astype(BF16)) + carry
        carry = p[PREFIX_BLOCK - 1:PREFIX_BLOCK]
        out.append(p)
    return jnp.concatenate(out, axis=0)


def _route_kernel(cap, lg_ref, slot_ref, aff_ref, idx_ref):
    lg = lg_ref[0]
    lane = lax.broadcasted_iota(I32, lg.shape, 1)
    valid = lane < N_EXPERTS
    lg = jnp.where(valid, lg, -jnp.inf)
    ex = jnp.exp(lg - jnp.max(lg, axis=-1, keepdims=True))
    aff = ex / jnp.sum(ex, axis=-1, keepdims=True)
    aff_ref[0] = aff
    bits = lax.bitcast_convert_type(aff, I32)

    def step(k, ans):
        cand = ans | jnp.left_shift(jnp.int32(1), 30 - k)
        cnt = jnp.sum(jnp.where(bits >= cand, 1.0, 0.0), axis=0, keepdims=True)
        return jnp.where(cnt >= cap, cand, ans)

    thr = lax.fori_loop(0, 31, step, jnp.zeros((1, LANES), I32))
    gt = jnp.where(bits > thr, 1.0, 0.0)
    eq = jnp.where(bits == thr, 1.0, 0.0)
    r, c = lax.broadcasted_iota(I32, (PREFIX_BLOCK, PREFIX_BLOCK), 0), lax.broadcasted_iota(I32, (PREFIX_BLOCK, PREFIX_BLOCK), 1)
    tri = jnp.where(c <= r, 1.0, 0.0).astype(BF16)
    room = cap - jnp.sum(gt, axis=0, keepdims=True)
    eq_before = _prefix_rows(eq, tri) - eq
    sel = jnp.maximum(gt, jnp.where(eq_before < room, eq, 0.0))
    chosen_upto = _prefix_rows(sel, tri)
    slot_ref[0] = jnp.where(valid & (sel > 0.0), chosen_upto - 1.0, -1.0).astype(I32)
    s_len = lg.shape[0]
    cidx = lax.broadcasted_iota(I32, (1, cap), 1).astype(F32)
    rows = []
    for e in range(N_EXPERTS):
        before = jnp.where(chosen_upto[:, e:e + 1] <= cidx, 1.0, 0.0)
        rows.append(jnp.sum(before, axis=0, keepdims=True))
    idx_ref[0] = jnp.concatenate(rows, axis=0).astype(I32) + pl.program_id(0) * s_len


def _route(logits, cap):
    b, s, _ = logits.shape
    spec = pl.BlockSpec((1, s, LANES), lambda bi: (bi, 0, 0))
    return pl.pallas_call(
        functools.partial(_route_kernel, cap),
        grid=(b,),
        in_specs=[spec],
        out_specs=[spec, spec, pl.BlockSpec((1, N_EXPERTS, cap), lambda bi: (bi, 0, 0))],
        out_shape=[jax.ShapeDtypeStruct((b, s, LANES), I32), jax.ShapeDtypeStruct((b, s, LANES), F32),
                   jax.ShapeDtypeStruct((b, N_EXPERTS, cap), I32)],
        compiler_params=_params(("parallel",)),
        name="route",
    )(logits)


def _ffn_kernel(rows_per_step, idx_ref, x_hbm, g_ref, wg_ref, wu_ref, wd_ref, y_ref,
                rows_ref, xs_ref, acc_ref, sem):
    e = pl.program_id(0)
    f = pl.program_id(1)
    n_e = pl.num_programs(0)
    n_f = pl.num_programs(1)
    m = xs_ref.shape[0]
    rows_pad = rows_ref.shape[1]

    def row_copy(expert, i, slot):
        return pltpu.make_async_copy(x_hbm.at[pl.ds(idx_ref[expert, i], 1)],
                                     rows_ref.at[slot, pl.ds(i, 1)], sem.at[slot])

    def wait_rows(slot):
        pltpu.make_async_copy(x_hbm.at[pl.ds(0, rows_pad)], rows_ref.at[slot], sem.at[slot]).wait()

    @pl.when((e == 0) & (f == 0))
    def _():
        def issue(i, carry):
            row_copy(0, i, 0).start()
            return carry
        lax.fori_loop(0, rows_pad, issue, 0)

    @pl.when(f == 0)
    def _():
        slot = e % 2
        wait_rows(slot)
        xs_ref[...] = _rms(rows_ref[slot, :m, :], g_ref[...]).astype(BF16)
        acc_ref[...] = jnp.zeros_like(acc_ref)

    for j in range(rows_per_step):
        row_copy(e + 1, f * rows_per_step + j, (e + 1) % 2).start()

    xs = xs_ref[...]
    g = _dot(xs, wg_ref[0].astype(BF16))
    u = _dot(xs, wu_ref[0].astype(BF16))
    act = (g / (1.0 + jnp.exp(-g)) * u).astype(BF16)
    acc_ref[...] += _dot(act, wd_ref[0].astype(BF16))

    @pl.when(f == n_f - 1)
    def _():
        y_ref[0] = acc_ref[...].astype(BF16)

    @pl.when((e == n_e - 1) & (f == n_f - 1))
    def _():
        wait_rows(n_e % 2)


def _ffn(idx, x_rows, g_ffn, w_gate, w_up, w_down, tf=256):
    e, m = idx.shape
    d = x_rows.shape[1]
    ff = w_gate.shape[2]
    n_f = ff // tf
    rows_per_step = pl.cdiv(pl.cdiv(m, n_f), SUBLANES) * SUBLANES
    rows_pad = rows_per_step * n_f
    idx_pad = jnp.zeros((e + 1, rows_pad), I32).at[:e, :m].set(idx)
    row = lambda ei, f, idx_ref: (ei, 0, 0)
    grid_spec = pltpu.PrefetchScalarGridSpec(
        num_scalar_prefetch=1,
        grid=(e, n_f),
        in_specs=[pl.BlockSpec(memory_space=pl.ANY),
                  pl.BlockSpec((1, d), lambda ei, f, idx_ref: (0, 0)),
                  pl.BlockSpec((1, d, tf), lambda ei, f, idx_ref: (ei, 0, f)),
                  pl.BlockSpec((1, d, tf), lambda ei, f, idx_ref: (ei, 0, f)),
                  pl.BlockSpec((1, tf, d), lambda ei, f, idx_ref: (ei, f, 0))],
        out_specs=pl.BlockSpec((1, m, d), row),
        scratch_shapes=[pltpu.VMEM((2, rows_pad, d), F32), pltpu.VMEM((m, d), BF16),
                        pltpu.VMEM((m, d), F32), pltpu.SemaphoreType.DMA((2,))],
    )
    return pl.pallas_call(
        functools.partial(_ffn_kernel, rows_per_step),
        grid_spec=grid_spec,
        out_shape=jax.ShapeDtypeStruct((e, m, d), BF16),
        compiler_params=_params(("arbitrary", "arbitrary")),
        name="ffn",
    )(idx_pad, x_rows, g_ffn, w_gate, w_up, w_down)


def _combine_kernel(cap, x1_ref, slot_ref, aff_ref, y_ref, g_ref, o_ref):
    slot = slot_ref[0]
    aff = aff_ref[0]
    tt = slot.shape[0]
    cidx = lax.broadcasted_iota(I32, (tt, cap), 1)
    acc = x1_ref[0]
    for e in range(N_EXPERTS):
        hit = slot[:, e:e + 1] == cidx
        acc = acc + aff[:, e:e + 1] * _dot(jnp.where(hit, 1.0, 0.0).astype(BF16), y_ref[e])
    o_ref[0] = _rms(acc, g_ref[...])


def _combine(x1, slot_t, aff_t, y, g_final, cap, tt=256):
    b, s, d = x1.shape
    tile = lambda bi, i: (bi, i, 0)
    return pl.pallas_call(
        functools.partial(_combine_kernel, cap),
        grid=(b, s // tt),
        in_specs=[pl.BlockSpec((1, tt, d), tile), pl.BlockSpec((1, tt, LANES), tile),
                  pl.BlockSpec((1, tt, LANES), tile),
                  pl.BlockSpec((N_EXPERTS, cap, d), lambda bi, i: (0, bi, 0)),
                  pl.BlockSpec((1, d), lambda bi, i: (0, 0))],
        out_specs=pl.BlockSpec((1, tt, d), tile),
        out_shape=jax.ShapeDtypeStruct((b, s, d), F32),
        compiler_params=_params(("parallel", "arbitrary")),
        name="combine",
    )(x1, slot_t, aff_t, y, g_final)


def _layer(x, positions, g_mix, w_in, w_pool_mix, pool_scale, w_pool_out, lam_q1, lam_k1, lam_q2,
           lam_k2, g_subln, w_attn_out, w_out, g_ffn, w_router, w_gate, w_up, w_down, lam_init):
    b, s, d = x.shape
    t = b * s
    cap = EC_CAPACITY_FACTOR * s // N_EXPERTS

    posb = jnp.broadcast_to(positions.reshape(t, 1).astype(F32), (t, LANES))
    inv = ROPE_THETA ** (-jnp.arange(ROT_HALF, dtype=F32) * 2.0 / ROT_DIM)
    lane = jnp.arange(LANES)
    inv_tab = jnp.where((lane % HEAD_DIM) < ROT_DIM, inv[lane % ROT_HALF], 0.0).reshape(1, LANES).astype(F32)

    u, q, k1, k2, v, gp, ga = _inproj(x.reshape(t, d), posb, g_mix.reshape(1, d), inv_tab, w_in.astype(BF16))
    sh = lambda a: a.reshape(b, s, d)

    lamv = jnp.zeros((8, LANES), F32).at[0:4, 0:HEAD_DIM].set(jnp.stack([lam_q1, lam_k1, lam_q2, lam_k2]).astype(F32))
    vt = jnp.transpose(sh(v), (0, 2, 1))
    o = _attention(sh(q), sh(k1), sh(k2), vt, lamv, g_subln.reshape(1, HEAD_WIDTH), lam_init)

    wr = jnp.zeros((d, LANES), BF16).at[:, :N_EXPERTS].set(w_router.astype(BF16))
    x1, logits = _mix(sh(u), o, sh(gp), sh(ga), x, w_pool_mix.astype(BF16), pool_scale.reshape(1, d),
                      w_pool_out.astype(BF16), w_attn_out.astype(BF16), w_out.astype(BF16),
                      g_ffn.reshape(1, d), wr)

    slot_t, aff_t, idx = _route(logits, cap)
    idx = jnp.transpose(idx, (1, 0, 2)).reshape(N_EXPERTS, b * cap)
    y = _ffn(idx, x1.reshape(t, d), g_ffn.reshape(1, d), w_gate, w_up, w_down)
    return x1, slot_t, aff_t, y, cap


def kernel(x, positions, g_mix, w_in, w_pool_mix, pool_scale, w_pool_out, lam_q1, lam_k1, lam_q2, lam_k2,
           g_subln, w_attn_out, w_out, g_ffn, w_router, w_gate, w_up, w_down, g_final):
    depth = g_mix.shape[0]
    assert depth == 1, "the final norm is fused into the last layer's combine step"
    l = 0
    lam_init = 0.8 - 0.6 * math.exp(-0.3 * l)
    x1, slot_t, aff_t, y, cap = _layer(x, positions, g_mix[l], w_in[l], w_pool_mix[l], pool_scale[l],
                                       w_pool_out[l], lam_q1[l], lam_k1[l], lam_q2[l], lam_k2[l], g_subln[l],
                                       w_attn_out[l], w_out[l], g_ffn[l], w_router[l], w_gate[l], w_up[l],
                                       w_down[l], lam_init)
    return _combine(x1, slot_t, aff_t, y, g_final.reshape(1, D_MODEL), cap)
```

```python
import functools
import math

import jax
import jax.numpy as jnp
from jax import lax
from jax.experimental import pallas as pl
from jax.experimental.pallas import tpu as pltpu

F32 = jnp.float32
BF16 = jnp.bfloat16
I32 = jnp.int32

D_MODEL = 1024
N_HEADS = 8
HEAD_DIM = 64
HEAD_WIDTH = 2 * HEAD_DIM
ROT_DIM = HEAD_DIM // 4
ROT_HALF = ROT_DIM // 2
ROPE_THETA = 500000.0
POOL_WINDOWS = (2, 4, 8, 16)
POOL_GROUP_DIM = D_MODEL // len(POOL_WINDOWS)
POOL_HALO = 16
N_EXPERTS = 16
EC_CAPACITY_FACTOR = 2
RMS_EPS = 1e-6
Q_SCALE = HEAD_DIM ** -0.5 * math.log2(math.e)
LANES = 128
SUBLANES = 8

VMEM_LIMIT = 56 * 1024 * 1024


def _params(sem):
    return pltpu.CompilerParams(dimension_semantics=sem, vmem_limit_bytes=VMEM_LIMIT)


def _rms(x, g):
    return (x * lax.rsqrt(jnp.mean(x * x, axis=-1, keepdims=True) + RMS_EPS)) * g


def _dot(a, b):
    return jnp.dot(a, b, preferred_element_type=F32)


def _rotary(z, cos, sin, first_half):
    up = pltpu.roll(z, LANES - ROT_HALF, 1)
    down = pltpu.roll(z, ROT_HALF, 1)
    partner = jnp.where(first_half, -up, down)
    return z * cos + partner * sin


def _inproj_kernel(x_ref, pos_ref, g_ref, inv_ref, w_ref,
                   u_ref, q_ref, k1_ref, k2_ref, v_ref, gp_ref, ga_ref):
    hb = _rms(x_ref[...], g_ref[...]).astype(BF16)
    ang = pos_ref[...] * inv_ref[...]
    cos = jnp.cos(ang)
    sin = jnp.sin(ang)
    lane = lax.broadcasted_iota(I32, ang.shape, 1)
    first_half = (lane % HEAD_DIM) < ROT_HALF
    map1 = lane < HEAD_DIM

    def col(j):
        return _dot(hb, w_ref[:, j * D_MODEL:(j + 1) * D_MODEL])

    u_ref[...] = col(0)
    zq = col(1)
    zk = col(2)
    for h in range(N_HEADS):
        sl = slice(h * HEAD_WIDTH, (h + 1) * HEAD_WIDTH)
        q_ref[:, sl] = (_rotary(zq[:, sl], cos, sin, first_half) * Q_SCALE).astype(BF16)
        kr = _rotary(zk[:, sl], cos, sin, first_half)
        k1_ref[:, sl] = jnp.where(map1, kr, 0.0).astype(BF16)
        k2_ref[:, sl] = jnp.where(map1, 0.0, kr).astype(BF16)
    v_ref[...] = col(3).astype(BF16)
    gp_ref[...] = 1.0 / (1.0 + jnp.exp(-col(4)))
    ga_ref[...] = 1.0 / (1.0 + jnp.exp(-col(5)))


def _inproj(x2d, posb, g_mix, inv_tab, w_in_bf, tm=256):
    t = x2d.shape[0]
    n_cols = w_in_bf.shape[1]
    row = lambda i: (i, 0)
    const = lambda i: (0, 0)
    tile = pl.BlockSpec((tm, D_MODEL), row)
    out_shapes = [jax.ShapeDtypeStruct((t, D_MODEL), dt) for dt in (F32, BF16, BF16, BF16, BF16, F32, F32)]
    return pl.pallas_call(
        _inproj_kernel,
        grid=(t // tm,),
        in_specs=[tile, pl.BlockSpec((tm, LANES), row), pl.BlockSpec((1, D_MODEL), const),
                  pl.BlockSpec((1, LANES), const), pl.BlockSpec((D_MODEL, n_cols), const)],
        out_specs=[tile] * 7,
        out_shape=out_shapes,
        compiler_params=_params(("parallel",)),
        name="inproj",
    )(x2d, posb, g_mix, inv_tab, w_in_bf)


SCORE_SLOTS = 2
SUM_ROWS = 16


def _attn_kernel(lam_init, tk, q_ref, k1_ref, k2_ref, vt_ref, lamv_ref, g_ref, o_ref, st_ref, acc_ref):
    lv = lamv_ref[...]
    lam = (jnp.exp(jnp.sum(lv[0:1] * lv[1:2], axis=-1, keepdims=True))
           - jnp.exp(jnp.sum(lv[2:3] * lv[3:4], axis=-1, keepdims=True)) + lam_init)
    q = q_ref[0]
    tq = q.shape[0]
    n_chunks = k1_ref.shape[1] // tk
    nt = (((1,), (1,)), ((), ()))
    k_refs = (k1_ref, k2_ref)

    def scores(c):
        for mp in range(2):
            st_ref[c % SCORE_SLOTS, mp] = lax.dot_general(k_refs[mp][0, c * tk:(c + 1) * tk, :], q, nt,
                                                          preferred_element_type=F32)

    ones_rows = jnp.ones((SUM_ROWS, tk), BF16)

    def accumulate(c, m):
        out_m = []
        for mp in range(2):
            st = st_ref[c % SCORE_SLOTS, mp]
            m_new = jnp.maximum(m[mp], jnp.max(st, axis=0, keepdims=True))
            alpha = jnp.exp2(m[mp] - m_new)
            et = jnp.exp2(st - m_new).astype(BF16)
            vt_ext = jnp.concatenate([vt_ref[0, :, c * tk:(c + 1) * tk], ones_rows], axis=0)
            pv = _dot(vt_ext, et)
            acc_ref[mp] = pv if c == 0 else acc_ref[mp] * alpha + pv
            out_m.append(m_new)
        return out_m

    m = [jnp.full((1, tq), -jnp.inf, F32)] * 2
    scores(0)
    for c in range(n_chunks):
        if c + 1 < n_chunks:
            scores(c + 1)
        m = accumulate(c, m)
    a1, a2 = acc_ref[0], acc_ref[1]
    r1 = 1.0 / a1[HEAD_WIDTH:HEAD_WIDTH + 1]
    r2 = lam / a2[HEAD_WIDTH:HEAD_WIDTH + 1]
    ot = a1[:HEAD_WIDTH] * r1 - a2[:HEAD_WIDTH] * r2
    o_ref[0] = (_rms(ot.T, g_ref[...]) * (1.0 - lam_init)).astype(BF16)


def _attention(q, k1, k2, vt, lamv, g_subln, lam_init, tq=512, tk=512):
    b, s, _ = q.shape
    qspec = pl.BlockSpec((1, tq, HEAD_WIDTH), lambda bi, h, qi: (bi, qi, h))
    kspec = pl.BlockSpec((1, s, HEAD_WIDTH), lambda bi, h, qi: (bi, 0, h))
    vspec = pl.BlockSpec((1, HEAD_WIDTH, s), lambda bi, h, qi: (bi, h, 0))
    const = lambda bi, h, qi: (0, 0)
    return pl.pallas_call(
        functools.partial(_attn_kernel, lam_init, tk),
        grid=(b, N_HEADS, s // tq),
        in_specs=[qspec, kspec, kspec, vspec,
                  pl.BlockSpec((8, LANES), const), pl.BlockSpec((1, HEAD_WIDTH), const)],
        out_specs=qspec,
        out_shape=jax.ShapeDtypeStruct(q.shape, BF16),
        scratch_shapes=[pltpu.VMEM((SCORE_SLOTS, 2, tk, tq), F32), pltpu.VMEM((2, HEAD_WIDTH + SUM_ROWS, tq), F32)],
        compiler_params=_params(("parallel", "parallel", "arbitrary")),
        name="attn",
    )(q, k1, k2, vt, lamv, g_subln)


def _mix_kernel(seq, u_ref, up_ref, un_ref, o_ref, gp_ref, ga_ref, x_ref,
                wmix_ref, ps_ref, wpo_ref, wao_ref, wo_ref, gffn_ref, wr_ref,
                x1_ref, lg_ref, upad_ref):
    tm = u_ref.shape[1]
    i = pl.program_id(1)
    nt = pl.num_programs(1)
    main = u_ref[0]
    upad_ref[0:POOL_HALO, :] = jnp.where(i > 0, up_ref[0], 0.0)
    upad_ref[POOL_HALO:POOL_HALO + tm, :] = main
    upad_ref[POOL_HALO + tm:, :] = jnp.where(i < nt - 1, un_ref[0], 0.0)
    t = i * tm + lax.broadcasted_iota(I32, (tm, 1), 0)
    mixed = []
    for g, w in enumerate(POOL_WINDOWS):
        cols = slice(g * POOL_GROUP_DIM, (g + 1) * POOL_GROUP_DIM)
        win = upad_ref[POOL_HALO - w // 2:POOL_HALO - w // 2 + tm, cols]
        for d in range(-w // 2 + 1, w // 2):
            win = win + upad_ref[POOL_HALO + d:POOL_HALO + d + tm, cols]
        cnt = (jnp.minimum(t + w // 2, seq) - jnp.maximum(t - w // 2, 0)).astype(F32)
        pooled = win / cnt - main[:, cols]
        mixed.append((_dot(pooled.astype(BF16), wmix_ref[g]) * ps_ref[:, cols]).astype(BF16))
    y_pool = _dot(jnp.concatenate(mixed, axis=-1), wpo_ref[...])
    y_attn = _dot(o_ref[0], wao_ref[...])
    merged = gp_ref[0] * y_pool + ga_ref[0] * y_attn
    x1 = x_ref[0] + _dot(merged.astype(BF16), wo_ref[...])
    x1_ref[0] = x1
    lg_ref[0] = _dot(_rms(x1, gffn_ref[...]).astype(BF16), wr_ref[...])


def _mix(u, o, gp, ga, x, wmix, ps, wpo, wao, wo, gffn, wr, tm=256):
    b, s, d = x.shape
    hb = tm // POOL_HALO
    n_halo = s // POOL_HALO
    tile = pl.BlockSpec((1, tm, d), lambda bi, i: (bi, i, 0))
    prev = pl.BlockSpec((1, POOL_HALO, d), lambda bi, i: (bi, jnp.maximum(i * hb - 1, 0), 0))
    nxt = pl.BlockSpec((1, POOL_HALO, d), lambda bi, i: (bi, jnp.minimum((i + 1) * hb, n_halo - 1), 0))
    c2 = lambda bi, i: (0, 0)
    c3 = lambda bi, i: (0, 0, 0)
    return pl.pallas_call(
        functools.partial(_mix_kernel, s),
        grid=(b, s // tm),
        in_specs=[tile, prev, nxt, tile, tile, tile, tile,
                  pl.BlockSpec(wmix.shape, c3), pl.BlockSpec((1, d), c2),
                  pl.BlockSpec((d, d), c2), pl.BlockSpec((d, d), c2), pl.BlockSpec((d, d), c2),
                  pl.BlockSpec((1, d), c2), pl.BlockSpec((d, LANES), c2)],
        out_specs=[tile, pl.BlockSpec((1, tm, LANES), lambda bi, i: (bi, i, 0))],
        out_shape=[jax.ShapeDtypeStruct((b, s, d), F32), jax.ShapeDtypeStruct((b, s, LANES), F32)],
        scratch_shapes=[pltpu.VMEM((tm + 2 * POOL_HALO, d), F32)],
        compiler_params=_params(("parallel", "parallel")),
        name="mix",
    )(u, u, u, o, gp, ga, x, wmix, ps, wpo, wao, wo, gffn, wr)


PREFIX_BLOCK = 256


def _prefix_rows(mask, tri):
    out = []
    carry = jnp.zeros((1, mask.shape[1]), F32)
    for blk in range(mask.shape[0] // PREFIX_BLOCK):
        m = mask[blk * PREFIX_BLOCK:(blk + 1) * PREFIX_BLOCK]
        p = _dot(tri, m.astype(BF16)) + carry
        carry = p[PREFIX_BLOCK - 1:PREFIX_BLOCK]
        out.append(p)
    return jnp.concatenate(out, axis=0)


def _route_kernel(cap, lg_ref, slot_ref, aff_ref, idx_ref):
    lg = lg_ref[0]
    lane = lax.broadcasted_iota(I32, lg.shape, 1)
    valid = lane < N_EXPERTS
    lg = jnp.where(valid, lg, -jnp.inf)
    ex = jnp.exp(lg - jnp.max(lg, axis=-1, keepdims=True))
    aff = ex / jnp.sum(ex, axis=-1, keepdims=True)
    aff_ref[0] = aff
    bits = lax.bitcast_convert_type(aff, I32)

    def step(k, ans):
        cand = ans | jnp.left_shift(jnp.int32(1), 30 - k)
        cnt = jnp.sum(jnp.where(bits >= cand, 1.0, 0.0), axis=0, keepdims=True)
        return jnp.where(cnt >= cap, cand, ans)

    thr = lax.fori_loop(0, 31, step, jnp.zeros((1, LANES), I32))
    gt = jnp.where(bits > thr, 1.0, 0.0)
    eq = jnp.where(bits == thr, 1.0, 0.0)
    r, c = lax.broadcasted_iota(I32, (PREFIX_BLOCK, PREFIX_BLOCK), 0), lax.broadcasted_iota(I32, (PREFIX_BLOCK, PREFIX_BLOCK), 1)
    tri = jnp.where(c <= r, 1.0, 0.0).astype(BF16)
    room = cap - jnp.sum(gt, axis=0, keepdims=True)
    eq_before = _prefix_rows(eq, tri) - eq
    sel = jnp.maximum(gt, jnp.where(eq_before < room, eq, 0.0))
    chosen_upto = _prefix_rows(sel, tri)
    slot_ref[0] = jnp.where(valid & (sel > 0.0), chosen_upto - 1.0, -1.0).astype(I32)
    s_len = lg.shape[0]
    cidx = lax.broadcasted_iota(I32, (1, cap), 1).astype(F32)
    rows = []
    for e in range(N_EXPERTS):
        before = jnp.where(chosen_upto[:, e:e + 1] <= cidx, 1.0, 0.0)
        rows.append(jnp.sum(before, axis=0, keepdims=True))
    idx_ref[0] = jnp.concatenate(rows, axis=0).astype(I32) + pl.program_id(0) * s_len


def _route(logits, cap):
    b, s, _ = logits.shape
    spec = pl.BlockSpec((1, s, LANES), lambda bi: (bi, 0, 0))
    return pl.pallas_call(
        functools.partial(_route_kernel, cap),
        grid=(b,),
        in_specs=[spec],
        out_specs=[spec, spec, pl.BlockSpec((1, N_EXPERTS, cap), lambda bi: (bi, 0, 0))],
        out_shape=[jax.ShapeDtypeStruct((b, s, LANES), I32), jax.ShapeDtypeStruct((b, s, LANES), F32),
                   jax.ShapeDtypeStruct((b, N_EXPERTS, cap), I32)],
        compiler_params=_params(("parallel",)),
        name="route",
    )(logits)


def _ffn_kernel(rows_per_step, idx_ref, x_hbm, g_ref, wg_ref, wu_ref, wd_ref, y_ref,
                rows_ref, xs_ref, acc_ref, sem):
    e = pl.program_id(0)
    f = pl.program_id(1)
    n_e = pl.num_programs(0)
    n_f = pl.num_programs(1)
    m = xs_ref.shape[0]
    rows_pad = rows_ref.shape[1]

    def row_copy(expert, i, slot):
        return pltpu.make_async_copy(x_hbm.at[pl.ds(idx_ref[expert, i], 1)],
                                     rows_ref.at[slot, pl.ds(i, 1)], sem.at[slot])

    def wait_rows(slot):
        pltpu.make_async_copy(x_hbm.at[pl.ds(0, rows_pad)], rows_ref.at[slot], sem.at[slot]).wait()

    @pl.when((e == 0) & (f == 0))
    def _():
        def issue(i, carry):
            row_copy(0, i, 0).start()
            return carry
        lax.fori_loop(0, rows_pad, issue, 0)

    @pl.when(f == 0)
    def _():
        slot = e % 2
        wait_rows(slot)
        xs_ref[...] = _rms(rows_ref[slot, :m, :], g_ref[...]).astype(BF16)
        acc_ref[...] = jnp.zeros_like(acc_ref)

    for j in range(rows_per_step):
        row_copy(e + 1, f * rows_per_step + j, (e + 1) % 2).start(priority=j % 2)

    xs = xs_ref[...]
    g = _dot(xs, wg_ref[0].astype(BF16))
    u = _dot(xs, wu_ref[0].astype(BF16))
    act = (g / (1.0 + jnp.exp(-g)) * u).astype(BF16)
    acc_ref[...] += _dot(act, wd_ref[0].astype(BF16))

    @pl.when(f == n_f - 1)
    def _():
        y_ref[0] = acc_ref[...].astype(BF16)

    @pl.when((e == n_e - 1) & (f == n_f - 1))
    def _():
        wait_rows(n_e % 2)


def _ffn(idx, x_rows, g_ffn, w_gate, w_up, w_down, tf=256):
    e, m = idx.shape
    d = x_rows.shape[1]
    ff = w_gate.shape[2]
    n_f = ff // tf
    rows_per_step = pl.cdiv(pl.cdiv(m, n_f), SUBLANES) * SUBLANES
    rows_pad = rows_per_step * n_f
    idx_pad = jnp.zeros((e + 1, rows_pad), I32).at[:e, :m].set(idx)
    row = lambda ei, f, idx_ref: (ei, 0, 0)
    grid_spec = pltpu.PrefetchScalarGridSpec(
        num_scalar_prefetch=1,
        grid=(e, n_f),
        in_specs=[pl.BlockSpec(memory_space=pl.ANY),
                  pl.BlockSpec((1, d), lambda ei, f, idx_ref: (0, 0)),
                  pl.BlockSpec((1, d, tf), lambda ei, f, idx_ref: (ei, 0, f)),
                  pl.BlockSpec((1, d, tf), lambda ei, f, idx_ref: (ei, 0, f)),
                  pl.BlockSpec((1, tf, d), lambda ei, f, idx_ref: (ei, f, 0))],
        out_specs=pl.BlockSpec((1, m, d), row),
        scratch_shapes=[pltpu.VMEM((2, rows_pad, d), F32), pltpu.VMEM((m, d), BF16),
                        pltpu.VMEM((m, d), F32), pltpu.SemaphoreType.DMA((2,))],
    )
    return pl.pallas_call(
        functools.partial(_ffn_kernel, rows_per_step),
        grid_spec=grid_spec,
        out_shape=jax.ShapeDtypeStruct((e, m, d), BF16),
        compiler_params=_params(("arbitrary", "arbitrary")),
        name="ffn",
    )(idx_pad, x_rows, g_ffn, w_gate, w_up, w_down)


def _combine_kernel(cap, x1_ref, slot_ref, aff_ref, y_ref, g_ref, o_ref):
    slot = slot_ref[0]
    aff = aff_ref[0]
    tt = slot.shape[0]
    cidx = lax.broadcasted_iota(I32, (tt, cap), 1)
    acc = x1_ref[0]
    for e in range(N_EXPERTS):
        hit = slot[:, e:e + 1] == cidx
        acc = acc + aff[:, e:e + 1] * _dot(jnp.where(hit, 1.0, 0.0).astype(BF16), y_ref[e])
    o_ref[0] = _rms(acc, g_ref[...])


def _combine(x1, slot_t, aff_t, y, g_final, cap, tt=256):
    b, s, d = x1.shape
    tile = lambda bi, i: (bi, i, 0)
    return pl.pallas_call(
        functools.partial(_combine_kernel, cap),
        grid=(b, s // tt),
        in_specs=[pl.BlockSpec((1, tt, d), tile), pl.BlockSpec((1, tt, LANES), tile),
                  pl.BlockSpec((1, tt, LANES), tile),
                  pl.BlockSpec((N_EXPERTS, cap, d), lambda bi, i: (0, bi, 0)),
                  pl.BlockSpec((1, d), lambda bi, i: (0, 0))],
        out_specs=pl.BlockSpec((1, tt, d), tile),
        out_shape=jax.ShapeDtypeStruct((b, s, d), F32),
        compiler_params=_params(("parallel", "arbitrary")),
        name="combine",
    )(x1, slot_t, aff_t, y, g_final)


def _layer(x, positions, g_mix, w_in, w_pool_mix, pool_scale, w_pool_out, lam_q1, lam_k1, lam_q2,
           lam_k2, g_subln, w_attn_out, w_out, g_ffn, w_router, w_gate, w_up, w_down, lam_init):
    b, s, d = x.shape
    t = b * s
    cap = EC_CAPACITY_FACTOR * s // N_EXPERTS

    posb = jnp.broadcast_to(positions.reshape(t, 1).astype(F32), (t, LANES))
    inv = ROPE_THETA ** (-jnp.arange(ROT_HALF, dtype=F32) * 2.0 / ROT_DIM)
    lane = jnp.arange(LANES)
    inv_tab = jnp.where((lane % HEAD_DIM) < ROT_DIM, inv[lane % ROT_HALF], 0.0).reshape(1, LANES).astype(F32)

    u, q, k1, k2, v, gp, ga = _inproj(x.reshape(t, d), posb, g_mix.reshape(1, d), inv_tab, w_in.astype(BF16))
    sh = lambda a: a.reshape(b, s, d)

    lamv = jnp.zeros((8, LANES), F32).at[0:4, 0:HEAD_DIM].set(jnp.stack([lam_q1, lam_k1, lam_q2, lam_k2]).astype(F32))
    vt = jnp.transpose(sh(v), (0, 2, 1))
    o = _attention(sh(q), sh(k1), sh(k2), vt, lamv, g_subln.reshape(1, HEAD_WIDTH), lam_init)

    wr = jnp.zeros((d, LANES), BF16).at[:, :N_EXPERTS].set(w_router.astype(BF16))
    x1, logits = _mix(sh(u), o, sh(gp), sh(ga), x, w_pool_mix.astype(BF16), pool_scale.reshape(1, d),
                      w_pool_out.astype(BF16), w_attn_out.astype(BF16), w_out.astype(BF16),
                      g_ffn.reshape(1, d), wr)

    slot_t, aff_t, idx = _route(logits, cap)
    idx = jnp.transpose(idx, (1, 0, 2)).reshape(N_EXPERTS, b * cap)
    y = _ffn(idx, x1.reshape(t, d), g_ffn.reshape(1, d), w_gate, w_up, w_down)
    return x1, slot_t, aff_t, y, cap


def kernel(x, positions, g_mix, w_in, w_pool_mix, pool_scale, w_pool_out, lam_q1, lam_k1, lam_q2, lam_k2,
           g_subln, w_attn_out, w_out, g_ffn, w_router, w_gate, w_up, w_down, g_final):
    depth = g_mix.shape[0]
    assert depth == 1, "the final norm is fused into the last layer's combine step"
    l = 0
    lam_init = 0.8 - 0.6 * math.exp(-0.3 * l)
    x1, slot_t, aff_t, y, cap = _layer(x, positions, g_mix[l], w_in[l], w_pool_mix[l], pool_scale[l],
                                       w_pool_out[l], lam_q1[l], lam_k1[l], lam_q2[l], lam_k2[l], g_subln[l],
                                       w_attn_out[l], w_out[l], g_ffn[l], w_router[l], w_gate[l], w_up[l],
                                       w_down[l], lam_init)
    return _combine(x1, slot_t, aff_t, y, g_final.reshape(1, D_MODEL), cap)
```

```python
import functools
import math

import jax
import jax.numpy as jnp
from jax import lax
from jax.experimental import pallas as pl
from jax.experimental.pallas import tpu as pltpu

F32 = jnp.float32
BF16 = jnp.bfloat16
I32 = jnp.int32

D_MODEL = 1024
N_HEADS = 8
HEAD_DIM = 64
HEAD_WIDTH = 2 * HEAD_DIM
ROT_DIM = HEAD_DIM // 4
ROT_HALF = ROT_DIM // 2
ROPE_THETA = 500000.0
POOL_WINDOWS = (2, 4, 8, 16)
POOL_GROUP_DIM = D_MODEL // len(POOL_WINDOWS)
POOL_HALO = 16
N_EXPERTS = 16
EC_CAPACITY_FACTOR = 2
RMS_EPS = 1e-6
Q_SCALE = HEAD_DIM ** -0.5 * math.log2(math.e)
LANES = 128

VMEM_LIMIT = 56 * 1024 * 1024


def _params(sem):
    return pltpu.CompilerParams(dimension_semantics=sem, vmem_limit_bytes=VMEM_LIMIT)


def _rms(x, g):
    return (x * lax.rsqrt(jnp.mean(x * x, axis=-1, keepdims=True) + RMS_EPS)) * g


def _dot(a, b):
    return jnp.dot(a, b, preferred_element_type=F32)


def _rotary(z, cos, sin, first_half):
    up = pltpu.roll(z, LANES - ROT_HALF, 1)
    down = pltpu.roll(z, ROT_HALF, 1)
    partner = jnp.where(first_half, -up, down)
    return z * cos + partner * sin


def _inproj_kernel(x_ref, pos_ref, g_ref, inv_ref, w_ref,
                   u_ref, q_ref, k1_ref, k2_ref, v_ref, gp_ref, ga_ref):
    hb = _rms(x_ref[...], g_ref[...]).astype(BF16)
    ang = pos_ref[...] * inv_ref[...]
    cos = jnp.cos(ang)
    sin = jnp.sin(ang)
    lane = lax.broadcasted_iota(I32, ang.shape, 1)
    first_half = (lane % HEAD_DIM) < ROT_HALF
    map1 = lane < HEAD_DIM

    def col(j):
        return _dot(hb, w_ref[:, j * D_MODEL:(j + 1) * D_MODEL])

    u_ref[...] = col(0)
    zq = col(1)
    zk = col(2)
    for h in range(N_HEADS):
        sl = slice(h * HEAD_WIDTH, (h + 1) * HEAD_WIDTH)
        q_ref[:, sl] = (_rotary(zq[:, sl], cos, sin, first_half) * Q_SCALE).astype(BF16)
        kr = _rotary(zk[:, sl], cos, sin, first_half)
        k1_ref[:, sl] = jnp.where(map1, kr, 0.0).astype(BF16)
        k2_ref[:, sl] = jnp.where(map1, 0.0, kr).astype(BF16)
    v_ref[...] = col(3).astype(BF16)
    gp_ref[...] = 1.0 / (1.0 + jnp.exp(-col(4)))
    ga_ref[...] = 1.0 / (1.0 + jnp.exp(-col(5)))


def _inproj(x2d, posb, g_mix, inv_tab, w_in_bf, tm=256):
    t = x2d.shape[0]
    n_cols = w_in_bf.shape[1]
    row = lambda i: (i, 0)
    const = lambda i: (0, 0)
    tile = pl.BlockSpec((tm, D_MODEL), row)
    out_shapes = [jax.ShapeDtypeStruct((t, D_MODEL), dt) for dt in (F32, BF16, BF16, BF16, BF16, F32, F32)]
    return pl.pallas_call(
        _inproj_kernel,
        grid=(t // tm,),
        in_specs=[tile, pl.BlockSpec((tm, LANES), row), pl.BlockSpec((1, D_MODEL), const),
                  pl.BlockSpec((1, LANES), const), pl.BlockSpec((D_MODEL, n_cols), const)],
        out_specs=[tile] * 7,
        out_shape=out_shapes,
        compiler_params=_params(("parallel",)),
        name="inproj",
    )(x2d, posb, g_mix, inv_tab, w_in_bf)


SCORE_SLOTS = 2
SUM_ROWS = 16
MAX_EXCESS = 64.0


def _attn_kernel(lam_init, tk, q_ref, k1_ref, k2_ref, vt_ref, lamv_ref, g_ref, o_ref, st_ref, acc_ref):
    lv = lamv_ref[...]
    lam = (jnp.exp(jnp.sum(lv[0:1] * lv[1:2], axis=-1, keepdims=True))
           - jnp.exp(jnp.sum(lv[2:3] * lv[3:4], axis=-1, keepdims=True)) + lam_init)
    q = q_ref[0]
    tq = q.shape[0]
    n_chunks = k1_ref.shape[1] // tk
    nt = (((1,), (1,)), ((), ()))
    k_refs = (k1_ref, k2_ref)

    def scores(c):
        for mp in range(2):
            st_ref[c % SCORE_SLOTS, mp] = lax.dot_general(k_refs[mp][0, c * tk:(c + 1) * tk, :], q, nt,
                                                          preferred_element_type=F32)

    ones_rows = jnp.ones((SUM_ROWS, tk), BF16)

    def accumulate(c, m):
        out_m = []
        for mp in range(2):
            st = st_ref[c % SCORE_SLOTS, mp]
            m_new = jnp.maximum(m[mp], jnp.max(st, axis=0, keepdims=True))
            alpha = jnp.exp2(m[mp] - m_new)
            et = jnp.exp2(st - m_new).astype(BF16)
            vt_ext = jnp.concatenate([vt_ref[0, :, c * tk:(c + 1) * tk], ones_rows], axis=0)
            pv = _dot(vt_ext, et)
            acc_ref[mp] = pv if c == 0 else acc_ref[mp] * alpha + pv
            out_m.append(m_new)
        return out_m

    def stream(c, m, sigma_prev, excess):
        out_m, out_sigma, out_excess = [], [], []
        for mp in range(2):
            s = lax.dot_general(k_refs[mp][0, c * tk:(c + 1) * tk, :], q, nt, preferred_element_type=F32)
            sigma = m[mp]
            et = jnp.exp2(s - sigma).astype(BF16)
            m_new = jnp.maximum(sigma, jnp.max(s, axis=0, keepdims=True))
            vt_ext = jnp.concatenate([vt_ref[0, :, c * tk:(c + 1) * tk], ones_rows], axis=0)
            acc_ref[mp] = acc_ref[mp] * jnp.exp2(sigma_prev[mp] - sigma) + _dot(vt_ext, et)
            out_m.append(m_new)
            out_sigma.append(sigma)
            out_excess.append(jnp.maximum(excess[mp], m_new - sigma))
        return out_m, out_sigma, out_excess

    m = [jnp.full((1, tq), -jnp.inf, F32)] * 2
    scores(0)
    m = accumulate(0, m)
    sigma, excess = m, [jnp.zeros((1, tq), F32)] * 2
    for c in range(1, n_chunks):
        m, sigma, excess = stream(c, m, sigma, excess)

    @pl.when(jnp.max(jnp.maximum(excess[0], excess[1])) > MAX_EXCESS)
    def _():
        m = [jnp.full((1, tq), -jnp.inf, F32)] * 2
        scores(0)
        for c in range(n_chunks):
            if c + 1 < n_chunks:
                scores(c + 1)
            m = accumulate(c, m)

    a1, a2 = acc_ref[0], acc_ref[1]
    r1 = 1.0 / a1[HEAD_WIDTH:HEAD_WIDTH + 1]
    r2 = lam / a2[HEAD_WIDTH:HEAD_WIDTH + 1]
    ot = a1[:HEAD_WIDTH] * r1 - a2[:HEAD_WIDTH] * r2
    o_ref[0] = (_rms(ot.T, g_ref[...]) * (1.0 - lam_init)).astype(BF16)


def _attention(q, k1, k2, vt, lamv, g_subln, lam_init, tq=512, tk=512):
    b, s, _ = q.shape
    qspec = pl.BlockSpec((1, tq, HEAD_WIDTH), lambda bi, h, qi: (bi, qi, h))
    kspec = pl.BlockSpec((1, s, HEAD_WIDTH), lambda bi, h, qi: (bi, 0, h))
    vspec = pl.BlockSpec((1, HEAD_WIDTH, s), lambda bi, h, qi: (bi, h, 0))
    const = lambda bi, h, qi: (0, 0)
    return pl.pallas_call(
        functools.partial(_attn_kernel, lam_init, tk),
        grid=(b, N_HEADS, s // tq),
        in_specs=[qspec, kspec, kspec, vspec,
                  pl.BlockSpec((8, LANES), const), pl.BlockSpec((1, HEAD_WIDTH), const)],
        out_specs=qspec,
        out_shape=jax.ShapeDtypeStruct(q.shape, BF16),
        scratch_shapes=[pltpu.VMEM((SCORE_SLOTS, 2, tk, tq), F32), pltpu.VMEM((2, HEAD_WIDTH + SUM_ROWS, tq), F32)],
        compiler_params=_params(("parallel", "parallel", "arbitrary")),
        name="attn",
    )(q, k1, k2, vt, lamv, g_subln)


def _mix_kernel(seq, u_ref, up_ref, un_ref, o_ref, gp_ref, ga_ref, x_ref,
                wmix_ref, ps_ref, wpo_ref, wao_ref, wo_ref, gffn_ref, wr_ref,
                x1_ref, hn_ref, lg_ref, upad_ref):
    tm = u_ref.shape[1]
    i = pl.program_id(1)
    nt = pl.num_programs(1)
    main = u_ref[0]
    upad_ref[0:POOL_HALO, :] = jnp.where(i > 0, up_ref[0], 0.0)
    upad_ref[POOL_HALO:POOL_HALO + tm, :] = main
    upad_ref[POOL_HALO + tm:, :] = jnp.where(i < nt - 1, un_ref[0], 0.0)
    t = i * tm + lax.broadcasted_iota(I32, (tm, 1), 0)
    mixed = []
    for g, w in enumerate(POOL_WINDOWS):
        cols = slice(g * POOL_GROUP_DIM, (g + 1) * POOL_GROUP_DIM)
        win = upad_ref[POOL_HALO - w // 2:POOL_HALO - w // 2 + tm, cols]
        for d in range(-w // 2 + 1, w // 2):
            win = win + upad_ref[POOL_HALO + d:POOL_HALO + d + tm, cols]
        cnt = (jnp.minimum(t + w // 2, seq) - jnp.maximum(t - w // 2, 0)).astype(F32)
        pooled = win / cnt - main[:, cols]
        mixed.append((_dot(pooled.astype(BF16), wmix_ref[g]) * ps_ref[:, cols]).astype(BF16))
    y_pool = _dot(jnp.concatenate(mixed, axis=-1), wpo_ref[...])
    y_attn = _dot(o_ref[0], wao_ref[...])
    merged = gp_ref[0] * y_pool + ga_ref[0] * y_attn
    x1 = x_ref[0] + _dot(merged.astype(BF16), wo_ref[...])
    x1_ref[0] = x1
    hn = _rms(x1, gffn_ref[...]).astype(BF16)
    hn_ref[0] = hn
    lg_ref[0] = _dot(hn, wr_ref[...])


def _mix(u, o, gp, ga, x, wmix, ps, wpo, wao, wo, gffn, wr, tm=256):
    b, s, d = x.shape
    hb = tm // POOL_HALO
    n_halo = s // POOL_HALO
    tile = pl.BlockSpec((1, tm, d), lambda bi, i: (bi, i, 0))
    prev = pl.BlockSpec((1, POOL_HALO, d), lambda bi, i: (bi, jnp.maximum(i * hb - 1, 0), 0))
    nxt = pl.BlockSpec((1, POOL_HALO, d), lambda bi, i: (bi, jnp.minimum((i + 1) * hb, n_halo - 1), 0))
    c2 = lambda bi, i: (0, 0)
    c3 = lambda bi, i: (0, 0, 0)
    return pl.pallas_call(
        functools.partial(_mix_kernel, s),
        grid=(b, s // tm),
        in_specs=[tile, prev, nxt, tile, tile, tile, tile,
                  pl.BlockSpec(wmix.shape, c3), pl.BlockSpec((1, d), c2),
                  pl.BlockSpec((d, d), c2), pl.BlockSpec((d, d), c2), pl.BlockSpec((d, d), c2),
                  pl.BlockSpec((1, d), c2), pl.BlockSpec((d, LANES), c2)],
        out_specs=[tile, tile, pl.BlockSpec((1, tm, LANES), lambda bi, i: (bi, i, 0))],
        out_shape=[jax.ShapeDtypeStruct((b, s, d), F32), jax.ShapeDtypeStruct((b, s, d), BF16),
                   jax.ShapeDtypeStruct((b, s, LANES), F32)],
        scratch_shapes=[pltpu.VMEM((tm + 2 * POOL_HALO, d), F32)],
        compiler_params=_params(("parallel", "parallel")),
        name="mix",
    )(u, u, u, o, gp, ga, x, wmix, ps, wpo, wao, wo, gffn, wr)


PREFIX_BLOCK = 256


def _prefix_rows(mask, tri):
    out = []
    carry = jnp.zeros((1, mask.shape[1]), F32)
    for blk in range(mask.shape[0] // PREFIX_BLOCK):
        m = mask[blk * PREFIX_BLOCK:(blk + 1) * PREFIX_BLOCK]
        p = _dot(tri, m.astype(BF16)) + carry
        carry = p[PREFIX_BLOCK - 1:PREFIX_BLOCK]
        out.append(p)
    return jnp.concatenate(out, axis=0)


def _route_kernel(cap, lg_ref, slot_ref, aff_ref):
    lg = lg_ref[0]
    lane = lax.broadcasted_iota(I32, lg.shape, 1)
    valid = lane < N_EXPERTS
    lg = jnp.where(valid, lg, -jnp.inf)
    ex = jnp.exp(lg - jnp.max(lg, axis=-1, keepdims=True))
    aff = ex / jnp.sum(ex, axis=-1, keepdims=True)
    aff_ref[0] = aff
    bits = lax.bitcast_convert_type(aff, I32)

    def step(k, ans):
        cand = ans | jnp.left_shift(jnp.int32(1), 30 - k)
        cnt = jnp.sum(jnp.where(bits >= cand, 1.0, 0.0), axis=0, keepdims=True)
        return jnp.where(cnt >= cap, cand, ans)

    thr = lax.fori_loop(0, 31, step, jnp.zeros((1, LANES), I32))
    gt = jnp.where(bits > thr, 1.0, 0.0)
    eq = jnp.where(bits == thr, 1.0, 0.0)
    r, c = lax.broadcasted_iota(I32, (PREFIX_BLOCK, PREFIX_BLOCK), 0), lax.broadcasted_iota(I32, (PREFIX_BLOCK, PREFIX_BLOCK), 1)
    tri = jnp.where(c <= r, 1.0, 0.0).astype(BF16)
    room = cap - jnp.sum(gt, axis=0, keepdims=True)
    eq_before = _prefix_rows(eq, tri) - eq
    sel = jnp.maximum(gt, jnp.where(eq_before < room, eq, 0.0))
    slot = _prefix_rows(sel, tri) - 1.0
    slot_ref[0] = jnp.where(valid & (sel > 0.0), slot, -1.0).astype(I32)


def _route(logits, cap):
    b, s, _ = logits.shape
    spec = pl.BlockSpec((1, s, LANES), lambda bi: (bi, 0, 0))
    return pl.pallas_call(
        functools.partial(_route_kernel, cap),
        grid=(b,),
        in_specs=[spec],
        out_specs=[spec, spec],
        out_shape=[jax.ShapeDtypeStruct((b, s, LANES), I32), jax.ShapeDtypeStruct((b, s, LANES), F32)],
        compiler_params=_params(("parallel",)),
        name="route",
    )(logits)


def _gather_kernel(cap, slot_ref, aff_ref, h_ref, xs_ref, val_ref):
    sid = slot_ref[0, 0]
    s = sid.shape[1]
    hit = sid == lax.broadcasted_iota(I32, (cap, s), 0)
    xs_ref[0] = _dot(jnp.where(hit, 1.0, 0.0).astype(BF16), h_ref[0]).astype(BF16)
    vals = jnp.sum(jnp.where(hit, aff_ref[0, 0], 0.0), axis=-1, keepdims=True)
    val_ref[0] = jnp.broadcast_to(vals, (cap, LANES))


def _gather(slot_row, aff_row, hn, cap):
    b, s, d = hn.shape
    rspec = pl.BlockSpec((1, 1, 1, s), lambda bi, e: (bi, e, 0, 0))
    return pl.pallas_call(
        functools.partial(_gather_kernel, cap),
        grid=(b, N_EXPERTS),
        in_specs=[rspec, rspec, pl.BlockSpec((1, s, d), lambda bi, e: (bi, 0, 0))],
        out_specs=[pl.BlockSpec((1, cap, d), lambda bi, e: (e, bi, 0)),
                   pl.BlockSpec((1, cap, LANES), lambda bi, e: (e, bi, 0))],
        out_shape=[jax.ShapeDtypeStruct((N_EXPERTS, b * cap, d), BF16),
                   jax.ShapeDtypeStruct((N_EXPERTS, b * cap, LANES), F32)],
        compiler_params=_params(("parallel", "arbitrary")),
        name="gather",
    )(slot_row, aff_row, hn)


def _ffn_kernel(xs_ref, wg_ref, wu_ref, wd_ref, val_ref, y_ref, acc_ref):
    f = pl.program_id(1)

    @pl.when(f == 0)
    def _():
        acc_ref[...] = jnp.zeros_like(acc_ref)

    xs = xs_ref[0]
    g = _dot(xs, wg_ref[0].astype(BF16))
    u = _dot(xs, wu_ref[0].astype(BF16))
    act = (g / (1.0 + jnp.exp(-g)) * u).astype(BF16)
    acc_ref[...] += _dot(act, wd_ref[0].astype(BF16))

    @pl.when(f == pl.num_programs(1) - 1)
    def _():
        y_ref[0] = (acc_ref[...] * val_ref[0][:, 0:1]).astype(BF16)


def _ffn(xs, w_gate, w_up, w_down, vals, tf=256):
    e, m, d = xs.shape
    ff = w_gate.shape[2]
    row = lambda ei, f: (ei, 0, 0)
    return pl.pallas_call(
        _ffn_kernel,
        grid=(e, ff // tf),
        in_specs=[pl.BlockSpec((1, m, d), row),
                  pl.BlockSpec((1, d, tf), lambda ei, f: (ei, 0, f)),
                  pl.BlockSpec((1, d, tf), lambda ei, f: (ei, 0, f)),
                  pl.BlockSpec((1, tf, d), lambda ei, f: (ei, f, 0)),
                  pl.BlockSpec((1, m, LANES), row)],
        out_specs=pl.BlockSpec((1, m, d), row),
        out_shape=jax.ShapeDtypeStruct((e, m, d), BF16),
        scratch_shapes=[pltpu.VMEM((m, d), F32)],
        compiler_params=_params(("parallel", "arbitrary")),
        name="ffn",
    )(xs, w_gate, w_up, w_down, vals)


def _combine_kernel(cap, x1_ref, slot_ref, y_ref, g_ref, o_ref):
    slot = slot_ref[0]
    tt = slot.shape[0]
    cidx = lax.broadcasted_iota(I32, (tt, cap), 1)
    acc = x1_ref[0]
    for e in range(N_EXPERTS):
        hit = slot[:, e:e + 1] == cidx
        acc = acc + _dot(jnp.where(hit, 1.0, 0.0).astype(BF16), y_ref[e])
    o_ref[0] = _rms(acc, g_ref[...])


def _combine(x1, slot_t, y, g_final, cap, tt=256):
    b, s, d = x1.shape
    tile = lambda bi, i: (bi, i, 0)
    return pl.pallas_call(
        functools.partial(_combine_kernel, cap),
        grid=(b, s // tt),
        in_specs=[pl.BlockSpec((1, tt, d), tile), pl.BlockSpec((1, tt, LANES), tile),
                  pl.BlockSpec((N_EXPERTS, cap, d), lambda bi, i: (0, bi, 0)),
                  pl.BlockSpec((1, d), lambda bi, i: (0, 0))],
        out_specs=pl.BlockSpec((1, tt, d), tile),
        out_shape=jax.ShapeDtypeStruct((b, s, d), F32),
        compiler_params=_params(("parallel", "arbitrary")),
        name="combine",
    )(x1, slot_t, y, g_final)


def _layer(x, positions, g_mix, w_in, w_pool_mix, pool_scale, w_pool_out, lam_q1, lam_k1, lam_q2,
           lam_k2, g_subln, w_attn_out, w_out, g_ffn, w_router, w_gate, w_up, w_down, lam_init):
    b, s, d = x.shape
    t = b * s
    cap = EC_CAPACITY_FACTOR * s // N_EXPERTS

    posb = jnp.broadcast_to(positions.reshape(t, 1).astype(F32), (t, LANES))
    inv = ROPE_THETA ** (-jnp.arange(ROT_HALF, dtype=F32) * 2.0 / ROT_DIM)
    lane = jnp.arange(LANES)
    inv_tab = jnp.where((lane % HEAD_DIM) < ROT_DIM, inv[lane % ROT_HALF], 0.0).reshape(1, LANES).astype(F32)

    u, q, k1, k2, v, gp, ga = _inproj(x.reshape(t, d), posb, g_mix.reshape(1, d), inv_tab, w_in.astype(BF16))
    sh = lambda a: a.reshape(b, s, d)

    lamv = jnp.zeros((8, LANES), F32).at[0:4, 0:HEAD_DIM].set(jnp.stack([lam_q1, lam_k1, lam_q2, lam_k2]).astype(F32))
    vt = jnp.transpose(sh(v), (0, 2, 1))
    o = _attention(sh(q), sh(k1), sh(k2), vt, lamv, g_subln.reshape(1, HEAD_WIDTH), lam_init)

    wr = jnp.zeros((d, LANES), BF16).at[:, :N_EXPERTS].set(w_router.astype(BF16))
    x1, hn, logits = _mix(sh(u), o, sh(gp), sh(ga), x, w_pool_mix.astype(BF16), pool_scale.reshape(1, d),
                          w_pool_out.astype(BF16), w_attn_out.astype(BF16), w_out.astype(BF16),
                          g_ffn.reshape(1, d), wr)

    slot_t, aff_t = _route(logits, cap)
    to_rows = lambda a: jnp.transpose(a[:, :, :N_EXPERTS], (0, 2, 1)).reshape(b, N_EXPERTS, 1, s)
    xs, vals = _gather(to_rows(slot_t), to_rows(aff_t), hn, cap)
    y = _ffn(xs, w_gate, w_up, w_down, vals)
    return x1, slot_t, y, cap


def kernel(x, positions, g_mix, w_in, w_pool_mix, pool_scale, w_pool_out, lam_q1, lam_k1, lam_q2, lam_k2,
           g_subln, w_attn_out, w_out, g_ffn, w_router, w_gate, w_up, w_down, g_final):
    depth = g_mix.shape[0]
    assert depth == 1, "the final norm is fused into the last layer's combine step"
    l = 0
    lam_init = 0.8 - 0.6 * math.exp(-0.3 * l)
    x1, slot_t, y, cap = _layer(x, positions, g_mix[l], w_in[l], w_pool_mix[l], pool_scale[l], w_pool_out[l],
                                lam_q1[l], lam_k1[l], lam_q2[l], lam_k2[l], g_subln[l], w_attn_out[l],
                                w_out[l], g_ffn[l], w_router[l], w_gate[l], w_up[l], w_down[l], lam_init)
    return _combine(x1, slot_t, y, g_final.reshape(1, D_MODEL), cap)
```

```python
import functools
import math

import jax
import jax.numpy as jnp
from jax import lax
from jax.experimental import pallas as pl
from jax.experimental.pallas import tpu as pltpu

F32 = jnp.float32
BF16 = jnp.bfloat16
I32 = jnp.int32

D_MODEL = 1024
N_HEADS = 8
HEAD_DIM = 64
HEAD_WIDTH = 2 * HEAD_DIM
ROT_DIM = HEAD_DIM // 4
ROT_HALF = ROT_DIM // 2
ROPE_THETA = 500000.0
POOL_WINDOWS = (2, 4, 8, 16)
POOL_GROUP_DIM = D_MODEL // len(POOL_WINDOWS)
POOL_HALO = 16
N_EXPERTS = 16
EC_CAPACITY_FACTOR = 2
RMS_EPS = 1e-6
Q_SCALE = HEAD_DIM ** -0.5 * math.log2(math.e)
LANES = 128

VMEM_LIMIT = 56 * 1024 * 1024


def _params(sem):
    return pltpu.CompilerParams(dimension_semantics=sem, vmem_limit_bytes=VMEM_LIMIT)


def _rms(x, g):
    return (x * lax.rsqrt(jnp.mean(x * x, axis=-1, keepdims=True) + RMS_EPS)) * g


def _dot(a, b):
    return jnp.dot(a, b, preferred_element_type=F32)


def _rotary(z, cos, sin, first_half):
    up = pltpu.roll(z, LANES - ROT_HALF, 1)
    down = pltpu.roll(z, ROT_HALF, 1)
    partner = jnp.where(first_half, -up, down)
    return z * cos + partner * sin


def _inproj_kernel(x_ref, pos_ref, g_ref, inv_ref, w_ref,
                   u_ref, q_ref, k1_ref, k2_ref, v_ref, gp_ref, ga_ref):
    hb = _rms(x_ref[...], g_ref[...]).astype(BF16)
    ang = pos_ref[...] * inv_ref[...]
    cos = jnp.cos(ang)
    sin = jnp.sin(ang)
    lane = lax.broadcasted_iota(I32, ang.shape, 1)
    first_half = (lane % HEAD_DIM) < ROT_HALF
    map1 = lane < HEAD_DIM

    def col(j):
        return _dot(hb, w_ref[:, j * D_MODEL:(j + 1) * D_MODEL])

    u_ref[...] = col(0)
    zq = col(1)
    zk = col(2)
    for h in range(N_HEADS):
        sl = slice(h * HEAD_WIDTH, (h + 1) * HEAD_WIDTH)
        q_ref[:, sl] = (_rotary(zq[:, sl], cos, sin, first_half) * Q_SCALE).astype(BF16)
        kr = _rotary(zk[:, sl], cos, sin, first_half)
        k1_ref[:, sl] = jnp.where(map1, kr, 0.0).astype(BF16)
        k2_ref[:, sl] = jnp.where(map1, 0.0, kr).astype(BF16)
    v_ref[...] = col(3).astype(BF16)
    gp_ref[...] = 1.0 / (1.0 + jnp.exp(-col(4)))
    ga_ref[...] = 1.0 / (1.0 + jnp.exp(-col(5)))


def _inproj(x2d, posb, g_mix, inv_tab, w_in_bf, tm=256):
    t = x2d.shape[0]
    n_cols = w_in_bf.shape[1]
    row = lambda i: (i, 0)
    const = lambda i: (0, 0)
    tile = pl.BlockSpec((tm, D_MODEL), row)
    out_shapes = [jax.ShapeDtypeStruct((t, D_MODEL), dt) for dt in (F32, BF16, BF16, BF16, BF16, F32, F32)]
    return pl.pallas_call(
        _inproj_kernel,
        grid=(t // tm,),
        in_specs=[tile, pl.BlockSpec((tm, LANES), row), pl.BlockSpec((1, D_MODEL), const),
                  pl.BlockSpec((1, LANES), const), pl.BlockSpec((D_MODEL, n_cols), const)],
        out_specs=[tile] * 7,
        out_shape=out_shapes,
        compiler_params=_params(("parallel",)),
        name="inproj",
    )(x2d, posb, g_mix, inv_tab, w_in_bf)


SCORE_SLOTS = 2
SUM_ROWS = 16

def _attn_kernel(lam_init, tk, q_ref, k1_ref, k2_ref, vt_ref, lamv_ref, g_ref, o_ref, st_ref, acc_ref):
    lv = lamv_ref[...]
    lam = (jnp.exp(jnp.sum(lv[0:1] * lv[1:2], axis=-1, keepdims=True))
           - jnp.exp(jnp.sum(lv[2:3] * lv[3:4], axis=-1, keepdims=True)) + lam_init)
    q = q_ref[0]
    tq = q.shape[0]
    n_chunks = k1_ref.shape[1] // tk
    nt = (((1,), (1,)), ((), ()))
    k_refs = (k1_ref, k2_ref)

    def scores(c):
        for mp in range(2):
            st_ref[c % SCORE_SLOTS, mp] = lax.dot_general(k_refs[mp][0, c * tk:(c + 1) * tk, :], q, nt,
                                                          preferred_element_type=F32)

    ones_rows = jnp.ones((SUM_ROWS, tk), BF16)

    def accumulate(c, m):
        out_m = []
        for mp in range(2):
            st = st_ref[c % SCORE_SLOTS, mp]
            m_new = jnp.maximum(m[mp], jnp.max(st, axis=0, keepdims=True))
            alpha = jnp.exp2(m[mp] - m_new)
            et = jnp.exp2(st - m_new).astype(BF16)
            vt_ext = jnp.concatenate([vt_ref[0, :, c * tk:(c + 1) * tk], ones_rows], axis=0)
            pv = _dot(vt_ext, et)
            acc_ref[mp] = pv if c == 0 else acc_ref[mp] * alpha + pv
            out_m.append(m_new)
        return out_m

    m = [jnp.full((1, tq), -jnp.inf, F32)] * 2
    scores(0)
    for c in range(n_chunks):
        if c + 1 < n_chunks:
            scores(c + 1)
        m = accumulate(c, m)
    a1, a2 = acc_ref[0], acc_ref[1]
    r1 = 1.0 / a1[HEAD_WIDTH:HEAD_WIDTH + 1]
    r2 = lam / a2[HEAD_WIDTH:HEAD_WIDTH + 1]
    ot = a1[:HEAD_WIDTH] * r1 - a2[:HEAD_WIDTH] * r2
    o_ref[0] = (_rms(ot.T, g_ref[...]) * (1.0 - lam_init)).astype(BF16)


def _attention(q, k1, k2, vt, lamv, g_subln, lam_init, tq=1024, tk=256):
    b, s, _ = q.shape
    qspec = pl.BlockSpec((1, tq, HEAD_WIDTH), lambda bi, h, qi: (bi, qi, h))
    kspec = pl.BlockSpec((1, s, HEAD_WIDTH), lambda bi, h, qi: (bi, 0, h))
    vspec = pl.BlockSpec((1, HEAD_WIDTH, s), lambda bi, h, qi: (bi, h, 0))
    const = lambda bi, h, qi: (0, 0)
    return pl.pallas_call(
        functools.partial(_attn_kernel, lam_init, tk),
        grid=(b, N_HEADS, s // tq),
        in_specs=[qspec, kspec, kspec, vspec,
                  pl.BlockSpec((8, LANES), const), pl.BlockSpec((1, HEAD_WIDTH), const)],
        out_specs=qspec,
        out_shape=jax.ShapeDtypeStruct(q.shape, BF16),
        scratch_shapes=[pltpu.VMEM((SCORE_SLOTS, 2, tk, tq), F32), pltpu.VMEM((2, HEAD_WIDTH + SUM_ROWS, tq), F32)],
        compiler_params=_params(("parallel", "parallel", "arbitrary")),
        name="attn",
    )(q, k1, k2, vt, lamv, g_subln)


def _mix_kernel(seq, u_ref, up_ref, un_ref, o_ref, gp_ref, ga_ref, x_ref,
                wmix_ref, ps_ref, wpo_ref, wao_ref, wo_ref, gffn_ref, wr_ref,
                x1_ref, hn_ref, lg_ref, upad_ref):
    tm = u_ref.shape[1]
    i = pl.program_id(1)
    nt = pl.num_programs(1)
    main = u_ref[0]
    upad_ref[0:POOL_HALO, :] = jnp.where(i > 0, up_ref[0], 0.0)
    upad_ref[POOL_HALO:POOL_HALO + tm, :] = main
    upad_ref[POOL_HALO + tm:, :] = jnp.where(i < nt - 1, un_ref[0], 0.0)
    t = i * tm + lax.broadcasted_iota(I32, (tm, 1), 0)
    mixed = []
    for g, w in enumerate(POOL_WINDOWS):
        cols = slice(g * POOL_GROUP_DIM, (g + 1) * POOL_GROUP_DIM)
        win = upad_ref[POOL_HALO - w // 2:POOL_HALO - w // 2 + tm, cols]
        for d in range(-w // 2 + 1, w // 2):
            win = win + upad_ref[POOL_HALO + d:POOL_HALO + d + tm, cols]
        cnt = (jnp.minimum(t + w // 2, seq) - jnp.maximum(t - w // 2, 0)).astype(F32)
        pooled = win / cnt - main[:, cols]
        mixed.append((_dot(pooled.astype(BF16), wmix_ref[g]) * ps_ref[:, cols]).astype(BF16))
    y_pool = _dot(jnp.concatenate(mixed, axis=-1), wpo_ref[...])
    y_attn = _dot(o_ref[0], wao_ref[...])
    merged = gp_ref[0] * y_pool + ga_ref[0] * y_attn
    x1 = x_ref[0] + _dot(merged.astype(BF16), wo_ref[...])
    x1_ref[0] = x1
    hn = _rms(x1, gffn_ref[...]).astype(BF16)
    hn_ref[0] = hn
    lg_ref[0] = _dot(hn, wr_ref[...])


def _mix(u, o, gp, ga, x, wmix, ps, wpo, wao, wo, gffn, wr, tm=256):
    b, s, d = x.shape
    hb = tm // POOL_HALO
    n_halo = s // POOL_HALO
    tile = pl.BlockSpec((1, tm, d), lambda bi, i: (bi, i, 0))
    prev = pl.BlockSpec((1, POOL_HALO, d), lambda bi, i: (bi, jnp.maximum(i * hb - 1, 0), 0))
    nxt = pl.BlockSpec((1, POOL_HALO, d), lambda bi, i: (bi, jnp.minimum((i + 1) * hb, n_halo - 1), 0))
    c2 = lambda bi, i: (0, 0)
    c3 = lambda bi, i: (0, 0, 0)
    return pl.pallas_call(
        functools.partial(_mix_kernel, s),
        grid=(b, s // tm),
        in_specs=[tile, prev, nxt, tile, tile, tile, tile,
                  pl.BlockSpec(wmix.shape, c3), pl.BlockSpec((1, d), c2),
                  pl.BlockSpec((d, d), c2), pl.BlockSpec((d, d), c2), pl.BlockSpec((d, d), c2),
                  pl.BlockSpec((1, d), c2), pl.BlockSpec((d, LANES), c2)],
        out_specs=[tile, tile, pl.BlockSpec((1, tm, LANES), lambda bi, i: (bi, i, 0))],
        out_shape=[jax.ShapeDtypeStruct((b, s, d), F32), jax.ShapeDtypeStruct((b, s, d), BF16),
                   jax.ShapeDtypeStruct((b, s, LANES), F32)],
        scratch_shapes=[pltpu.VMEM((tm + 2 * POOL_HALO, d), F32)],
        compiler_params=_params(("parallel", "parallel")),
        name="mix",
    )(u, u, u, o, gp, ga, x, wmix, ps, wpo, wao, wo, gffn, wr)


PREFIX_BLOCK = 256
BF16_ROWS = 16


def _prefix_rows(mask, tri):
    out = []
    carries = [jnp.zeros((1, mask.shape[1]), F32)]
    for blk in range(mask.shape[0] // PREFIX_BLOCK):
        m = mask[blk * PREFIX_BLOCK:(blk + 1) * PREFIX_BLOCK]
        p = _dot(tri, m.astype(BF16)) + carries[-1]
        carries.append(p[PREFIX_BLOCK - 1:PREFIX_BLOCK])
        out.append(p)
    return jnp.concatenate(out, axis=0), jnp.concatenate(carries, axis=0)


def _route_kernel(cap, lg_ref, slot_ref, aff_ref, start_ref):
    lg = lg_ref[0]
    lane = lax.broadcasted_iota(I32, lg.shape, 1)
    valid = lane < N_EXPERTS
    lg = jnp.where(valid, lg, -jnp.inf)
    ex = jnp.exp(lg - jnp.max(lg, axis=-1, keepdims=True))
    aff = ex / jnp.sum(ex, axis=-1, keepdims=True)
    aff_ref[0] = aff
    bits = lax.bitcast_convert_type(aff, I32)

    def step(k, ans):
        cand = ans | jnp.left_shift(jnp.int32(1), 30 - k)
        cnt = jnp.sum(jnp.where(bits >= cand, 1.0, 0.0), axis=0, keepdims=True)
        return jnp.where(cnt >= cap, cand, ans)

    thr = lax.fori_loop(0, 31, step, jnp.zeros((1, LANES), I32))
    gt = jnp.where(bits > thr, 1.0, 0.0)
    eq = jnp.where(bits == thr, 1.0, 0.0)
    r, c = lax.broadcasted_iota(I32, (PREFIX_BLOCK, PREFIX_BLOCK), 0), lax.broadcasted_iota(I32, (PREFIX_BLOCK, PREFIX_BLOCK), 1)
    tri = jnp.where(c <= r, 1.0, 0.0).astype(BF16)
    room = cap - jnp.sum(gt, axis=0, keepdims=True)
    eq_before = _prefix_rows(eq, tri)[0] - eq
    sel = jnp.maximum(gt, jnp.where(eq_before < room, eq, 0.0))
    chosen_upto, block_start = _prefix_rows(sel, tri)
    slot_ref[0] = jnp.where(valid & (sel > 0.0), chosen_upto - 1.0, -1.0).astype(I32)
    start_ref[0] = block_start.astype(I32)


def _route(logits, cap):
    b, s, _ = logits.shape
    n_starts = s // PREFIX_BLOCK + 1
    spec = pl.BlockSpec((1, s, LANES), lambda bi: (bi, 0, 0))
    return pl.pallas_call(
        functools.partial(_route_kernel, cap),
        grid=(b,),
        in_specs=[spec],
        out_specs=[spec, spec, pl.BlockSpec((1, n_starts, LANES), lambda bi: (bi, 0, 0))],
        out_shape=[jax.ShapeDtypeStruct((b, s, LANES), I32), jax.ShapeDtypeStruct((b, s, LANES), F32),
                   jax.ShapeDtypeStruct((b, n_starts, LANES), I32)],
        compiler_params=_params(("parallel",)),
        name="route",
    )(logits)


def _slot_window(start_ref, base, cap):
    first = start_ref[base]
    end = start_ref[base + N_EXPERTS]
    w0 = jnp.minimum((first // BF16_ROWS) * BF16_ROWS, cap - PREFIX_BLOCK)
    return w0, end - w0 <= PREFIX_BLOCK


def _gather_kernel(cap, start_ref, slot_ref, h_ref, xs_ref, acc_ref):
    b = pl.program_id(0)
    e = pl.program_id(1)
    sid = slot_ref[0, 0]
    s = sid.shape[1]
    n_blk = s // PREFIX_BLOCK
    windows = [_slot_window(start_ref, (b * (n_blk + 1) + j) * N_EXPERTS + e, cap) for j in range(n_blk)]
    fits = functools.reduce(jnp.logical_and, [ok for _, ok in windows])

    @pl.when(fits)
    def _():
        acc_ref[...] = jnp.zeros_like(acc_ref)
        rows = lax.broadcasted_iota(I32, (PREFIX_BLOCK, PREFIX_BLOCK), 0)
        for j, (w0, _) in enumerate(windows):
            tok = slice(j * PREFIX_BLOCK, (j + 1) * PREFIX_BLOCK)
            hit = (sid[:, tok] - w0) == rows
            win = pl.ds(pl.multiple_of(w0, BF16_ROWS), PREFIX_BLOCK)
            acc_ref[win, :] += _dot(jnp.where(hit, 1.0, 0.0).astype(BF16), h_ref[0, tok, :])
        xs_ref[0] = acc_ref[...].astype(BF16)

    @pl.when(jnp.logical_not(fits))
    def _():
        hit = sid == lax.broadcasted_iota(I32, (cap, s), 0)
        xs_ref[0] = _dot(jnp.where(hit, 1.0, 0.0).astype(BF16), h_ref[0]).astype(BF16)


def _gather(starts, slot_row, hn, cap):
    b, s, d = hn.shape
    return pl.pallas_call(
        functools.partial(_gather_kernel, cap),
        grid_spec=pltpu.PrefetchScalarGridSpec(
            num_scalar_prefetch=1,
            grid=(b, N_EXPERTS),
            in_specs=[pl.BlockSpec((1, 1, 1, s), lambda bi, e, st: (bi, e, 0, 0)),
                      pl.BlockSpec((1, s, d), lambda bi, e, st: (bi, 0, 0))],
            out_specs=pl.BlockSpec((1, cap, d), lambda bi, e, st: (e, bi, 0)),
            scratch_shapes=[pltpu.VMEM((cap, d), F32)]),
        out_shape=jax.ShapeDtypeStruct((N_EXPERTS, b * cap, d), BF16),
        compiler_params=_params(("arbitrary", "arbitrary")),
        name="gather",
    )(starts, slot_row, hn)


def _ffn_kernel(xs_ref, wg_ref, wu_ref, wd_ref, y_ref, acc_ref):
    f = pl.program_id(1)

    @pl.when(f == 0)
    def _():
        acc_ref[...] = jnp.zeros_like(acc_ref)

    xs = xs_ref[0]
    g = _dot(xs, wg_ref[0].astype(BF16))
    u = _dot(xs, wu_ref[0].astype(BF16))
    act = (g / (1.0 + jnp.exp(-g)) * u).astype(BF16)
    acc_ref[...] += _dot(act, wd_ref[0].astype(BF16))

    @pl.when(f == pl.num_programs(1) - 1)
    def _():
        y_ref[0] = acc_ref[...].astype(BF16)


def _ffn(xs, w_gate, w_up, w_down, tf=256):
    e, m, d = xs.shape
    ff = w_gate.shape[2]
    row = lambda ei, f: (ei, 0, 0)
    return pl.pallas_call(
        _ffn_kernel,
        grid=(e, ff // tf),
        in_specs=[pl.BlockSpec((1, m, d), row),
                  pl.BlockSpec((1, d, tf), lambda ei, f: (ei, 0, f)),
                  pl.BlockSpec((1, d, tf), lambda ei, f: (ei, 0, f)),
                  pl.BlockSpec((1, tf, d), lambda ei, f: (ei, f, 0))],
        out_specs=pl.BlockSpec((1, m, d), row),
        out_shape=jax.ShapeDtypeStruct((e, m, d), BF16),
        scratch_shapes=[pltpu.VMEM((m, d), F32)],
        compiler_params=_params(("parallel", "arbitrary")),
        name="ffn",
    )(xs, w_gate, w_up, w_down)


def _combine_kernel(cap, start_ref, x1_ref, slot_ref, aff_ref, y_ref, g_ref, o_ref):
    b = pl.program_id(0)
    i = pl.program_id(1)
    slot = slot_ref[0]
    aff = aff_ref[0]
    base = (b * (pl.num_programs(1) + 1) + i) * N_EXPERTS
    windows = [_slot_window(start_ref, base + e, cap) for e in range(N_EXPERTS)]
    fits = functools.reduce(jnp.logical_and, [ok for _, ok in windows])

    def finish(acc):
        o_ref[0] = _rms(acc, g_ref[...])

    @pl.when(fits)
    def _():
        cidx = lax.broadcasted_iota(I32, (PREFIX_BLOCK, PREFIX_BLOCK), 1)
        acc = x1_ref[0]
        for e, (w0, _) in enumerate(windows):
            hit = (slot[:, e:e + 1] - w0) == cidx
            rows = y_ref[e, pl.ds(pl.multiple_of(w0, BF16_ROWS), PREFIX_BLOCK), :]
            acc = acc + aff[:, e:e + 1] * _dot(jnp.where(hit, 1.0, 0.0).astype(BF16), rows)
        finish(acc)

    @pl.when(jnp.logical_not(fits))
    def _():
        cidx = lax.broadcasted_iota(I32, (PREFIX_BLOCK, cap), 1)
        acc = x1_ref[0]
        for e in range(N_EXPERTS):
            hit = slot[:, e:e + 1] == cidx
            acc = acc + aff[:, e:e + 1] * _dot(jnp.where(hit, 1.0, 0.0).astype(BF16), y_ref[e])
        finish(acc)


def _combine(starts, x1, slot_t, aff_t, y, g_final, cap):
    b, s, d = x1.shape
    tile = lambda bi, i, st: (bi, i, 0)
    return pl.pallas_call(
        functools.partial(_combine_kernel, cap),
        grid_spec=pltpu.PrefetchScalarGridSpec(
            num_scalar_prefetch=1,
            grid=(b, s // PREFIX_BLOCK),
            in_specs=[pl.BlockSpec((1, PREFIX_BLOCK, d), tile), pl.BlockSpec((1, PREFIX_BLOCK, LANES), tile),
                      pl.BlockSpec((1, PREFIX_BLOCK, LANES), tile),
                      pl.BlockSpec((N_EXPERTS, cap, d), lambda bi, i, st: (0, bi, 0)),
                      pl.BlockSpec((1, d), lambda bi, i, st: (0, 0))],
            out_specs=pl.BlockSpec((1, PREFIX_BLOCK, d), tile)),
        out_shape=jax.ShapeDtypeStruct((b, s, d), F32),
        compiler_params=_params(("arbitrary", "arbitrary")),
        name="combine",
    )(starts, x1, slot_t, aff_t, y, g_final)


def _layer(x, positions, g_mix, w_in, w_pool_mix, pool_scale, w_pool_out, lam_q1, lam_k1, lam_q2,
           lam_k2, g_subln, w_attn_out, w_out, g_ffn, w_router, w_gate, w_up, w_down, lam_init):
    b, s, d = x.shape
    t = b * s
    cap = EC_CAPACITY_FACTOR * s // N_EXPERTS

    posb = jnp.broadcast_to(positions.reshape(t, 1).astype(F32), (t, LANES))
    inv = ROPE_THETA ** (-jnp.arange(ROT_HALF, dtype=F32) * 2.0 / ROT_DIM)
    lane = jnp.arange(LANES)
    inv_tab = jnp.where((lane % HEAD_DIM) < ROT_DIM, inv[lane % ROT_HALF], 0.0).reshape(1, LANES).astype(F32)

    u, q, k1, k2, v, gp, ga = _inproj(x.reshape(t, d), posb, g_mix.reshape(1, d), inv_tab, w_in.astype(BF16))
    sh = lambda a: a.reshape(b, s, d)

    lamv = jnp.zeros((8, LANES), F32).at[0:4, 0:HEAD_DIM].set(jnp.stack([lam_q1, lam_k1, lam_q2, lam_k2]).astype(F32))
    vt = jnp.transpose(sh(v), (0, 2, 1))
    o = _attention(sh(q), sh(k1), sh(k2), vt, lamv, g_subln.reshape(1, HEAD_WIDTH), lam_init)

    wr = jnp.zeros((d, LANES), BF16).at[:, :N_EXPERTS].set(w_router.astype(BF16))
    x1, hn, logits = _mix(sh(u), o, sh(gp), sh(ga), x, w_pool_mix.astype(BF16), pool_scale.reshape(1, d),
                          w_pool_out.astype(BF16), w_attn_out.astype(BF16), w_out.astype(BF16),
                          g_ffn.reshape(1, d), wr)

    slot_t, aff_t, starts = _route(logits, cap)
    starts = starts[:, :, :N_EXPERTS].reshape(-1)
    slot_row = jnp.transpose(slot_t[:, :, :N_EXPERTS], (0, 2, 1)).reshape(b, N_EXPERTS, 1, s)
    xs = _gather(starts, slot_row, hn, cap)
    y = _ffn(xs, w_gate, w_up, w_down)
    return x1, starts, slot_t, aff_t, y, cap


def kernel(x, positions, g_mix, w_in, w_pool_mix, pool_scale, w_pool_out, lam_q1, lam_k1, lam_q2, lam_k2,
           g_subln, w_attn_out, w_out, g_ffn, w_router, w_gate, w_up, w_down, g_final):
    depth = g_mix.shape[0]
    assert depth == 1, "the final norm is fused into the last layer's combine step"
    l = 0
    lam_init = 0.8 - 0.6 * math.exp(-0.3 * l)
    x1, starts, slot_t, aff_t, y, cap = _layer(
        x, positions, g_mix[l], w_in[l], w_pool_mix[l], pool_scale[l], w_pool_out[l], lam_q1[l], lam_k1[l],
        lam_q2[l], lam_k2[l], g_subln[l], w_attn_out[l], w_out[l], g_ffn[l], w_router[l], w_gate[l], w_up[l],
        w_down[l], lam_init)
    return _combine(starts, x1, slot_t, aff_t, y, g_final.reshape(1, D_MODEL), cap)
```

```python
import functools
import math

import jax
import jax.numpy as jnp
from jax import lax
from jax.experimental import pallas as pl
from jax.experimental.pallas import tpu as pltpu

F32 = jnp.float32
BF16 = jnp.bfloat16
I32 = jnp.int32

D_MODEL = 1024
N_HEADS = 8
HEAD_DIM = 64
HEAD_WIDTH = 2 * HEAD_DIM
ROT_DIM = HEAD_DIM // 4
ROT_HALF = ROT_DIM // 2
ROPE_THETA = 500000.0
POOL_WINDOWS = (2, 4, 8, 16)
POOL_GROUP_DIM = D_MODEL // len(POOL_WINDOWS)
POOL_HALO = 16
N_EXPERTS = 16
EC_CAPACITY_FACTOR = 2
RMS_EPS = 1e-6
Q_SCALE = HEAD_DIM ** -0.5 * math.log2(math.e)
LANES = 128

VMEM_LIMIT = 56 * 1024 * 1024


def _params(sem):
    return pltpu.CompilerParams(dimension_semantics=sem, vmem_limit_bytes=VMEM_LIMIT)


def _rms(x, g):
    return (x * lax.rsqrt(jnp.mean(x * x, axis=-1, keepdims=True) + RMS_EPS)) * g


def _dot(a, b):
    return jnp.dot(a, b, preferred_element_type=F32)


def _rotary(z, cos, sin, first_half):
    up = pltpu.roll(z, LANES - ROT_HALF, 1)
    down = pltpu.roll(z, ROT_HALF, 1)
    partner = jnp.where(first_half, -up, down)
    return z * cos + partner * sin


def _inproj_kernel(x_ref, pos_ref, g_ref, inv_ref, w_ref,
                   u_ref, q_ref, k1_ref, k2_ref, v_ref, gp_ref, ga_ref):
    hb = _rms(x_ref[...], g_ref[...]).astype(BF16)
    ang = pos_ref[...] * inv_ref[...]
    cos = jnp.cos(ang)
    sin = jnp.sin(ang)
    lane = lax.broadcasted_iota(I32, ang.shape, 1)
    first_half = (lane % HEAD_DIM) < ROT_HALF
    map1 = lane < HEAD_DIM

    def col(j):
        return _dot(hb, w_ref[:, j * D_MODEL:(j + 1) * D_MODEL])

    u_ref[...] = col(0)
    zq = col(1)
    zk = col(2)
    for h in range(N_HEADS):
        sl = slice(h * HEAD_WIDTH, (h + 1) * HEAD_WIDTH)
        q_ref[:, sl] = (_rotary(zq[:, sl], cos, sin, first_half) * Q_SCALE).astype(BF16)
        kr = _rotary(zk[:, sl], cos, sin, first_half)
        k1_ref[:, sl] = jnp.where(map1, kr, 0.0).astype(BF16)
        k2_ref[:, sl] = jnp.where(map1, 0.0, kr).astype(BF16)
    v_ref[...] = col(3).astype(BF16)
    gp_ref[...] = 1.0 / (1.0 + jnp.exp(-col(4)))
    ga_ref[...] = 1.0 / (1.0 + jnp.exp(-col(5)))


def _inproj(x2d, posb, g_mix, inv_tab, w_in_bf, tm=256):
    t = x2d.shape[0]
    n_cols = w_in_bf.shape[1]
    row = lambda i: (i, 0)
    const = lambda i: (0, 0)
    tile = pl.BlockSpec((tm, D_MODEL), row)
    out_shapes = [jax.ShapeDtypeStruct((t, D_MODEL), dt) for dt in (F32, BF16, BF16, BF16, BF16, F32, F32)]
    return pl.pallas_call(
        _inproj_kernel,
        grid=(t // tm,),
        in_specs=[tile, pl.BlockSpec((tm, LANES), row), pl.BlockSpec((1, D_MODEL), const),
                  pl.BlockSpec((1, LANES), const), pl.BlockSpec((D_MODEL, n_cols), const)],
        out_specs=[tile] * 7,
        out_shape=out_shapes,
        compiler_params=_params(("parallel",)),
        name="inproj",
    )(x2d, posb, g_mix, inv_tab, w_in_bf)


SCORE_SLOTS = 2
SUM_ROWS = 16

def _attn_kernel(lam_init, tk, q_ref, k1_ref, k2_ref, vt_ref, lamv_ref, g_ref, o_ref, st_ref, acc_ref):
    lv = lamv_ref[...]
    lam = (jnp.exp(jnp.sum(lv[0:1] * lv[1:2], axis=-1, keepdims=True))
           - jnp.exp(jnp.sum(lv[2:3] * lv[3:4], axis=-1, keepdims=True)) + lam_init)
    q = q_ref[0]
    tq = q.shape[0]
    n_chunks = k1_ref.shape[1] // tk
    nt = (((1,), (1,)), ((), ()))
    k_refs = (k1_ref, k2_ref)

    def scores(c):
        for mp in range(2):
            st_ref[c % SCORE_SLOTS, mp] = lax.dot_general(k_refs[mp][0, c * tk:(c + 1) * tk, :], q, nt,
                                                          preferred_element_type=F32)

    ones_rows = jnp.ones((SUM_ROWS, tk), BF16)

    def accumulate(c, m):
        out_m = []
        for mp in range(2):
            st = st_ref[c % SCORE_SLOTS, mp]
            m_new = jnp.maximum(m[mp], jnp.max(st, axis=0, keepdims=True))
            alpha = jnp.exp2(m[mp] - m_new)
            et = jnp.exp2(st - m_new).astype(BF16)
            vt_ext = jnp.concatenate([vt_ref[0, :, c * tk:(c + 1) * tk], ones_rows], axis=0)
            pv = _dot(vt_ext, et)
            acc_ref[mp] = pv if c == 0 else acc_ref[mp] * alpha + pv
            out_m.append(m_new)
        return out_m

    m = [jnp.full((1, tq), -jnp.inf, F32)] * 2
    scores(0)
    for c in range(n_chunks):
        if c + 1 < n_chunks:
            scores(c + 1)
        m = accumulate(c, m)
    a1, a2 = acc_ref[0], acc_ref[1]
    r1 = 1.0 / a1[HEAD_WIDTH:HEAD_WIDTH + 1]
    r2 = lam / a2[HEAD_WIDTH:HEAD_WIDTH + 1]
    ot = a1[:HEAD_WIDTH] * r1 - a2[:HEAD_WIDTH] * r2
    o_ref[0] = (_rms(ot.T, g_ref[...]) * (1.0 - lam_init)).astype(BF16)


def _attention(q, k1, k2, vt, lamv, g_subln, lam_init, tq=1024, tk=256):
    b, s, _ = q.shape
    qspec = pl.BlockSpec((1, tq, HEAD_WIDTH), lambda bi, h, qi: (bi, qi, h))
    kspec = pl.BlockSpec((1, s, HEAD_WIDTH), lambda bi, h, qi: (bi, 0, h))
    vspec = pl.BlockSpec((1, HEAD_WIDTH, s), lambda bi, h, qi: (bi, h, 0))
    const = lambda bi, h, qi: (0, 0)
    return pl.pallas_call(
        functools.partial(_attn_kernel, lam_init, tk),
        grid=(b, N_HEADS, s // tq),
        in_specs=[qspec, kspec, kspec, vspec,
                  pl.BlockSpec((8, LANES), const), pl.BlockSpec((1, HEAD_WIDTH), const)],
        out_specs=qspec,
        out_shape=jax.ShapeDtypeStruct(q.shape, BF16),
        scratch_shapes=[pltpu.VMEM((SCORE_SLOTS, 2, tk, tq), F32), pltpu.VMEM((2, HEAD_WIDTH + SUM_ROWS, tq), F32)],
        compiler_params=_params(("parallel", "parallel", "arbitrary")),
        name="attn",
    )(q, k1, k2, vt, lamv, g_subln)


def _mix_kernel(seq, u_ref, up_ref, un_ref, o_ref, gp_ref, ga_ref, x_ref,
                wmix_ref, ps_ref, wpo_ref, wao_ref, wo_ref, gffn_ref, wr_ref,
                x1_ref, hn_ref, lg_ref, upad_ref):
    tm = u_ref.shape[1]
    i = pl.program_id(1)
    nt = pl.num_programs(1)
    main = u_ref[0]
    upad_ref[0:POOL_HALO, :] = jnp.where(i > 0, up_ref[0], 0.0)
    upad_ref[POOL_HALO:POOL_HALO + tm, :] = main
    upad_ref[POOL_HALO + tm:, :] = jnp.where(i < nt - 1, un_ref[0], 0.0)
    t = i * tm + lax.broadcasted_iota(I32, (tm, 1), 0)
    mixed = []
    for g, w in enumerate(POOL_WINDOWS):
        cols = slice(g * POOL_GROUP_DIM, (g + 1) * POOL_GROUP_DIM)
        win = upad_ref[POOL_HALO - w // 2:POOL_HALO - w // 2 + tm, cols]
        for d in range(-w // 2 + 1, w // 2):
            win = win + upad_ref[POOL_HALO + d:POOL_HALO + d + tm, cols]
        cnt = (jnp.minimum(t + w // 2, seq) - jnp.maximum(t - w // 2, 0)).astype(F32)
        pooled = win / cnt - main[:, cols]
        mixed.append((_dot(pooled.astype(BF16), wmix_ref[g]) * ps_ref[:, cols]).astype(BF16))
    y_pool = _dot(jnp.concatenate(mixed, axis=-1), wpo_ref[...])
    y_attn = _dot(o_ref[0], wao_ref[...])
    merged = gp_ref[0] * y_pool + ga_ref[0] * y_attn
    x1 = x_ref[0] + _dot(merged.astype(BF16), wo_ref[...])
    x1_ref[0] = x1
    hn = _rms(x1, gffn_ref[...]).astype(BF16)
    hn_ref[0] = hn
    lg_ref[0] = _dot(hn, wr_ref[...])


def _mix(u, o, gp, ga, x, wmix, ps, wpo, wao, wo, gffn, wr, tm=512):
    b, s, d = x.shape
    hb = tm // POOL_HALO
    n_halo = s // POOL_HALO
    tile = pl.BlockSpec((1, tm, d), lambda bi, i: (bi, i, 0))
    prev = pl.BlockSpec((1, POOL_HALO, d), lambda bi, i: (bi, jnp.maximum(i * hb - 1, 0), 0))
    nxt = pl.BlockSpec((1, POOL_HALO, d), lambda bi, i: (bi, jnp.minimum((i + 1) * hb, n_halo - 1), 0))
    c2 = lambda bi, i: (0, 0)
    c3 = lambda bi, i: (0, 0, 0)
    return pl.pallas_call(
        functools.partial(_mix_kernel, s),
        grid=(b, s // tm),
        in_specs=[tile, prev, nxt, tile, tile, tile, tile,
                  pl.BlockSpec(wmix.shape, c3), pl.BlockSpec((1, d), c2),
                  pl.BlockSpec((d, d), c2), pl.BlockSpec((d, d), c2), pl.BlockSpec((d, d), c2),
                  pl.BlockSpec((1, d), c2), pl.BlockSpec((d, LANES), c2)],
        out_specs=[tile, tile, pl.BlockSpec((1, tm, LANES), lambda bi, i: (bi, i, 0))],
        out_shape=[jax.ShapeDtypeStruct((b, s, d), F32), jax.ShapeDtypeStruct((b, s, d), BF16),
                   jax.ShapeDtypeStruct((b, s, LANES), F32)],
        scratch_shapes=[pltpu.VMEM((tm + 2 * POOL_HALO, d), F32)],
        compiler_params=_params(("parallel", "parallel")),
        name="mix",
    )(u, u, u, o, gp, ga, x, wmix, ps, wpo, wao, wo, gffn, wr)


PREFIX_BLOCK = 256
BF16_ROWS = 16
MIN_NORMAL_BITS = 0x00800000


def _prefix_rows(mask, tri):
    out = []
    carries = [jnp.zeros((1, mask.shape[1]), F32)]
    for blk in range(mask.shape[0] // PREFIX_BLOCK):
        m = mask[blk * PREFIX_BLOCK:(blk + 1) * PREFIX_BLOCK]
        p = _dot(tri, m.astype(BF16)) + carries[-1]
        carries.append(p[PREFIX_BLOCK - 1:PREFIX_BLOCK])
        out.append(p)
    return jnp.concatenate(out, axis=0), jnp.concatenate(carries, axis=0)


def _route_kernel(cap, lg_ref, slot_ref, aff_ref, start_ref):
    lg = lg_ref[0]
    lane = lax.broadcasted_iota(I32, lg.shape, 1)
    valid = lane < N_EXPERTS
    lg = jnp.where(valid, lg, -jnp.inf)
    ex = jnp.exp(lg - jnp.max(lg, axis=-1, keepdims=True))
    aff = ex / jnp.sum(ex, axis=-1, keepdims=True)
    aff_ref[0] = aff
    def step(k, ans):
        cand = ans | jnp.left_shift(jnp.int32(1), 30 - k)
        cnt = jnp.sum(jnp.where(aff >= lax.bitcast_convert_type(cand, F32), 1.0, 0.0), axis=0, keepdims=True)
        return jnp.where(cnt >= cap, cand, ans)

    thr = lax.fori_loop(0, 31, step, jnp.zeros((1, LANES), I32))
    thr = jnp.where(thr < MIN_NORMAL_BITS, 0, thr)
    nxt = jnp.where(thr == 0, MIN_NORMAL_BITS, thr + 1)
    lo = lax.bitcast_convert_type(thr, F32)
    hi = lax.bitcast_convert_type(nxt, F32)
    gt = jnp.where(aff >= hi, 1.0, 0.0)
    eq = jnp.where((aff >= lo) & (aff < hi), 1.0, 0.0)
    r, c = lax.broadcasted_iota(I32, (PREFIX_BLOCK, PREFIX_BLOCK), 0), lax.broadcasted_iota(I32, (PREFIX_BLOCK, PREFIX_BLOCK), 1)
    tri = jnp.where(c <= r, 1.0, 0.0).astype(BF16)
    room = cap - jnp.sum(gt, axis=0, keepdims=True)
    eq_before = _prefix_rows(eq, tri)[0] - eq
    sel = jnp.maximum(gt, jnp.where(eq_before < room, eq, 0.0))
    chosen_upto, block_start = _prefix_rows(sel, tri)
    slot_ref[0] = jnp.where(valid & (sel > 0.0), chosen_upto - 1.0, -1.0).astype(I32)
    start_ref[0] = block_start.astype(I32)


def _route(logits, cap):
    b, s, _ = logits.shape
    n_starts = s // PREFIX_BLOCK + 1
    spec = pl.BlockSpec((1, s, LANES), lambda bi: (bi, 0, 0))
    return pl.pallas_call(
        functools.partial(_route_kernel, cap),
        grid=(b,),
        in_specs=[spec],
        out_specs=[spec, spec, pl.BlockSpec((1, n_starts, LANES), lambda bi: (bi, 0, 0))],
        out_shape=[jax.ShapeDtypeStruct((b, s, LANES), I32), jax.ShapeDtypeStruct((b, s, LANES), F32),
                   jax.ShapeDtypeStruct((b, n_starts, LANES), I32)],
        compiler_params=_params(("parallel",)),
        name="route",
    )(logits)


def _slot_window(start_ref, base, cap):
    first = start_ref[base]
    end = start_ref[base + N_EXPERTS]
    w0 = jnp.minimum((first // BF16_ROWS) * BF16_ROWS, cap - PREFIX_BLOCK)
    return w0, end - w0 <= PREFIX_BLOCK


def _gather_kernel(cap, start_ref, slot_ref, h_ref, xs_ref, acc_ref):
    b = pl.program_id(0)
    e = pl.program_id(1)
    sid = slot_ref[0, 0]
    s = sid.shape[1]
    n_blk = s // PREFIX_BLOCK
    windows = [_slot_window(start_ref, (b * (n_blk + 1) + j) * N_EXPERTS + e, cap) for j in range(n_blk)]
    fits = functools.reduce(jnp.logical_and, [ok for _, ok in windows])

    @pl.when(fits)
    def _():
        acc_ref[...] = jnp.zeros_like(acc_ref)
        rows = lax.broadcasted_iota(I32, (PREFIX_BLOCK, PREFIX_BLOCK), 0)
        for j, (w0, _) in enumerate(windows):
            tok = slice(j * PREFIX_BLOCK, (j + 1) * PREFIX_BLOCK)
            hit = (sid[:, tok] - w0) == rows
            win = pl.ds(pl.multiple_of(w0, BF16_ROWS), PREFIX_BLOCK)
            acc_ref[win, :] += _dot(jnp.where(hit, 1.0, 0.0).astype(BF16), h_ref[0, tok, :])
        xs_ref[0] = acc_ref[...].astype(BF16)

    @pl.when(jnp.logical_not(fits))
    def _():
        hit = sid == lax.broadcasted_iota(I32, (cap, s), 0)
        xs_ref[0] = _dot(jnp.where(hit, 1.0, 0.0).astype(BF16), h_ref[0]).astype(BF16)


def _gather(starts, slot_row, hn, cap):
    b, s, d = hn.shape
    return pl.pallas_call(
        functools.partial(_gather_kernel, cap),
        grid_spec=pltpu.PrefetchScalarGridSpec(
            num_scalar_prefetch=1,
            grid=(b, N_EXPERTS),
            in_specs=[pl.BlockSpec((1, 1, 1, s), lambda bi, e, st: (bi, e, 0, 0)),
                      pl.BlockSpec((1, s, d), lambda bi, e, st: (bi, 0, 0))],
            out_specs=pl.BlockSpec((1, cap, d), lambda bi, e, st: (e, bi, 0)),
            scratch_shapes=[pltpu.VMEM((cap, d), F32)]),
        out_shape=jax.ShapeDtypeStruct((N_EXPERTS, b * cap, d), BF16),
        compiler_params=_params(("arbitrary", "arbitrary")),
        name="gather",
    )(starts, slot_row, hn)


def _ffn_kernel(xs_ref, wg_ref, wu_ref, wd_ref, y_ref, acc_ref):
    f = pl.program_id(1)

    @pl.when(f == 0)
    def _():
        acc_ref[...] = jnp.zeros_like(acc_ref)

    xs = xs_ref[0]
    g = _dot(xs, wg_ref[0].astype(BF16))
    u = _dot(xs, wu_ref[0].astype(BF16))
    act = (g / (1.0 + jnp.exp(-g)) * u).astype(BF16)
    acc_ref[...] += _dot(act, wd_ref[0].astype(BF16))

    @pl.when(f == pl.num_programs(1) - 1)
    def _():
        y_ref[0] = acc_ref[...].astype(BF16)


def _ffn(xs, w_gate, w_up, w_down, tf=256):
    e, m, d = xs.shape
    ff = w_gate.shape[2]
    row = lambda ei, f: (ei, 0, 0)
    return pl.pallas_call(
        _ffn_kernel,
        grid=(e, ff // tf),
        in_specs=[pl.BlockSpec((1, m, d), row),
                  pl.BlockSpec((1, d, tf), lambda ei, f: (ei, 0, f)),
                  pl.BlockSpec((1, d, tf), lambda ei, f: (ei, 0, f)),
                  pl.BlockSpec((1, tf, d), lambda ei, f: (ei, f, 0))],
        out_specs=pl.BlockSpec((1, m, d), row),
        out_shape=jax.ShapeDtypeStruct((e, m, d), BF16),
        scratch_shapes=[pltpu.VMEM((m, d), F32)],
        compiler_params=_params(("parallel", "arbitrary")),
        name="ffn",
    )(xs, w_gate, w_up, w_down)


def _combine_kernel(cap, start_ref, x1_ref, slot_ref, aff_ref, y_ref, g_ref, o_ref):
    b = pl.program_id(0)
    i = pl.program_id(1)
    slot = slot_ref[0]
    aff = aff_ref[0]
    base = (b * (pl.num_programs(1) + 1) + i) * N_EXPERTS
    windows = [_slot_window(start_ref, base + e, cap) for e in range(N_EXPERTS)]
    fits = functools.reduce(jnp.logical_and, [ok for _, ok in windows])

    def finish(acc):
        o_ref[0] = _rms(acc, g_ref[...])

    @pl.when(fits)
    def _():
        cidx = lax.broadcasted_iota(I32, (PREFIX_BLOCK, PREFIX_BLOCK), 1)
        acc = x1_ref[0]
        for e, (w0, _) in enumerate(windows):
            hit = (slot[:, e:e + 1] - w0) == cidx
            rows = y_ref[e, pl.ds(pl.multiple_of(w0, BF16_ROWS), PREFIX_BLOCK), :]
            acc = acc + aff[:, e:e + 1] * _dot(jnp.where(hit, 1.0, 0.0).astype(BF16), rows)
        finish(acc)

    @pl.when(jnp.logical_not(fits))
    def _():
        cidx = lax.broadcasted_iota(I32, (PREFIX_BLOCK, cap), 1)
        acc = x1_ref[0]
        for e in range(N_EXPERTS):
            hit = slot[:, e:e + 1] == cidx
            acc = acc + aff[:, e:e + 1] * _dot(jnp.where(hit, 1.0, 0.0).astype(BF16), y_ref[e])
        finish(acc)


def _combine(starts, x1, slot_t, aff_t, y, g_final, cap):
    b, s, d = x1.shape
    tile = lambda bi, i, st: (bi, i, 0)
    return pl.pallas_call(
        functools.partial(_combine_kernel, cap),
        grid_spec=pltpu.PrefetchScalarGridSpec(
            num_scalar_prefetch=1,
            grid=(b, s // PREFIX_BLOCK),
            in_specs=[pl.BlockSpec((1, PREFIX_BLOCK, d), tile), pl.BlockSpec((1, PREFIX_BLOCK, LANES), tile),
                      pl.BlockSpec((1, PREFIX_BLOCK, LANES), tile),
                      pl.BlockSpec((N_EXPERTS, cap, d), lambda bi, i, st: (0, bi, 0)),
                      pl.BlockSpec((1, d), lambda bi, i, st: (0, 0))],
            out_specs=pl.BlockSpec((1, PREFIX_BLOCK, d), tile)),
        out_shape=jax.ShapeDtypeStruct((b, s, d), F32),
        compiler_params=_params(("arbitrary", "arbitrary")),
        name="combine",
    )(starts, x1, slot_t, aff_t, y, g_final)


def _layer(x, positions, g_mix, w_in, w_pool_mix, pool_scale, w_pool_out, lam_q1, lam_k1, lam_q2,
           lam_k2, g_subln, w_attn_out, w_out, g_ffn, w_router, w_gate, w_up, w_down, lam_init):
    b, s, d = x.shape
    t = b * s
    cap = EC_CAPACITY_FACTOR * s // N_EXPERTS

    posb = jnp.broadcast_to(positions.reshape(t, 1).astype(F32), (t, LANES))
    inv = ROPE_THETA ** (-jnp.arange(ROT_HALF, dtype=F32) * 2.0 / ROT_DIM)
    lane = jnp.arange(LANES)
    inv_tab = jnp.where((lane % HEAD_DIM) < ROT_DIM, inv[lane % ROT_HALF], 0.0).reshape(1, LANES).astype(F32)

    u, q, k1, k2, v, gp, ga = _inproj(x.reshape(t, d), posb, g_mix.reshape(1, d), inv_tab, w_in.astype(BF16))
    sh = lambda a: a.reshape(b, s, d)

    lamv = jnp.zeros((8, LANES), F32).at[0:4, 0:HEAD_DIM].set(jnp.stack([lam_q1, lam_k1, lam_q2, lam_k2]).astype(F32))
    vt = jnp.transpose(sh(v), (0, 2, 1))
    o = _attention(sh(q), sh(k1), sh(k2), vt, lamv, g_subln.reshape(1, HEAD_WIDTH), lam_init)

    wr = jnp.zeros((d, LANES), BF16).at[:, :N_EXPERTS].set(w_router.astype(BF16))
    x1, hn, logits = _mix(sh(u), o, sh(gp), sh(ga), x, w_pool_mix.astype(BF16), pool_scale.reshape(1, d),
                          w_pool_out.astype(BF16), w_attn_out.astype(BF16), w_out.astype(BF16),
                          g_ffn.reshape(1, d), wr)

    slot_t, aff_t, starts = _route(logits, cap)
    starts = starts[:, :, :N_EXPERTS].reshape(-1)
    slot_row = jnp.transpose(slot_t[:, :, :N_EXPERTS], (0, 2, 1)).reshape(b, N_EXPERTS, 1, s)
    xs = _gather(starts, slot_row, hn, cap)
    y = _ffn(xs, w_gate, w_up, w_down)
    return x1, starts, slot_t, aff_t, y, cap


def kernel(x, positions, g_mix, w_in, w_pool_mix, pool_scale, w_pool_out, lam_q1, lam_k1, lam_q2, lam_k2,
           g_subln, w_attn_out, w_out, g_ffn, w_router, w_gate, w_up, w_down, g_final):
    depth = g_mix.shape[0]
    assert depth == 1, "the final norm is fused into the last layer's combine step"
    l = 0
    lam_init = 0.8 - 0.6 * math.exp(-0.3 * l)
    x1, starts, slot_t, aff_t, y, cap = _layer(
        x, positions, g_mix[l], w_in[l], w_pool_mix[l], pool_scale[l], w_pool_out[l], lam_q1[l], lam_k1[l],
        lam_q2[l], lam_k2[l], g_subln[l], w_attn_out[l], w_out[l], g_ffn[l], w_router[l], w_gate[l], w_up[l],
        w_down[l], lam_init)
    return _combine(starts, x1, slot_t, aff_t, y, g_final.reshape(1, D_MODEL), cap)
```

```python
import functools
import math

import jax
import jax.numpy as jnp
from jax import lax
from jax.experimental import pallas as pl
from jax.experimental.pallas import tpu as pltpu

F32 = jnp.float32
BF16 = jnp.bfloat16
I32 = jnp.int32

D_MODEL = 1024
N_HEADS = 8
HEAD_DIM = 64
HEAD_WIDTH = 2 * HEAD_DIM
ROT_DIM = HEAD_DIM // 4
ROT_HALF = ROT_DIM // 2
ROPE_THETA = 500000.0
POOL_WINDOWS = (2, 4, 8, 16)
POOL_GROUP_DIM = D_MODEL // len(POOL_WINDOWS)
POOL_HALO = 16
N_EXPERTS = 16
EC_CAPACITY_FACTOR = 2
RMS_EPS = 1e-6
Q_SCALE = HEAD_DIM ** -0.5 * math.log2(math.e)
LANES = 128

VMEM_LIMIT = 56 * 1024 * 1024


def _params(sem):
    return pltpu.CompilerParams(dimension_semantics=sem, vmem_limit_bytes=VMEM_LIMIT)


def _rms(x, g):
    return (x * lax.rsqrt(jnp.mean(x * x, axis=-1, keepdims=True) + RMS_EPS)) * g


def _dot(a, b):
    return jnp.dot(a, b, preferred_element_type=F32)


def _rotary(z, cos, sin, first_half):
    up = pltpu.roll(z, LANES - ROT_HALF, 1)
    down = pltpu.roll(z, ROT_HALF, 1)
    partner = jnp.where(first_half, -up, down)
    return z * cos + partner * sin


def _inproj_kernel(x_ref, pos_ref, g_ref, inv_ref, w_ref,
                   u_ref, q_ref, k1_ref, k2_ref, v_ref, gp_ref, ga_ref):
    hb = _rms(x_ref[...], g_ref[...]).astype(BF16)
    ang = pos_ref[...] * inv_ref[...]
    cos = jnp.cos(ang)
    sin = jnp.sin(ang)
    lane = lax.broadcasted_iota(I32, ang.shape, 1)
    first_half = (lane % HEAD_DIM) < ROT_HALF
    map1 = lane < HEAD_DIM

    def col(j):
        return _dot(hb, w_ref[:, j * D_MODEL:(j + 1) * D_MODEL])

    u_ref[...] = col(0)
    zq = col(1)
    zk = col(2)
    for h in range(N_HEADS):
        sl = slice(h * HEAD_WIDTH, (h + 1) * HEAD_WIDTH)
        q_ref[:, sl] = (_rotary(zq[:, sl], cos, sin, first_half) * Q_SCALE).astype(BF16)
        kr = _rotary(zk[:, sl], cos, sin, first_half)
        k1_ref[:, sl] = jnp.where(map1, kr, 0.0).astype(BF16)
        k2_ref[:, sl] = jnp.where(map1, 0.0, kr).astype(BF16)
    v_ref[...] = col(3).astype(BF16)
    gp_ref[...] = 1.0 / (1.0 + jnp.exp(-col(4)))
    ga_ref[...] = 1.0 / (1.0 + jnp.exp(-col(5)))


def _inproj(x2d, posb, g_mix, inv_tab, w_in_bf, tm=256):
    t = x2d.shape[0]
    n_cols = w_in_bf.shape[1]
    row = lambda i: (i, 0)
    const = lambda i: (0, 0)
    tile = pl.BlockSpec((tm, D_MODEL), row)
    out_shapes = [jax.ShapeDtypeStruct((t, D_MODEL), dt) for dt in (F32, BF16, BF16, BF16, BF16, F32, F32)]
    return pl.pallas_call(
        _inproj_kernel,
        grid=(t // tm,),
        in_specs=[tile, pl.BlockSpec((tm, LANES), row), pl.BlockSpec((1, D_MODEL), const),
                  pl.BlockSpec((1, LANES), const), pl.BlockSpec((D_MODEL, n_cols), const)],
        out_specs=[tile] * 7,
        out_shape=out_shapes,
        compiler_params=_params(("parallel",)),
        name="inproj",
    )(x2d, posb, g_mix, inv_tab, w_in_bf)


SCORE_SLOTS = 3
SUM_ROWS = 16

def _attn_kernel(lam_init, tk, q_ref, k1_ref, k2_ref, vt_ref, lamv_ref, g_ref, o_ref, st_ref, acc_ref):
    lv = lamv_ref[...]
    lam = (jnp.exp(jnp.sum(lv[0:1] * lv[1:2], axis=-1, keepdims=True))
           - jnp.exp(jnp.sum(lv[2:3] * lv[3:4], axis=-1, keepdims=True)) + lam_init)
    q = q_ref[0]
    tq = q.shape[0]
    n_chunks = k1_ref.shape[1] // tk
    nt = (((1,), (1,)), ((), ()))
    k_refs = (k1_ref, k2_ref)

    def scores(c):
        for mp in range(2):
            st_ref[c % SCORE_SLOTS, mp] = lax.dot_general(k_refs[mp][0, c * tk:(c + 1) * tk, :], q, nt,
                                                          preferred_element_type=F32)

    ones_rows = jnp.ones((SUM_ROWS, tk), BF16)

    def accumulate(c, m):
        out_m = []
        for mp in range(2):
            st = st_ref[c % SCORE_SLOTS, mp]
            m_new = jnp.maximum(m[mp], jnp.max(st, axis=0, keepdims=True))
            alpha = jnp.exp2(m[mp] - m_new)
            et = jnp.exp2(st - m_new).astype(BF16)
            vt_ext = jnp.concatenate([vt_ref[0, :, c * tk:(c + 1) * tk], ones_rows], axis=0)
            pv = _dot(vt_ext, et)
            acc_ref[mp] = pv if c == 0 else acc_ref[mp] * alpha + pv
            out_m.append(m_new)
        return out_m

    m = [jnp.full((1, tq), -jnp.inf, F32)] * 2
    for c in range(min(SCORE_SLOTS - 1, n_chunks)):
        scores(c)
    for c in range(n_chunks):
        if c + SCORE_SLOTS - 1 < n_chunks:
            scores(c + SCORE_SLOTS - 1)
        m = accumulate(c, m)
    a1, a2 = acc_ref[0], acc_ref[1]
    r1 = 1.0 / a1[HEAD_WIDTH:HEAD_WIDTH + 1]
    r2 = lam / a2[HEAD_WIDTH:HEAD_WIDTH + 1]
    ot = a1[:HEAD_WIDTH] * r1 - a2[:HEAD_WIDTH] * r2
    o_ref[0] = (_rms(ot.T, g_ref[...]) * (1.0 - lam_init)).astype(BF16)


def _attention(q, k1, k2, vt, lamv, g_subln, lam_init, tq=1024, tk=256):
    b, s, _ = q.shape
    qspec = pl.BlockSpec((1, tq, HEAD_WIDTH), lambda bi, h, qi: (bi, qi, h))
    kspec = pl.BlockSpec((1, s, HEAD_WIDTH), lambda bi, h, qi: (bi, 0, h))
    vspec = pl.BlockSpec((1, HEAD_WIDTH, s), lambda bi, h, qi: (bi, h, 0))
    const = lambda bi, h, qi: (0, 0)
    return pl.pallas_call(
        functools.partial(_attn_kernel, lam_init, tk),
        grid=(b, N_HEADS, s // tq),
        in_specs=[qspec, kspec, kspec, vspec,
                  pl.BlockSpec((8, LANES), const), pl.BlockSpec((1, HEAD_WIDTH), const)],
        out_specs=qspec,
        out_shape=jax.ShapeDtypeStruct(q.shape, BF16),
        scratch_shapes=[pltpu.VMEM((SCORE_SLOTS, 2, tk, tq), F32), pltpu.VMEM((2, HEAD_WIDTH + SUM_ROWS, tq), F32)],
        compiler_params=_params(("parallel", "parallel", "arbitrary")),
        name="attn",
    )(q, k1, k2, vt, lamv, g_subln)


def _mix_kernel(seq, u_ref, up_ref, un_ref, o_ref, gp_ref, ga_ref, x_ref,
                wmix_ref, ps_ref, wpo_ref, wao_ref, wo_ref, gffn_ref, wr_ref,
                x1_ref, hn_ref, lg_ref, upad_ref):
    tm = u_ref.shape[1]
    i = pl.program_id(1)
    nt = pl.num_programs(1)
    main = u_ref[0]
    upad_ref[0:POOL_HALO, :] = jnp.where(i > 0, up_ref[0], 0.0)
    upad_ref[POOL_HALO:POOL_HALO + tm, :] = main
    upad_ref[POOL_HALO + tm:, :] = jnp.where(i < nt - 1, un_ref[0], 0.0)
    t = i * tm + lax.broadcasted_iota(I32, (tm, 1), 0)
    mixed = []
    for g, w in enumerate(POOL_WINDOWS):
        cols = slice(g * POOL_GROUP_DIM, (g + 1) * POOL_GROUP_DIM)
        win = upad_ref[POOL_HALO - w // 2:POOL_HALO - w // 2 + tm, cols]
        for d in range(-w // 2 + 1, w // 2):
            win = win + upad_ref[POOL_HALO + d:POOL_HALO + d + tm, cols]
        cnt = (jnp.minimum(t + w // 2, seq) - jnp.maximum(t - w // 2, 0)).astype(F32)
        pooled = win / cnt - main[:, cols]
        mixed.append((_dot(pooled.astype(BF16), wmix_ref[g]) * ps_ref[:, cols]).astype(BF16))
    y_pool = _dot(jnp.concatenate(mixed, axis=-1), wpo_ref[...])
    y_attn = _dot(o_ref[0], wao_ref[...])
    merged = gp_ref[0] * y_pool + ga_ref[0] * y_attn
    x1 = x_ref[0] + _dot(merged.astype(BF16), wo_ref[...])
    x1_ref[0] = x1
    hn = _rms(x1, gffn_ref[...]).astype(BF16)
    hn_ref[0] = hn
    lg_ref[0] = _dot(hn, wr_ref[...])


def _mix(u, o, gp, ga, x, wmix, ps, wpo, wao, wo, gffn, wr, tm=512):
    b, s, d = x.shape
    hb = tm // POOL_HALO
    n_halo = s // POOL_HALO
    tile = pl.BlockSpec((1, tm, d), lambda bi, i: (bi, i, 0))
    prev = pl.BlockSpec((1, POOL_HALO, d), lambda bi, i: (bi, jnp.maximum(i * hb - 1, 0), 0))
    nxt = pl.BlockSpec((1, POOL_HALO, d), lambda bi, i: (bi, jnp.minimum((i + 1) * hb, n_halo - 1), 0))
    c2 = lambda bi, i: (0, 0)
    c3 = lambda bi, i: (0, 0, 0)
    return pl.pallas_call(
        functools.partial(_mix_kernel, s),
        grid=(b, s // tm),
        in_specs=[tile, prev, nxt, tile, tile, tile, tile,
                  pl.BlockSpec(wmix.shape, c3), pl.BlockSpec((1, d), c2),
                  pl.BlockSpec((d, d), c2), pl.BlockSpec((d, d), c2), pl.BlockSpec((d, d), c2),
                  pl.BlockSpec((1, d), c2), pl.BlockSpec((d, LANES), c2)],
        out_specs=[tile, tile, pl.BlockSpec((1, tm, LANES), lambda bi, i: (bi, i, 0))],
        out_shape=[jax.ShapeDtypeStruct((b, s, d), F32), jax.ShapeDtypeStruct((b, s, d), BF16),
                   jax.ShapeDtypeStruct((b, s, LANES), F32)],
        scratch_shapes=[pltpu.VMEM((tm + 2 * POOL_HALO, d), F32)],
        compiler_params=_params(("parallel", "parallel")),
        name="mix",
    )(u, u, u, o, gp, ga, x, wmix, ps, wpo, wao, wo, gffn, wr)


PREFIX_BLOCK = 256
BF16_ROWS = 16
MIN_NORMAL_BITS = 0x00800000


def _prefix_rows(mask, tri):
    out = []
    carries = [jnp.zeros((1, mask.shape[1]), F32)]
    for blk in range(mask.shape[0] // PREFIX_BLOCK):
        m = mask[blk * PREFIX_BLOCK:(blk + 1) * PREFIX_BLOCK]
        p = _dot(tri, m.astype(BF16)) + carries[-1]
        carries.append(p[PREFIX_BLOCK - 1:PREFIX_BLOCK])
        out.append(p)
    return jnp.concatenate(out, axis=0), jnp.concatenate(carries, axis=0)


def _route_kernel(cap, lg_ref, slot_ref, aff_ref, start_ref):
    lg = lg_ref[0]
    lane = lax.broadcasted_iota(I32, lg.shape, 1)
    valid = lane < N_EXPERTS
    lg = jnp.where(valid, lg, -jnp.inf)
    ex = jnp.exp(lg - jnp.max(lg, axis=-1, keepdims=True))
    aff = ex / jnp.sum(ex, axis=-1, keepdims=True)
    aff_ref[0] = aff
    aff_e = aff.T[:N_EXPERTS, :]

    def step(k, ans):
        cand = ans | jnp.left_shift(jnp.int32(1), 30 - k)
        cnt = jnp.sum(jnp.where(aff_e >= lax.bitcast_convert_type(cand, F32), 1.0, 0.0), axis=1, keepdims=True)
        return jnp.where(cnt >= cap, cand, ans)

    thr_e = lax.fori_loop(0, 31, step, jnp.zeros((N_EXPERTS, 1), I32))
    thr_sq = jnp.concatenate([jnp.broadcast_to(thr_e, (N_EXPERTS, LANES)),
                              jnp.zeros((LANES - N_EXPERTS, LANES), I32)], axis=0)
    thr = thr_sq.T[0:1, :]
    thr = jnp.where(thr < MIN_NORMAL_BITS, 0, thr)
    nxt = jnp.where(thr == 0, MIN_NORMAL_BITS, thr + 1)
    lo = lax.bitcast_convert_type(thr, F32)
    hi = lax.bitcast_convert_type(nxt, F32)
    gt = jnp.where(aff >= hi, 1.0, 0.0)
    eq = jnp.where((aff >= lo) & (aff < hi), 1.0, 0.0)
    r, c = lax.broadcasted_iota(I32, (PREFIX_BLOCK, PREFIX_BLOCK), 0), lax.broadcasted_iota(I32, (PREFIX_BLOCK, PREFIX_BLOCK), 1)
    tri = jnp.where(c <= r, 1.0, 0.0).astype(BF16)
    room = cap - jnp.sum(gt, axis=0, keepdims=True)
    eq_before = _prefix_rows(eq, tri)[0] - eq
    sel = jnp.maximum(gt, jnp.where(eq_before < room, eq, 0.0))
    chosen_upto, block_start = _prefix_rows(sel, tri)
    slot_ref[0] = jnp.where(valid & (sel > 0.0), chosen_upto - 1.0, -1.0).astype(I32)
    start_ref[0] = block_start.astype(I32)


def _route(logits, cap):
    b, s, _ = logits.shape
    n_starts = s // PREFIX_BLOCK + 1
    spec = pl.BlockSpec((1, s, LANES), lambda bi: (bi, 0, 0))
    return pl.pallas_call(
        functools.partial(_route_kernel, cap),
        grid=(b,),
        in_specs=[spec],
        out_specs=[spec, spec, pl.BlockSpec((1, n_starts, LANES), lambda bi: (bi, 0, 0))],
        out_shape=[jax.ShapeDtypeStruct((b, s, LANES), I32), jax.ShapeDtypeStruct((b, s, LANES), F32),
                   jax.ShapeDtypeStruct((b, n_starts, LANES), I32)],
        compiler_params=_params(("parallel",)),
        name="route",
    )(logits)


def _slot_window(start_ref, base, cap):
    first = start_ref[base]
    end = start_ref[base + N_EXPERTS]
    w0 = jnp.minimum((first // BF16_ROWS) * BF16_ROWS, cap - PREFIX_BLOCK)
    return w0, end - w0 <= PREFIX_BLOCK


def _gather_kernel(cap, start_ref, slot_ref, h_ref, xs_ref, acc_ref):
    b = pl.program_id(0)
    e = pl.program_id(1)
    sid = slot_ref[0, 0]
    s = sid.shape[1]
    n_blk = s // PREFIX_BLOCK
    windows = [_slot_window(start_ref, (b * (n_blk + 1) + j) * N_EXPERTS + e, cap) for j in range(n_blk)]
    fits = functools.reduce(jnp.logical_and, [ok for _, ok in windows])

    @pl.when(fits)
    def _():
        acc_ref[...] = jnp.zeros_like(acc_ref)
        rows = lax.broadcasted_iota(I32, (PREFIX_BLOCK, PREFIX_BLOCK), 0)
        for j, (w0, _) in enumerate(windows):
            tok = slice(j * PREFIX_BLOCK, (j + 1) * PREFIX_BLOCK)
            hit = (sid[:, tok] - w0) == rows
            win = pl.ds(pl.multiple_of(w0, BF16_ROWS), PREFIX_BLOCK)
            acc_ref[win, :] += _dot(jnp.where(hit, 1.0, 0.0).astype(BF16), h_ref[0, tok, :])
        xs_ref[0] = acc_ref[...].astype(BF16)

    @pl.when(jnp.logical_not(fits))
    def _():
        hit = sid == lax.broadcasted_iota(I32, (cap, s), 0)
        xs_ref[0] = _dot(jnp.where(hit, 1.0, 0.0).astype(BF16), h_ref[0]).astype(BF16)


def _gather(starts, slot_row, hn, cap):
    b, s, d = hn.shape
    return pl.pallas_call(
        functools.partial(_gather_kernel, cap),
        grid_spec=pltpu.PrefetchScalarGridSpec(
            num_scalar_prefetch=1,
            grid=(b, N_EXPERTS),
            in_specs=[pl.BlockSpec((1, 1, 1, s), lambda bi, e, st: (bi, e, 0, 0)),
                      pl.BlockSpec((1, s, d), lambda bi, e, st: (bi, 0, 0))],
            out_specs=pl.BlockSpec((1, cap, d), lambda bi, e, st: (e, bi, 0)),
            scratch_shapes=[pltpu.VMEM((cap, d), F32)]),
        out_shape=jax.ShapeDtypeStruct((N_EXPERTS, b * cap, d), BF16),
        compiler_params=_params(("arbitrary", "arbitrary")),
        name="gather",
    )(starts, slot_row, hn)


def _ffn_kernel(xs_ref, wg_ref, wu_ref, wd_ref, y_ref, acc_ref):
    f = pl.program_id(1)

    @pl.when(f == 0)
    def _():
        acc_ref[...] = jnp.zeros_like(acc_ref)

    xs = xs_ref[0]
    g = _dot(xs, wg_ref[0].astype(BF16))
    u = _dot(xs, wu_ref[0].astype(BF16))
    act = (g / (1.0 + jnp.exp(-g)) * u).astype(BF16)
    acc_ref[...] += _dot(act, wd_ref[0].astype(BF16))

    @pl.when(f == pl.num_programs(1) - 1)
    def _():
        y_ref[0] = acc_ref[...].astype(BF16)


def _ffn(xs, w_gate, w_up, w_down, tf=256):
    e, m, d = xs.shape
    ff = w_gate.shape[2]
    row = lambda ei, f: (ei, 0, 0)
    return pl.pallas_call(
        _ffn_kernel,
        grid=(e, ff // tf),
        in_specs=[pl.BlockSpec((1, m, d), row),
                  pl.BlockSpec((1, d, tf), lambda ei, f: (ei, 0, f)),
                  pl.BlockSpec((1, d, tf), lambda ei, f: (ei, 0, f)),
                  pl.BlockSpec((1, tf, d), lambda ei, f: (ei, f, 0))],
        out_specs=pl.BlockSpec((1, m, d), row),
        out_shape=jax.ShapeDtypeStruct((e, m, d), BF16),
        scratch_shapes=[pltpu.VMEM((m, d), F32)],
        compiler_params=_params(("parallel", "arbitrary")),
        name="ffn",
    )(xs, w_gate, w_up, w_down)


def _combine_kernel(cap, start_ref, x1_ref, slot_ref, aff_ref, y_ref, g_ref, o_ref):
    b = pl.program_id(0)
    i = pl.program_id(1)
    slot = slot_ref[0]
    aff = aff_ref[0]
    base = (b * (pl.num_programs(1) + 1) + i) * N_EXPERTS
    windows = [_slot_window(start_ref, base + e, cap) for e in range(N_EXPERTS)]
    fits = functools.reduce(jnp.logical_and, [ok for _, ok in windows])

    def finish(acc):
        o_ref[0] = _rms(acc, g_ref[...])

    @pl.when(fits)
    def _():
        cidx = lax.broadcasted_iota(I32, (PREFIX_BLOCK, PREFIX_BLOCK), 1)
        acc = x1_ref[0]
        for e, (w0, _) in enumerate(windows):
            hit = (slot[:, e:e + 1] - w0) == cidx
            rows = y_ref[e, pl.ds(pl.multiple_of(w0, BF16_ROWS), PREFIX_BLOCK), :]
            acc = acc + aff[:, e:e + 1] * _dot(jnp.where(hit, 1.0, 0.0).astype(BF16), rows)
        finish(acc)

    @pl.when(jnp.logical_not(fits))
    def _():
        cidx = lax.broadcasted_iota(I32, (PREFIX_BLOCK, cap), 1)
        acc = x1_ref[0]
        for e in range(N_EXPERTS):
            hit = slot[:, e:e + 1] == cidx
            acc = acc + aff[:, e:e + 1] * _dot(jnp.where(hit, 1.0, 0.0).astype(BF16), y_ref[e])
        finish(acc)


def _combine(starts, x1, slot_t, aff_t, y, g_final, cap):
    b, s, d = x1.shape
    tile = lambda bi, i, st: (bi, i, 0)
    return pl.pallas_call(
        functools.partial(_combine_kernel, cap),
        grid_spec=pltpu.PrefetchScalarGridSpec(
            num_scalar_prefetch=1,
            grid=(b, s // PREFIX_BLOCK),
            in_specs=[pl.BlockSpec((1, PREFIX_BLOCK, d), tile), pl.BlockSpec((1, PREFIX_BLOCK, LANES), tile),
                      pl.BlockSpec((1, PREFIX_BLOCK, LANES), tile),
                      pl.BlockSpec((N_EXPERTS, cap, d), lambda bi, i, st: (0, bi, 0)),
                      pl.BlockSpec((1, d), lambda bi, i, st: (0, 0))],
            out_specs=pl.BlockSpec((1, PREFIX_BLOCK, d), tile)),
        out_shape=jax.ShapeDtypeStruct((b, s, d), F32),
        compiler_params=_params(("arbitrary", "arbitrary")),
        name="combine",
    )(starts, x1, slot_t, aff_t, y, g_final)


def _layer(x, positions, g_mix, w_in, w_pool_mix, pool_scale, w_pool_out, lam_q1, lam_k1, lam_q2,
           lam_k2, g_subln, w_attn_out, w_out, g_ffn, w_router, w_gate, w_up, w_down, lam_init):
    b, s, d = x.shape
    t = b * s
    cap = EC_CAPACITY_FACTOR * s // N_EXPERTS

    posb = jnp.broadcast_to(positions.reshape(t, 1).astype(F32), (t, LANES))
    inv = ROPE_THETA ** (-jnp.arange(ROT_HALF, dtype=F32) * 2.0 / ROT_DIM)
    lane = jnp.arange(LANES)
    inv_tab = jnp.where((lane % HEAD_DIM) < ROT_DIM, inv[lane % ROT_HALF], 0.0).reshape(1, LANES).astype(F32)

    u, q, k1, k2, v, gp, ga = _inproj(x.reshape(t, d), posb, g_mix.reshape(1, d), inv_tab, w_in.astype(BF16))
    sh = lambda a: a.reshape(b, s, d)

    lamv = jnp.zeros((8, LANES), F32).at[0:4, 0:HEAD_DIM].set(jnp.stack([lam_q1, lam_k1, lam_q2, lam_k2]).astype(F32))
    vt = jnp.transpose(sh(v), (0, 2, 1))
    o = _attention(sh(q), sh(k1), sh(k2), vt, lamv, g_subln.reshape(1, HEAD_WIDTH), lam_init)

    wr = jnp.zeros((d, LANES), BF16).at[:, :N_EXPERTS].set(w_router.astype(BF16))
    x1, hn, logits = _mix(sh(u), o, sh(gp), sh(ga), x, w_pool_mix.astype(BF16), pool_scale.reshape(1, d),
                          w_pool_out.astype(BF16), w_attn_out.astype(BF16), w_out.astype(BF16),
                          g_ffn.reshape(1, d), wr)

    slot_t, aff_t, starts = _route(logits, cap)
    starts = starts[:, :, :N_EXPERTS].reshape(-1)
    slot_row = jnp.transpose(slot_t[:, :, :N_EXPERTS], (0, 2, 1)).reshape(b, N_EXPERTS, 1, s)
    xs = _gather(starts, slot_row, hn, cap)
    y = _ffn(xs, w_gate, w_up, w_down)
    return x1, starts, slot_t, aff_t, y, cap


def kernel(x, positions, g_mix, w_in, w_pool_mix, pool_scale, w_pool_out, lam_q1, lam_k1, lam_q2, lam_k2,
           g_subln, w_attn_out, w_out, g_ffn, w_router, w_gate, w_up, w_down, g_final):
    depth = g_mix.shape[0]
    assert depth == 1, "the final norm is fused into the last layer's combine step"
    l = 0
    lam_init = 0.8 - 0.6 * math.exp(-0.3 * l)
    x1, starts, slot_t, aff_t, y, cap = _layer(
        x, positions, g_mix[l], w_in[l], w_pool_mix[l], pool_scale[l], w_pool_out[l], lam_q1[l], lam_k1[l],
        lam_q2[l], lam_k2[l], g_subln[l], w_attn_out[l], w_out[l], g_ffn[l], w_router[l], w_gate[l], w_up[l],
        w_down[l], lam_init)
    return _combine(starts, x1, slot_t, aff_t, y, g_final.reshape(1, D_MODEL), cap)
```

```python
import functools
import math

import jax
import jax.numpy as jnp
from jax import lax
from jax.experimental import pallas as pl
from jax.experimental.pallas import tpu as pltpu

F32 = jnp.float32
BF16 = jnp.bfloat16
I32 = jnp.int32

D_MODEL = 1024
N_HEADS = 8
HEAD_DIM = 64
HEAD_WIDTH = 2 * HEAD_DIM
ROT_DIM = HEAD_DIM // 4
ROT_HALF = ROT_DIM // 2
ROPE_THETA = 500000.0
POOL_WINDOWS = (2, 4, 8, 16)
POOL_GROUP_DIM = D_MODEL // len(POOL_WINDOWS)
POOL_HALO = 16
N_EXPERTS = 16
EC_CAPACITY_FACTOR = 2
RMS_EPS = 1e-6
Q_SCALE = HEAD_DIM ** -0.5 * math.log2(math.e)
LANES = 128

VMEM_LIMIT = 56 * 1024 * 1024


def _params(sem):
    return pltpu.CompilerParams(dimension_semantics=sem, vmem_limit_bytes=VMEM_LIMIT)


def _rms(x, g):
    return (x * lax.rsqrt(jnp.mean(x * x, axis=-1, keepdims=True) + RMS_EPS)) * g


def _dot(a, b):
    return jnp.dot(a, b, preferred_element_type=F32)


def _rotary(z, cos, sin, first_half):
    up = pltpu.roll(z, LANES - ROT_HALF, 1)
    down = pltpu.roll(z, ROT_HALF, 1)
    partner = jnp.where(first_half, -up, down)
    return z * cos + partner * sin


def _inproj_kernel(x_ref, pos_ref, g_ref, inv_ref, w_ref,
                   u_ref, q_ref, k1_ref, k2_ref, v_ref, gp_ref, ga_ref):
    hb = _rms(x_ref[...], g_ref[...]).astype(BF16)
    ang = pos_ref[...] * inv_ref[...]
    cos = jnp.cos(ang)
    sin = jnp.sin(ang)
    lane = lax.broadcasted_iota(I32, ang.shape, 1)
    first_half = (lane % HEAD_DIM) < ROT_HALF
    map1 = lane < HEAD_DIM

    def col(j):
        return _dot(hb, w_ref[:, j * D_MODEL:(j + 1) * D_MODEL])

    u_ref[...] = col(0)
    zq = col(1)
    zk = col(2)
    for h in range(N_HEADS):
        sl = slice(h * HEAD_WIDTH, (h + 1) * HEAD_WIDTH)
        q_ref[:, sl] = (_rotary(zq[:, sl], cos, sin, first_half) * Q_SCALE).astype(BF16)
        kr = _rotary(zk[:, sl], cos, sin, first_half)
        k1_ref[:, sl] = jnp.where(map1, kr, 0.0).astype(BF16)
        k2_ref[:, sl] = jnp.where(map1, 0.0, kr).astype(BF16)
    v_ref[...] = col(3).astype(BF16)
    gp_ref[...] = 1.0 / (1.0 + jnp.exp(-col(4)))
    ga_ref[...] = 1.0 / (1.0 + jnp.exp(-col(5)))


def _inproj(x2d, posb, g_mix, inv_tab, w_in_bf, tm=256):
    t = x2d.shape[0]
    n_cols = w_in_bf.shape[1]
    row = lambda i: (i, 0)
    const = lambda i: (0, 0)
    tile = pl.BlockSpec((tm, D_MODEL), row)
    out_shapes = [jax.ShapeDtypeStruct((t, D_MODEL), dt) for dt in (F32, BF16, BF16, BF16, BF16, F32, F32)]
    return pl.pallas_call(
        _inproj_kernel,
        grid=(t // tm,),
        in_specs=[tile, pl.BlockSpec((tm, LANES), row), pl.BlockSpec((1, D_MODEL), const),
                  pl.BlockSpec((1, LANES), const), pl.BlockSpec((D_MODEL, n_cols), const)],
        out_specs=[tile] * 7,
        out_shape=out_shapes,
        compiler_params=_params(("parallel",)),
        name="inproj",
    )(x2d, posb, g_mix, inv_tab, w_in_bf)


SCORE_SLOTS = 4
SUM_ROWS = 16

def _attn_kernel(lam_init, tk, q_ref, k1_ref, k2_ref, vt_ref, lamv_ref, g_ref, o_ref, st_ref, acc_ref):
    lv = lamv_ref[...]
    lam = (jnp.exp(jnp.sum(lv[0:1] * lv[1:2], axis=-1, keepdims=True))
           - jnp.exp(jnp.sum(lv[2:3] * lv[3:4], axis=-1, keepdims=True)) + lam_init)
    q = q_ref[0]
    tq = q.shape[0]
    n_chunks = k1_ref.shape[1] // tk
    nt = (((1,), (1,)), ((), ()))
    k_refs = (k1_ref, k2_ref)

    def scores(c):
        for mp in range(2):
            st_ref[c % SCORE_SLOTS, mp] = lax.dot_general(k_refs[mp][0, c * tk:(c + 1) * tk, :], q, nt,
                                                          preferred_element_type=F32)

    ones_rows = jnp.ones((SUM_ROWS, tk), BF16)

    def accumulate(c, m):
        out_m = []
        for mp in range(2):
            st = st_ref[c % SCORE_SLOTS, mp]
            m_new = jnp.maximum(m[mp], jnp.max(st, axis=0, keepdims=True))
            alpha = jnp.exp2(m[mp] - m_new)
            et = jnp.exp2(st - m_new).astype(BF16)
            vt_ext = jnp.concatenate([vt_ref[0, :, c * tk:(c + 1) * tk], ones_rows], axis=0)
            pv = _dot(vt_ext, et)
            acc_ref[mp] = pv if c == 0 else acc_ref[mp] * alpha + pv
            out_m.append(m_new)
        return out_m

    m = [jnp.full((1, tq), -jnp.inf, F32)] * 2
    for c in range(min(SCORE_SLOTS - 1, n_chunks)):
        scores(c)
    for c in range(n_chunks):
        if c + SCORE_SLOTS - 1 < n_chunks:
            scores(c + SCORE_SLOTS - 1)
        m = accumulate(c, m)
    a1, a2 = acc_ref[0], acc_ref[1]
    r1 = 1.0 / a1[HEAD_WIDTH:HEAD_WIDTH + 1]
    r2 = lam / a2[HEAD_WIDTH:HEAD_WIDTH + 1]
    ot = a1[:HEAD_WIDTH] * r1 - a2[:HEAD_WIDTH] * r2
    o_ref[0] = (_rms(ot.T, g_ref[...]) * (1.0 - lam_init)).astype(BF16)


def _attention(q, k1, k2, vt, lamv, g_subln, lam_init, tq=1024, tk=256):
    b, s, _ = q.shape
    qspec = pl.BlockSpec((1, tq, HEAD_WIDTH), lambda bi, h, qi: (bi, qi, h))
    kspec = pl.BlockSpec((1, s, HEAD_WIDTH), lambda bi, h, qi: (bi, 0, h))
    vspec = pl.BlockSpec((1, HEAD_WIDTH, s), lambda bi, h, qi: (bi, h, 0))
    const = lambda bi, h, qi: (0, 0)
    return pl.pallas_call(
        functools.partial(_attn_kernel, lam_init, tk),
        grid=(b, N_HEADS, s // tq),
        in_specs=[qspec, kspec, kspec, vspec,
                  pl.BlockSpec((8, LANES), const), pl.BlockSpec((1, HEAD_WIDTH), const)],
        out_specs=qspec,
        out_shape=jax.ShapeDtypeStruct(q.shape, BF16),
        scratch_shapes=[pltpu.VMEM((SCORE_SLOTS, 2, tk, tq), F32), pltpu.VMEM((2, HEAD_WIDTH + SUM_ROWS, tq), F32)],
        compiler_params=_params(("parallel", "parallel", "arbitrary")),
        name="attn",
    )(q, k1, k2, vt, lamv, g_subln)


def _mix_kernel(seq, u_ref, up_ref, un_ref, o_ref, gp_ref, ga_ref, x_ref,
                wmix_ref, ps_ref, wpo_ref, wao_ref, wo_ref, gffn_ref, wr_ref,
                x1_ref, hn_ref, lg_ref, upad_ref):
    tm = u_ref.shape[1]
    i = pl.program_id(1)
    nt = pl.num_programs(1)
    main = u_ref[0]
    upad_ref[0:POOL_HALO, :] = jnp.where(i > 0, up_ref[0], 0.0)
    upad_ref[POOL_HALO:POOL_HALO + tm, :] = main
    upad_ref[POOL_HALO + tm:, :] = jnp.where(i < nt - 1, un_ref[0], 0.0)
    t = i * tm + lax.broadcasted_iota(I32, (tm, 1), 0)
    mixed = []
    for g, w in enumerate(POOL_WINDOWS):
        cols = slice(g * POOL_GROUP_DIM, (g + 1) * POOL_GROUP_DIM)
        win = upad_ref[POOL_HALO - w // 2:POOL_HALO - w // 2 + tm, cols]
        for d in range(-w // 2 + 1, w // 2):
            win = win + upad_ref[POOL_HALO + d:POOL_HALO + d + tm, cols]
        cnt = (jnp.minimum(t + w // 2, seq) - jnp.maximum(t - w // 2, 0)).astype(F32)
        pooled = win / cnt - main[:, cols]
        mixed.append((_dot(pooled.astype(BF16), wmix_ref[g]) * ps_ref[:, cols]).astype(BF16))
    y_pool = _dot(jnp.concatenate(mixed, axis=-1), wpo_ref[...])
    y_attn = _dot(o_ref[0], wao_ref[...])
    merged = gp_ref[0] * y_pool + ga_ref[0] * y_attn
    x1 = x_ref[0] + _dot(merged.astype(BF16), wo_ref[...])
    x1_ref[0] = x1
    hn = _rms(x1, gffn_ref[...]).astype(BF16)
    hn_ref[0] = hn
    lg_ref[0] = _dot(hn, wr_ref[...])


def _mix(u, o, gp, ga, x, wmix, ps, wpo, wao, wo, gffn, wr, tm=512):
    b, s, d = x.shape
    hb = tm // POOL_HALO
    n_halo = s // POOL_HALO
    tile = pl.BlockSpec((1, tm, d), lambda bi, i: (bi, i, 0))
    prev = pl.BlockSpec((1, POOL_HALO, d), lambda bi, i: (bi, jnp.maximum(i * hb - 1, 0), 0))
    nxt = pl.BlockSpec((1, POOL_HALO, d), lambda bi, i: (bi, jnp.minimum((i + 1) * hb, n_halo - 1), 0))
    c2 = lambda bi, i: (0, 0)
    c3 = lambda bi, i: (0, 0, 0)
    return pl.pallas_call(
        functools.partial(_mix_kernel, s),
        grid=(b, s // tm),
        in_specs=[tile, prev, nxt, tile, tile, tile, tile,
                  pl.BlockSpec(wmix.shape, c3), pl.BlockSpec((1, d), c2),
                  pl.BlockSpec((d, d), c2), pl.BlockSpec((d, d), c2), pl.BlockSpec((d, d), c2),
                  pl.BlockSpec((1, d), c2), pl.BlockSpec((d, LANES), c2)],
        out_specs=[tile, tile, pl.BlockSpec((1, tm, LANES), lambda bi, i: (bi, i, 0))],
        out_shape=[jax.ShapeDtypeStruct((b, s, d), F32), jax.ShapeDtypeStruct((b, s, d), BF16),
                   jax.ShapeDtypeStruct((b, s, LANES), F32)],
        scratch_shapes=[pltpu.VMEM((tm + 2 * POOL_HALO, d), F32)],
        compiler_params=_params(("parallel", "parallel")),
        name="mix",
    )(u, u, u, o, gp, ga, x, wmix, ps, wpo, wao, wo, gffn, wr)


PREFIX_BLOCK = 256
BF16_ROWS = 16
MIN_NORMAL_BITS = 0x00800000


def _prefix_rows(mask, tri):
    out = []
    carries = [jnp.zeros((1, mask.shape[1]), F32)]
    for blk in range(mask.shape[0] // PREFIX_BLOCK):
        m = mask[blk * PREFIX_BLOCK:(blk + 1) * PREFIX_BLOCK]
        p = _dot(tri, m.astype(BF16)) + carries[-1]
        carries.append(p[PREFIX_BLOCK - 1:PREFIX_BLOCK])
        out.append(p)
    return jnp.concatenate(out, axis=0), jnp.concatenate(carries, axis=0)


def _route_kernel(cap, lg_ref, slot_ref, aff_ref, start_ref):
    lg = lg_ref[0]
    lane = lax.broadcasted_iota(I32, lg.shape, 1)
    valid = lane < N_EXPERTS
    lg = jnp.where(valid, lg, -jnp.inf)
    ex = jnp.exp(lg - jnp.max(lg, axis=-1, keepdims=True))
    aff = ex / jnp.sum(ex, axis=-1, keepdims=True)
    aff_ref[0] = aff
    aff_e = aff.T[:N_EXPERTS, :]

    def step(k, ans):
        cand = ans | jnp.left_shift(jnp.int32(1), 30 - k)
        cnt = jnp.sum(jnp.where(aff_e >= lax.bitcast_convert_type(cand, F32), 1.0, 0.0), axis=1, keepdims=True)
        return jnp.where(cnt >= cap, cand, ans)

    thr_e = lax.fori_loop(0, 31, step, jnp.zeros((N_EXPERTS, 1), I32))
    thr_sq = jnp.concatenate([jnp.broadcast_to(thr_e, (N_EXPERTS, LANES)),
                              jnp.zeros((LANES - N_EXPERTS, LANES), I32)], axis=0)
    thr = thr_sq.T[0:1, :]
    thr = jnp.where(thr < MIN_NORMAL_BITS, 0, thr)
    nxt = jnp.where(thr == 0, MIN_NORMAL_BITS, thr + 1)
    lo = lax.bitcast_convert_type(thr, F32)
    hi = lax.bitcast_convert_type(nxt, F32)
    gt = jnp.where(aff >= hi, 1.0, 0.0)
    eq = jnp.where((aff >= lo) & (aff < hi), 1.0, 0.0)
    r, c = lax.broadcasted_iota(I32, (PREFIX_BLOCK, PREFIX_BLOCK), 0), lax.broadcasted_iota(I32, (PREFIX_BLOCK, PREFIX_BLOCK), 1)
    tri = jnp.where(c <= r, 1.0, 0.0).astype(BF16)
    room = cap - jnp.sum(gt, axis=0, keepdims=True)
    eq_before = _prefix_rows(eq, tri)[0] - eq
    sel = jnp.maximum(gt, jnp.where(eq_before < room, eq, 0.0))
    chosen_upto, block_start = _prefix_rows(sel, tri)
    slot_ref[0] = jnp.where(valid & (sel > 0.0), chosen_upto - 1.0, -1.0).astype(I32)
    start_ref[0] = block_start.astype(I32)


def _route(logits, cap):
    b, s, _ = logits.shape
    n_starts = s // PREFIX_BLOCK + 1
    spec = pl.BlockSpec((1, s, LANES), lambda bi: (bi, 0, 0))
    return pl.pallas_call(
        functools.partial(_route_kernel, cap),
        grid=(b,),
        in_specs=[spec],
        out_specs=[spec, spec, pl.BlockSpec((1, n_starts, LANES), lambda bi: (bi, 0, 0))],
        out_shape=[jax.ShapeDtypeStruct((b, s, LANES), I32), jax.ShapeDtypeStruct((b, s, LANES), F32),
                   jax.ShapeDtypeStruct((b, n_starts, LANES), I32)],
        compiler_params=_params(("parallel",)),
        name="route",
    )(logits)


def _slot_window(start_ref, base, cap):
    first = start_ref[base]
    end = start_ref[base + N_EXPERTS]
    w0 = jnp.minimum((first // BF16_ROWS) * BF16_ROWS, cap - PREFIX_BLOCK)
    return w0, end - w0 <= PREFIX_BLOCK


def _gather_kernel(cap, start_ref, slot_ref, h_ref, xs_ref, acc_ref):
    b = pl.program_id(0)
    e = pl.program_id(1)
    sid = slot_ref[0, 0]
    s = sid.shape[1]
    n_blk = s // PREFIX_BLOCK
    windows = [_slot_window(start_ref, (b * (n_blk + 1) + j) * N_EXPERTS + e, cap) for j in range(n_blk)]
    fits = functools.reduce(jnp.logical_and, [ok for _, ok in windows])

    @pl.when(fits)
    def _():
        acc_ref[...] = jnp.zeros_like(acc_ref)
        rows = lax.broadcasted_iota(I32, (PREFIX_BLOCK, PREFIX_BLOCK), 0)
        for j, (w0, _) in enumerate(windows):
            tok = slice(j * PREFIX_BLOCK, (j + 1) * PREFIX_BLOCK)
            hit = (sid[:, tok] - w0) == rows
            win = pl.ds(pl.multiple_of(w0, BF16_ROWS), PREFIX_BLOCK)
            acc_ref[win, :] += _dot(jnp.where(hit, 1.0, 0.0).astype(BF16), h_ref[0, tok, :])
        xs_ref[0] = acc_ref[...].astype(BF16)

    @pl.when(jnp.logical_not(fits))
    def _():
        hit = sid == lax.broadcasted_iota(I32, (cap, s), 0)
        xs_ref[0] = _dot(jnp.where(hit, 1.0, 0.0).astype(BF16), h_ref[0]).astype(BF16)


def _gather(starts, slot_row, hn, cap):
    b, s, d = hn.shape
    return pl.pallas_call(
        functools.partial(_gather_kernel, cap),
        grid_spec=pltpu.PrefetchScalarGridSpec(
            num_scalar_prefetch=1,
            grid=(b, N_EXPERTS),
            in_specs=[pl.BlockSpec((1, 1, 1, s), lambda bi, e, st: (bi, e, 0, 0)),
                      pl.BlockSpec((1, s, d), lambda bi, e, st: (bi, 0, 0))],
            out_specs=pl.BlockSpec((1, cap, d), lambda bi, e, st: (e, bi, 0)),
            scratch_shapes=[pltpu.VMEM((cap, d), F32)]),
        out_shape=jax.ShapeDtypeStruct((N_EXPERTS, b * cap, d), BF16),
        compiler_params=_params(("arbitrary", "arbitrary")),
        name="gather",
    )(starts, slot_row, hn)


def _ffn_kernel(xs_ref, wg_ref, wu_ref, wd_ref, y_ref, acc_ref):
    f = pl.program_id(1)

    @pl.when(f == 0)
    def _():
        acc_ref[...] = jnp.zeros_like(acc_ref)

    xs = xs_ref[0]
    g = _dot(xs, wg_ref[0].astype(BF16))
    u = _dot(xs, wu_ref[0].astype(BF16))
    act = (g / (1.0 + jnp.exp(-g)) * u).astype(BF16)
    acc_ref[...] += _dot(act, wd_ref[0].astype(BF16))

    @pl.when(f == pl.num_programs(1) - 1)
    def _():
        y_ref[0] = acc_ref[...].astype(BF16)


def _ffn(xs, w_gate, w_up, w_down, tf=256):
    e, m, d = xs.shape
    ff = w_gate.shape[2]
    row = lambda ei, f: (ei, 0, 0)
    return pl.pallas_call(
        _ffn_kernel,
        grid=(e, ff // tf),
        in_specs=[pl.BlockSpec((1, m, d), row),
                  pl.BlockSpec((1, d, tf), lambda ei, f: (ei, 0, f)),
                  pl.BlockSpec((1, d, tf), lambda ei, f: (ei, 0, f)),
                  pl.BlockSpec((1, tf, d), lambda ei, f: (ei, f, 0))],
        out_specs=pl.BlockSpec((1, m, d), row),
        out_shape=jax.ShapeDtypeStruct((e, m, d), BF16),
        scratch_shapes=[pltpu.VMEM((m, d), F32)],
        compiler_params=_params(("parallel", "arbitrary")),
        name="ffn",
    )(xs, w_gate, w_up, w_down)


def _combine_kernel(cap, start_ref, x1_ref, slot_ref, aff_ref, y_ref, g_ref, o_ref):
    b = pl.program_id(0)
    i = pl.program_id(1)
    slot = slot_ref[0]
    aff = aff_ref[0]
    base = (b * (pl.num_programs(1) + 1) + i) * N_EXPERTS
    windows = [_slot_window(start_ref, base + e, cap) for e in range(N_EXPERTS)]
    fits = functools.reduce(jnp.logical_and, [ok for _, ok in windows])

    def finish(acc):
        o_ref[0] = _rms(acc, g_ref[...])

    @pl.when(fits)
    def _():
        cidx = lax.broadcasted_iota(I32, (PREFIX_BLOCK, PREFIX_BLOCK), 1)
        acc = x1_ref[0]
        for e, (w0, _) in enumerate(windows):
            hit = (slot[:, e:e + 1] - w0) == cidx
            rows = y_ref[e, pl.ds(pl.multiple_of(w0, BF16_ROWS), PREFIX_BLOCK), :]
            acc = acc + aff[:, e:e + 1] * _dot(jnp.where(hit, 1.0, 0.0).astype(BF16), rows)
        finish(acc)

    @pl.when(jnp.logical_not(fits))
    def _():
        cidx = lax.broadcasted_iota(I32, (PREFIX_BLOCK, cap), 1)
        acc = x1_ref[0]
        for e in range(N_EXPERTS):
            hit = slot[:, e:e + 1] == cidx
            acc = acc + aff[:, e:e + 1] * _dot(jnp.where(hit, 1.0, 0.0).astype(BF16), y_ref[e])
        finish(acc)


def _combine(starts, x1, slot_t, aff_t, y, g_final, cap):
    b, s, d = x1.shape
    tile = lambda bi, i, st: (bi, i, 0)
    return pl.pallas_call(
        functools.partial(_combine_kernel, cap),
        grid_spec=pltpu.PrefetchScalarGridSpec(
            num_scalar_prefetch=1,
            grid=(b, s // PREFIX_BLOCK),
            in_specs=[pl.BlockSpec((1, PREFIX_BLOCK, d), tile), pl.BlockSpec((1, PREFIX_BLOCK, LANES), tile),
                      pl.BlockSpec((1, PREFIX_BLOCK, LANES), tile),
                      pl.BlockSpec((N_EXPERTS, cap, d), lambda bi, i, st: (0, bi, 0)),
                      pl.BlockSpec((1, d), lambda bi, i, st: (0, 0))],
            out_specs=pl.BlockSpec((1, PREFIX_BLOCK, d), tile)),
        out_shape=jax.ShapeDtypeStruct((b, s, d), F32),
        compiler_params=_params(("arbitrary", "arbitrary")),
        name="combine",
    )(starts, x1, slot_t, aff_t, y, g_final)


def _layer(x, positions, g_mix, w_in, w_pool_mix, pool_scale, w_pool_out, lam_q1, lam_k1, lam_q2,
           lam_k2, g_subln, w_attn_out, w_out, g_ffn, w_router, w_gate, w_up, w_down, lam_init):
    b, s, d = x.shape
    t = b * s
    cap = EC_CAPACITY_FACTOR * s // N_EXPERTS

    posb = jnp.broadcast_to(positions.reshape(t, 1).astype(F32), (t, LANES))
    inv = ROPE_THETA ** (-jnp.arange(ROT_HALF, dtype=F32) * 2.0 / ROT_DIM)
    lane = jnp.arange(LANES)
    inv_tab = jnp.where((lane % HEAD_DIM) < ROT_DIM, inv[lane % ROT_HALF], 0.0).reshape(1, LANES).astype(F32)

    u, q, k1, k2, v, gp, ga = _inproj(x.reshape(t, d), posb, g_mix.reshape(1, d), inv_tab, w_in.astype(BF16))
    sh = lambda a: a.reshape(b, s, d)

    lamv = jnp.zeros((8, LANES), F32).at[0:4, 0:HEAD_DIM].set(jnp.stack([lam_q1, lam_k1, lam_q2, lam_k2]).astype(F32))
    vt = jnp.transpose(sh(v), (0, 2, 1))
    o = _attention(sh(q), sh(k1), sh(k2), vt, lamv, g_subln.reshape(1, HEAD_WIDTH), lam_init)

    wr = jnp.zeros((d, LANES), BF16).at[:, :N_EXPERTS].set(w_router.astype(BF16))
    x1, hn, logits = _mix(sh(u), o, sh(gp), sh(ga), x, w_pool_mix.astype(BF16), pool_scale.reshape(1, d),
                          w_pool_out.astype(BF16), w_attn_out.astype(BF16), w_out.astype(BF16),
                          g_ffn.reshape(1, d), wr)

    slot_t, aff_t, starts = _route(logits, cap)
    starts = starts[:, :, :N_EXPERTS].reshape(-1)
    slot_row = jnp.transpose(slot_t[:, :, :N_EXPERTS], (0, 2, 1)).reshape(b, N_EXPERTS, 1, s)
    xs = _gather(starts, slot_row, hn, cap)
    y = _ffn(xs, w_gate, w_up, w_down)
    return x1, starts, slot_t, aff_t, y, cap


def kernel(x, positions, g_mix, w_in, w_pool_mix, pool_scale, w_pool_out, lam_q1, lam_k1, lam_q2, lam_k2,
           g_subln, w_attn_out, w_out, g_ffn, w_router, w_gate, w_up, w_down, g_final):
    depth = g_mix.shape[0]
    assert depth == 1, "the final norm is fused into the last layer's combine step"
    l = 0
    lam_init = 0.8 - 0.6 * math.exp(-0.3 * l)
    x1, starts, slot_t, aff_t, y, cap = _layer(
        x, positions, g_mix[l], w_in[l], w_pool_mix[l], pool_scale[l], w_pool_out[l], lam_q1[l], lam_k1[l],
        lam_q2[l], lam_k2[l], g_subln[l], w_attn_out[l], w_out[l], g_ffn[l], w_router[l], w_gate[l], w_up[l],
        w_down[l], lam_init)
    return _combine(starts, x1, slot_t, aff_t, y, g_final.reshape(1, D_MODEL), cap)
```

```python
import functools
import math

import jax
import jax.numpy as jnp
from jax import lax
from jax.experimental import pallas as pl
from jax.experimental.pallas import tpu as pltpu

F32 = jnp.float32
BF16 = jnp.bfloat16
I32 = jnp.int32

D_MODEL = 1024
N_HEADS = 8
HEAD_DIM = 64
HEAD_WIDTH = 2 * HEAD_DIM
ROT_DIM = HEAD_DIM // 4
ROT_HALF = ROT_DIM // 2
ROPE_THETA = 500000.0
POOL_WINDOWS = (2, 4, 8, 16)
POOL_GROUP_DIM = D_MODEL // len(POOL_WINDOWS)
POOL_HALO = 16
N_EXPERTS = 16
EC_CAPACITY_FACTOR = 2
RMS_EPS = 1e-6
Q_SCALE = HEAD_DIM ** -0.5 * math.log2(math.e)
LANES = 128

VMEM_LIMIT = 56 * 1024 * 1024


def _params(sem):
    return pltpu.CompilerParams(dimension_semantics=sem, vmem_limit_bytes=VMEM_LIMIT)


def _rms(x, g):
    return (x * lax.rsqrt(jnp.mean(x * x, axis=-1, keepdims=True) + RMS_EPS)) * g


def _dot(a, b):
    return jnp.dot(a, b, preferred_element_type=F32)


def _rotary(z, cos, sin, first_half):
    up = pltpu.roll(z, LANES - ROT_HALF, 1)
    down = pltpu.roll(z, ROT_HALF, 1)
    partner = jnp.where(first_half, -up, down)
    return z * cos + partner * sin


def _inproj_kernel(x_ref, pos_ref, g_ref, inv_ref, w_ref,
                   u_ref, q_ref, k1_ref, k2_ref, v_ref, gp_ref, ga_ref):
    hb = _rms(x_ref[...], g_ref[...]).astype(BF16)
    ang = inv_ref[...] * pos_ref[...]
    cos_f, sin_f = jnp.cos(ang), jnp.sin(ang)
    tm = ang.shape[1]
    rest = HEAD_DIM - ROT_DIM
    cos = jnp.concatenate([cos_f, cos_f, jnp.ones((rest, tm), F32)] * 2, axis=0).T
    sin = jnp.concatenate([sin_f, sin_f, jnp.zeros((rest, tm), F32)] * 2, axis=0).T
    lane = lax.broadcasted_iota(I32, cos.shape, 1)
    first_half = (lane % HEAD_DIM) < ROT_HALF
    map1 = lane < HEAD_DIM

    def col(j):
        return _dot(hb, w_ref[:, j * D_MODEL:(j + 1) * D_MODEL])

    u_ref[...] = col(0)
    zq = col(1)
    zk = col(2)
    for h in range(N_HEADS):
        sl = slice(h * HEAD_WIDTH, (h + 1) * HEAD_WIDTH)
        q_ref[:, sl] = (_rotary(zq[:, sl], cos, sin, first_half) * Q_SCALE).astype(BF16)
        kr = _rotary(zk[:, sl], cos, sin, first_half)
        k1_ref[:, sl] = jnp.where(map1, kr, 0.0).astype(BF16)
        k2_ref[:, sl] = jnp.where(map1, 0.0, kr).astype(BF16)
    v_ref[...] = col(3).astype(BF16)
    gp_ref[...] = 1.0 / (1.0 + jnp.exp(-col(4)))
    ga_ref[...] = 1.0 / (1.0 + jnp.exp(-col(5)))


def _inproj(x2d, pos_row, g_mix, inv_freq, w_in_bf, tm=256):
    t = x2d.shape[0]
    n_cols = w_in_bf.shape[1]
    row = lambda i: (i, 0)
    const = lambda i: (0, 0)
    tile = pl.BlockSpec((tm, D_MODEL), row)
    out_shapes = [jax.ShapeDtypeStruct((t, D_MODEL), dt) for dt in (F32, BF16, BF16, BF16, BF16, F32, F32)]
    return pl.pallas_call(
        _inproj_kernel,
        grid=(t // tm,),
        in_specs=[tile, pl.BlockSpec((1, tm), lambda i: (0, i)), pl.BlockSpec((1, D_MODEL), const),
                  pl.BlockSpec((ROT_HALF, 1), const), pl.BlockSpec((D_MODEL, n_cols), const)],
        out_specs=[tile] * 7,
        out_shape=out_shapes,
        compiler_params=_params(("parallel",)),
        name="inproj",
    )(x2d, pos_row, g_mix, inv_freq, w_in_bf)


SCORE_SLOTS = 3
SUM_ROWS = 16

def _attn_kernel(lam_init, tk, qt_ref, k1_ref, k2_ref, vt_ref, lamv_ref, g_ref, o_ref, st_ref, acc_ref):
    lv = lamv_ref[...]
    lam = (jnp.exp(jnp.sum(lv[0:1] * lv[1:2], axis=-1, keepdims=True))
           - jnp.exp(jnp.sum(lv[2:3] * lv[3:4], axis=-1, keepdims=True)) + lam_init)
    qt = qt_ref[0]
    tq = qt.shape[1]
    n_chunks = k1_ref.shape[1] // tk
    k_refs = (k1_ref, k2_ref)

    def scores(c):
        for mp in range(2):
            st_ref[c % SCORE_SLOTS, mp] = _dot(k_refs[mp][0, c * tk:(c + 1) * tk, :], qt)

    ones_rows = jnp.ones((SUM_ROWS, tk), BF16)

    def accumulate(c, m):
        out_m = []
        for mp in range(2):
            st = st_ref[c % SCORE_SLOTS, mp]
            m_new = jnp.maximum(m[mp], jnp.max(st, axis=0, keepdims=True))
            alpha = jnp.exp2(m[mp] - m_new)
            et = jnp.exp2(st - m_new).astype(BF16)
            vt_ext = jnp.concatenate([vt_ref[0, :, c * tk:(c + 1) * tk], ones_rows], axis=0)
            pv = _dot(vt_ext, et)
            acc_ref[mp] = pv if c == 0 else acc_ref[mp] * alpha + pv
            out_m.append(m_new)
        return out_m

    m = [jnp.full((1, tq), -jnp.inf, F32)] * 2
    for c in range(min(SCORE_SLOTS - 1, n_chunks)):
        scores(c)
    for c in range(n_chunks):
        if c + SCORE_SLOTS - 1 < n_chunks:
            scores(c + SCORE_SLOTS - 1)
        m = accumulate(c, m)
    a1, a2 = acc_ref[0], acc_ref[1]
    r1 = 1.0 / a1[HEAD_WIDTH:HEAD_WIDTH + 1]
    r2 = lam / a2[HEAD_WIDTH:HEAD_WIDTH + 1]
    ot = a1[:HEAD_WIDTH] * r1 - a2[:HEAD_WIDTH] * r2
    o_ref[0] = (_rms(ot.T, g_ref[...]) * (1.0 - lam_init)).astype(BF16)


def _attention(qt, k1, k2, vt, lamv, g_subln, lam_init, tq=1024, tk=256):
    b, s, _ = k1.shape
    qspec = pl.BlockSpec((1, tq, HEAD_WIDTH), lambda bi, h, qi: (bi, qi, h))
    qtspec = pl.BlockSpec((1, HEAD_WIDTH, tq), lambda bi, h, qi: (bi, h, qi))
    kspec = pl.BlockSpec((1, s, HEAD_WIDTH), lambda bi, h, qi: (bi, 0, h))
    vspec = pl.BlockSpec((1, HEAD_WIDTH, s), lambda bi, h, qi: (bi, h, 0))
    const = lambda bi, h, qi: (0, 0)
    return pl.pallas_call(
        functools.partial(_attn_kernel, lam_init, tk),
        grid=(b, N_HEADS, s // tq),
        in_specs=[qtspec, kspec, kspec, vspec,
                  pl.BlockSpec((8, LANES), const), pl.BlockSpec((1, HEAD_WIDTH), const)],
        out_specs=qspec,
        out_shape=jax.ShapeDtypeStruct(k1.shape, BF16),
        scratch_shapes=[pltpu.VMEM((SCORE_SLOTS, 2, tk, tq), F32), pltpu.VMEM((2, HEAD_WIDTH + SUM_ROWS, tq), F32)],
        compiler_params=_params(("parallel", "parallel", "arbitrary")),
        name="attn",
    )(qt, k1, k2, vt, lamv, g_subln)


def _mix_kernel(seq, u_ref, up_ref, un_ref, o_ref, gp_ref, ga_ref, x_ref,
                wmix_ref, ps_ref, wpo_ref, wao_ref, wo_ref, gffn_ref, wr_ref,
                x1_ref, hn_ref, lg_ref, upad_ref):
    tm = u_ref.shape[1]
    i = pl.program_id(1)
    nt = pl.num_programs(1)
    main = u_ref[0]
    upad_ref[0:POOL_HALO, :] = jnp.where(i > 0, up_ref[0], 0.0)
    upad_ref[POOL_HALO:POOL_HALO + tm, :] = main
    upad_ref[POOL_HALO + tm:, :] = jnp.where(i < nt - 1, un_ref[0], 0.0)
    t = i * tm + lax.broadcasted_iota(I32, (tm, 1), 0)
    mixed = []
    for g, w in enumerate(POOL_WINDOWS):
        cols = slice(g * POOL_GROUP_DIM, (g + 1) * POOL_GROUP_DIM)
        win = upad_ref[POOL_HALO - w // 2:POOL_HALO - w // 2 + tm, cols]
        for d in range(-w // 2 + 1, w // 2):
            win = win + upad_ref[POOL_HALO + d:POOL_HALO + d + tm, cols]
        cnt = (jnp.minimum(t + w // 2, seq) - jnp.maximum(t - w // 2, 0)).astype(F32)
        pooled = win / cnt - main[:, cols]
        mixed.append((_dot(pooled.astype(BF16), wmix_ref[g]) * ps_ref[:, cols]).astype(BF16))
    y_pool = _dot(jnp.concatenate(mixed, axis=-1), wpo_ref[...])
    y_attn = _dot(o_ref[0], wao_ref[...])
    merged = gp_ref[0] * y_pool + ga_ref[0] * y_attn
    x1 = x_ref[0] + _dot(merged.astype(BF16), wo_ref[...])
    x1_ref[0] = x1
    hn = _rms(x1, gffn_ref[...]).astype(BF16)
    hn_ref[0] = hn
    lg_ref[0] = _dot(hn, wr_ref[...])


def _mix(u, o, gp, ga, x, wmix, ps, wpo, wao, wo, gffn, wr, tm=512):
    b, s, d = x.shape
    hb = tm // POOL_HALO
    n_halo = s // POOL_HALO
    tile = pl.BlockSpec((1, tm, d), lambda bi, i: (bi, i, 0))
    prev = pl.BlockSpec((1, POOL_HALO, d), lambda bi, i: (bi, jnp.maximum(i * hb - 1, 0), 0))
    nxt = pl.BlockSpec((1, POOL_HALO, d), lambda bi, i: (bi, jnp.minimum((i + 1) * hb, n_halo - 1), 0))
    c2 = lambda bi, i: (0, 0)
    c3 = lambda bi, i: (0, 0, 0)
    return pl.pallas_call(
        functools.partial(_mix_kernel, s),
        grid=(b, s // tm),
        in_specs=[tile, prev, nxt, tile, tile, tile, tile,
                  pl.BlockSpec(wmix.shape, c3), pl.BlockSpec((1, d), c2),
                  pl.BlockSpec((d, d), c2), pl.BlockSpec((d, d), c2), pl.BlockSpec((d, d), c2),
                  pl.BlockSpec((1, d), c2), pl.BlockSpec((d, LANES), c2)],
        out_specs=[tile, tile, pl.BlockSpec((1, tm, LANES), lambda bi, i: (bi, i, 0))],
        out_shape=[jax.ShapeDtypeStruct((b, s, d), F32), jax.ShapeDtypeStruct((b, s, d), BF16),
                   jax.ShapeDtypeStruct((b, s, LANES), F32)],
        scratch_shapes=[pltpu.VMEM((tm + 2 * POOL_HALO, d), F32)],
        compiler_params=_params(("parallel", "parallel")),
        name="mix",
    )(u, u, u, o, gp, ga, x, wmix, ps, wpo, wao, wo, gffn, wr)


PREFIX_BLOCK = 256
BF16_ROWS = 16
MIN_NORMAL_BITS = 0x00800000


def _prefix_rows(mask, tri):
    out = []
    carries = [jnp.zeros((1, mask.shape[1]), F32)]
    for blk in range(mask.shape[0] // PREFIX_BLOCK):
        m = mask[blk * PREFIX_BLOCK:(blk + 1) * PREFIX_BLOCK]
        p = _dot(tri, m.astype(BF16)) + carries[-1]
        carries.append(p[PREFIX_BLOCK - 1:PREFIX_BLOCK])
        out.append(p)
    return jnp.concatenate(out, axis=0), jnp.concatenate(carries, axis=0)


def _route_kernel(cap, lg_ref, slot_ref, aff_ref, start_ref):
    lg = lg_ref[0]
    lane = lax.broadcasted_iota(I32, lg.shape, 1)
    valid = lane < N_EXPERTS
    lg = jnp.where(valid, lg, -jnp.inf)
    ex = jnp.exp(lg - jnp.max(lg, axis=-1, keepdims=True))
    aff = ex / jnp.sum(ex, axis=-1, keepdims=True)
    aff_ref[0] = aff
    aff_e = aff.T[:N_EXPERTS, :]

    def step(k, ans):
        cand = ans | jnp.left_shift(jnp.int32(1), 30 - k)
        cnt = jnp.sum(jnp.where(aff_e >= lax.bitcast_convert_type(cand, F32), 1.0, 0.0), axis=1, keepdims=True)
        return jnp.where(cnt >= cap, cand, ans)

    thr_e = lax.fori_loop(0, 31, step, jnp.zeros((N_EXPERTS, 1), I32))
    thr_sq = jnp.concatenate([jnp.broadcast_to(thr_e, (N_EXPERTS, LANES)),
                              jnp.zeros((LANES - N_EXPERTS, LANES), I32)], axis=0)
    thr = thr_sq.T[0:1, :]
    thr = jnp.where(thr < MIN_NORMAL_BITS, 0, thr)
    nxt = jnp.where(thr == 0, MIN_NORMAL_BITS, thr + 1)
    lo = lax.bitcast_convert_type(thr, F32)
    hi = lax.bitcast_convert_type(nxt, F32)
    gt = jnp.where(aff >= hi, 1.0, 0.0)
    eq = jnp.where((aff >= lo) & (aff < hi), 1.0, 0.0)
    r, c = lax.broadcasted_iota(I32, (PREFIX_BLOCK, PREFIX_BLOCK), 0), lax.broadcasted_iota(I32, (PREFIX_BLOCK, PREFIX_BLOCK), 1)
    tri = jnp.where(c <= r, 1.0, 0.0).astype(BF16)
    room = cap - jnp.sum(gt, axis=0, keepdims=True)
    eq_before = _prefix_rows(eq, tri)[0] - eq
    sel = jnp.maximum(gt, jnp.where(eq_before < room, eq, 0.0))
    chosen_upto, block_start = _prefix_rows(sel, tri)
    slot_ref[0] = jnp.where(valid & (sel > 0.0), chosen_upto - 1.0, -1.0).astype(I32)
    start_ref[0] = block_start.astype(I32)


def _route(logits, cap):
    b, s, _ = logits.shape
    n_starts = s // PREFIX_BLOCK + 1
    spec = pl.BlockSpec((1, s, LANES), lambda bi: (bi, 0, 0))
    return pl.pallas_call(
        functools.partial(_route_kernel, cap),
        grid=(b,),
        in_specs=[spec],
        out_specs=[spec, spec, pl.BlockSpec((1, n_starts, LANES), lambda bi: (bi, 0, 0))],
        out_shape=[jax.ShapeDtypeStruct((b, s, LANES), I32), jax.ShapeDtypeStruct((b, s, LANES), F32),
                   jax.ShapeDtypeStruct((b, n_starts, LANES), I32)],
        compiler_params=_params(("parallel",)),
        name="route",
    )(logits)


def _slot_window(start_ref, base, cap):
    first = start_ref[base]
    end = start_ref[base + N_EXPERTS]
    w0 = jnp.minimum((first // BF16_ROWS) * BF16_ROWS, cap - PREFIX_BLOCK)
    return w0, end - w0 <= PREFIX_BLOCK


def _gather_kernel(cap, start_ref, slot_ref, h_ref, xs_ref, acc_ref):
    b = pl.program_id(0)
    e = pl.program_id(1)
    sid = slot_ref[0, 0]
    s = sid.shape[1]
    n_blk = s // PREFIX_BLOCK
    windows = [_slot_window(start_ref, (b * (n_blk + 1) + j) * N_EXPERTS + e, cap) for j in range(n_blk)]
    fits = functools.reduce(jnp.logical_and, [ok for _, ok in windows])

    @pl.when(fits)
    def _():
        acc_ref[...] = jnp.zeros_like(acc_ref)
        rows = lax.broadcasted_iota(I32, (PREFIX_BLOCK, PREFIX_BLOCK), 0)
        for j, (w0, _) in enumerate(windows):
            tok = slice(j * PREFIX_BLOCK, (j + 1) * PREFIX_BLOCK)
            hit = (sid[:, tok] - w0) == rows
            win = pl.ds(pl.multiple_of(w0, BF16_ROWS), PREFIX_BLOCK)
            acc_ref[win, :] += _dot(jnp.where(hit, 1.0, 0.0).astype(BF16), h_ref[0, tok, :])
        xs_ref[0] = acc_ref[...].astype(BF16)

    @pl.when(jnp.logical_not(fits))
    def _():
        hit = sid == lax.broadcasted_iota(I32, (cap, s), 0)
        xs_ref[0] = _dot(jnp.where(hit, 1.0, 0.0).astype(BF16), h_ref[0]).astype(BF16)


def _gather(starts, slot_row, hn, cap):
    b, s, d = hn.shape
    return pl.pallas_call(
        functools.partial(_gather_kernel, cap),
        grid_spec=pltpu.PrefetchScalarGridSpec(
            num_scalar_prefetch=1,
            grid=(b, N_EXPERTS),
            in_specs=[pl.BlockSpec((1, 1, 1, s), lambda bi, e, st: (bi, e, 0, 0)),
                      pl.BlockSpec((1, s, d), lambda bi, e, st: (bi, 0, 0))],
            out_specs=pl.BlockSpec((1, cap, d), lambda bi, e, st: (e, bi, 0)),
            scratch_shapes=[pltpu.VMEM((cap, d), F32)]),
        out_shape=jax.ShapeDtypeStruct((N_EXPERTS, b * cap, d), BF16),
        compiler_params=_params(("arbitrary", "arbitrary")),
        name="gather",
    )(starts, slot_row, hn)


def _ffn_kernel(xs_ref, wg_ref, wu_ref, wd_ref, y_ref, acc_ref):
    f = pl.program_id(1)

    @pl.when(f == 0)
    def _():
        acc_ref[...] = jnp.zeros_like(acc_ref)

    xs = xs_ref[0]
    g = _dot(xs, wg_ref[0].astype(BF16))
    u = _dot(xs, wu_ref[0].astype(BF16))
    act = (g / (1.0 + jnp.exp(-g)) * u).astype(BF16)
    acc_ref[...] += _dot(act, wd_ref[0].astype(BF16))

    @pl.when(f == pl.num_programs(1) - 1)
    def _():
        y_ref[0] = acc_ref[...].astype(BF16)


def _ffn(xs, w_gate, w_up, w_down, tf=256):
    e, m, d = xs.shape
    ff = w_gate.shape[2]
    row = lambda ei, f: (ei, 0, 0)
    return pl.pallas_call(
        _ffn_kernel,
        grid=(e, ff // tf),
        in_specs=[pl.BlockSpec((1, m, d), row),
                  pl.BlockSpec((1, d, tf), lambda ei, f: (ei, 0, f)),
                  pl.BlockSpec((1, d, tf), lambda ei, f: (ei, 0, f)),
                  pl.BlockSpec((1, tf, d), lambda ei, f: (ei, f, 0))],
        out_specs=pl.BlockSpec((1, m, d), row),
        out_shape=jax.ShapeDtypeStruct((e, m, d), BF16),
        scratch_shapes=[pltpu.VMEM((m, d), F32)],
        compiler_params=_params(("parallel", "arbitrary")),
        name="ffn",
    )(xs, w_gate, w_up, w_down)


def _combine_kernel(cap, start_ref, x1_ref, slot_ref, aff_ref, y_ref, g_ref, o_ref):
    b = pl.program_id(0)
    i = pl.program_id(1)
    slot = slot_ref[0]
    aff = aff_ref[0]
    base = (b * (pl.num_programs(1) + 1) + i) * N_EXPERTS
    windows = [_slot_window(start_ref, base + e, cap) for e in range(N_EXPERTS)]
    fits = functools.reduce(jnp.logical_and, [ok for _, ok in windows])

    def finish(acc):
        o_ref[0] = _rms(acc, g_ref[...])

    @pl.when(fits)
    def _():
        cidx = lax.broadcasted_iota(I32, (PREFIX_BLOCK, PREFIX_BLOCK), 1)
        acc = x1_ref[0]
        for e, (w0, _) in enumerate(windows):
            hit = (slot[:, e:e + 1] - w0) == cidx
            rows = y_ref[e, pl.ds(pl.multiple_of(w0, BF16_ROWS), PREFIX_BLOCK), :]
            acc = acc + aff[:, e:e + 1] * _dot(jnp.where(hit, 1.0, 0.0).astype(BF16), rows)
        finish(acc)

    @pl.when(jnp.logical_not(fits))
    def _():
        cidx = lax.broadcasted_iota(I32, (PREFIX_BLOCK, cap), 1)
        acc = x1_ref[0]
        for e in range(N_EXPERTS):
            hit = slot[:, e:e + 1] == cidx
            acc = acc + aff[:, e:e + 1] * _dot(jnp.where(hit, 1.0, 0.0).astype(BF16), y_ref[e])
        finish(acc)


def _combine(starts, x1, slot_t, aff_t, y, g_final, cap):
    b, s, d = x1.shape
    tile = lambda bi, i, st: (bi, i, 0)
    return pl.pallas_call(
        functools.partial(_combine_kernel, cap),
        grid_spec=pltpu.PrefetchScalarGridSpec(
            num_scalar_prefetch=1,
            grid=(b, s // PREFIX_BLOCK),
            in_specs=[pl.BlockSpec((1, PREFIX_BLOCK, d), tile), pl.BlockSpec((1, PREFIX_BLOCK, LANES), tile),
                      pl.BlockSpec((1, PREFIX_BLOCK, LANES), tile),
                      pl.BlockSpec((N_EXPERTS, cap, d), lambda bi, i, st: (0, bi, 0)),
                      pl.BlockSpec((1, d), lambda bi, i, st: (0, 0))],
            out_specs=pl.BlockSpec((1, PREFIX_BLOCK, d), tile)),
        out_shape=jax.ShapeDtypeStruct((b, s, d), F32),
        compiler_params=_params(("arbitrary", "arbitrary")),
        name="combine",
    )(starts, x1, slot_t, aff_t, y, g_final)


def _layer(x, positions, g_mix, w_in, w_pool_mix, pool_scale, w_pool_out, lam_q1, lam_k1, lam_q2,
           lam_k2, g_subln, w_attn_out, w_out, g_ffn, w_router, w_gate, w_up, w_down, lam_init):
    b, s, d = x.shape
    t = b * s
    cap = EC_CAPACITY_FACTOR * s // N_EXPERTS

    pos_row = positions.reshape(1, t).astype(F32)
    inv = (ROPE_THETA ** (-jnp.arange(ROT_HALF, dtype=F32) * 2.0 / ROT_DIM)).reshape(ROT_HALF, 1)

    u, q, k1, k2, v, gp, ga = _inproj(x.reshape(t, d), pos_row, g_mix.reshape(1, d), inv, w_in.astype(BF16))
    sh = lambda a: a.reshape(b, s, d)

    lamv = jnp.zeros((8, LANES), F32).at[0:4, 0:HEAD_DIM].set(jnp.stack([lam_q1, lam_k1, lam_q2, lam_k2]).astype(F32))
    tr = lambda a: jnp.transpose(sh(a), (0, 2, 1))
    o = _attention(tr(q), sh(k1), sh(k2), tr(v), lamv, g_subln.reshape(1, HEAD_WIDTH), lam_init)

    wr = jnp.zeros((d, LANES), BF16).at[:, :N_EXPERTS].set(w_router.astype(BF16))
    x1, hn, logits = _mix(sh(u), o, sh(gp), sh(ga), x, w_pool_mix.astype(BF16), pool_scale.reshape(1, d),
                          w_pool_out.astype(BF16), w_attn_out.astype(BF16), w_out.astype(BF16),
                          g_ffn.reshape(1, d), wr)

    slot_t, aff_t, starts = _route(logits, cap)
    starts = starts[:, :, :N_EXPERTS].reshape(-1)
    slot_row = jnp.transpose(slot_t[:, :, :N_EXPERTS], (0, 2, 1)).reshape(b, N_EXPERTS, 1, s)
    xs = _gather(starts, slot_row, hn, cap)
    y = _ffn(xs, w_gate, w_up, w_down)
    return x1, starts, slot_t, aff_t, y, cap


def kernel(x, positions, g_mix, w_in, w_pool_mix, pool_scale, w_pool_out, lam_q1, lam_k1, lam_q2, lam_k2,
           g_subln, w_attn_out, w_out, g_ffn, w_router, w_gate, w_up, w_down, g_final):
    depth = g_mix.shape[0]
    assert depth == 1, "the final norm is fused into the last layer's combine step"
    l = 0
    lam_init = 0.8 - 0.6 * math.exp(-0.3 * l)
    x1, starts, slot_t, aff_t, y, cap = _layer(
        x, positions, g_mix[l], w_in[l], w_pool_mix[l], pool_scale[l], w_pool_out[l], lam_q1[l], lam_k1[l],
        lam_q2[l], lam_k2[l], g_subln[l], w_attn_out[l], w_out[l], g_ffn[l], w_router[l], w_gate[l], w_up[l],
        w_down[l], lam_init)
    return _combine(starts, x1, slot_t, aff_t, y, g_final.reshape(1, D_MODEL), cap)
```

```python
import functools
import math

import jax
import jax.numpy as jnp
from jax import lax
from jax.experimental import pallas as pl
from jax.experimental.pallas import tpu as pltpu

F32 = jnp.float32
BF16 = jnp.bfloat16
I32 = jnp.int32

D_MODEL = 1024
N_HEADS = 8
HEAD_DIM = 64
HEAD_WIDTH = 2 * HEAD_DIM
ROT_DIM = HEAD_DIM // 4
ROT_HALF = ROT_DIM // 2
ROPE_THETA = 500000.0
POOL_WINDOWS = (2, 4, 8, 16)
POOL_GROUP_DIM = D_MODEL // len(POOL_WINDOWS)
POOL_HALO = 16
N_EXPERTS = 16
EC_CAPACITY_FACTOR = 2
RMS_EPS = 1e-6
Q_SCALE = HEAD_DIM ** -0.5 * math.log2(math.e)
LANES = 128

VMEM_LIMIT = 56 * 1024 * 1024


def _params(sem):
    return pltpu.CompilerParams(dimension_semantics=sem, vmem_limit_bytes=VMEM_LIMIT)


def _rms(x, g):
    return (x * lax.rsqrt(jnp.mean(x * x, axis=-1, keepdims=True) + RMS_EPS)) * g


def _dot(a, b):
    return jnp.dot(a, b, preferred_element_type=F32)


def _rotary(z, cos, sin, first_half):
    up = pltpu.roll(z, LANES - ROT_HALF, 1)
    down = pltpu.roll(z, ROT_HALF, 1)
    partner = jnp.where(first_half, -up, down)
    return z * cos + partner * sin


def _inproj_kernel(x_ref, pos_ref, g_ref, inv_ref, w_ref,
                   u_ref, q_ref, k1_ref, k2_ref, v_ref, gp_ref, ga_ref):
    hb = _rms(x_ref[...], g_ref[...]).astype(BF16)
    ang = inv_ref[...] * pos_ref[...]
    cos_f, sin_f = jnp.cos(ang), jnp.sin(ang)
    tm = ang.shape[1]
    rest = HEAD_DIM - ROT_DIM
    cos = jnp.concatenate([cos_f, cos_f, jnp.ones((rest, tm), F32)] * 2, axis=0).T
    sin = jnp.concatenate([sin_f, sin_f, jnp.zeros((rest, tm), F32)] * 2, axis=0).T
    lane = lax.broadcasted_iota(I32, cos.shape, 1)
    first_half = (lane % HEAD_DIM) < ROT_HALF
    map1 = lane < HEAD_DIM

    def col(j):
        return _dot(hb, w_ref[:, j * D_MODEL:(j + 1) * D_MODEL])

    u_ref[...] = col(0)
    zq = col(1)
    zk = col(2)
    for h in range(N_HEADS):
        sl = slice(h * HEAD_WIDTH, (h + 1) * HEAD_WIDTH)
        q_ref[:, sl] = (_rotary(zq[:, sl], cos, sin, first_half) * Q_SCALE).astype(BF16)
        kr = _rotary(zk[:, sl], cos, sin, first_half)
        k1_ref[:, sl] = jnp.where(map1, kr, 0.0).astype(BF16)
        k2_ref[:, sl] = jnp.where(map1, 0.0, kr).astype(BF16)
    v_ref[...] = col(3).astype(BF16)
    gp_ref[...] = 1.0 / (1.0 + jnp.exp(-col(4)))
    ga_ref[...] = 1.0 / (1.0 + jnp.exp(-col(5)))


def _inproj(x2d, pos_row, g_mix, inv_freq, w_in_bf, tm=256):
    t = x2d.shape[0]
    n_cols = w_in_bf.shape[1]
    row = lambda i: (i, 0)
    const = lambda i: (0, 0)
    tile = pl.BlockSpec((tm, D_MODEL), row)
    out_shapes = [jax.ShapeDtypeStruct((t, D_MODEL), dt) for dt in (F32, BF16, BF16, BF16, BF16, F32, F32)]
    return pl.pallas_call(
        _inproj_kernel,
        grid=(t // tm,),
        in_specs=[tile, pl.BlockSpec((1, tm), lambda i: (0, i)), pl.BlockSpec((1, D_MODEL), const),
                  pl.BlockSpec((ROT_HALF, 1), const), pl.BlockSpec((D_MODEL, n_cols), const)],
        out_specs=[tile] * 7,
        out_shape=out_shapes,
        compiler_params=_params(("parallel",)),
        name="inproj",
    )(x2d, pos_row, g_mix, inv_freq, w_in_bf)


SCORE_SLOTS = 3
SUM_ROWS = 16

def _attn_kernel(lam_init, tk, q_ref, k1_ref, k2_ref, vt_ref, lamv_ref, g_ref, o_ref, st_ref, acc_ref):
    lv = lamv_ref[...]
    lam = (jnp.exp(jnp.sum(lv[0:1] * lv[1:2], axis=-1, keepdims=True))
           - jnp.exp(jnp.sum(lv[2:3] * lv[3:4], axis=-1, keepdims=True)) + lam_init)
    q = q_ref[0]
    tq = q.shape[0]
    n_chunks = k1_ref.shape[1] // tk
    nt = (((1,), (1,)), ((), ()))
    k_refs = (k1_ref, k2_ref)

    def scores(c):
        for mp in range(2):
            st_ref[c % SCORE_SLOTS, mp] = lax.dot_general(k_refs[mp][0, c * tk:(c + 1) * tk, :], q, nt,
                                                          preferred_element_type=F32)

    ones_rows = jnp.ones((SUM_ROWS, tk), BF16)

    def accumulate(c, m):
        out_m = []
        for mp in range(2):
            st = st_ref[c % SCORE_SLOTS, mp]
            m_new = jnp.maximum(m[mp], jnp.max(st, axis=0, keepdims=True))
            alpha = jnp.exp2(m[mp] - m_new)
            et = jnp.exp2(st - m_new).astype(BF16)
            vt_ext = jnp.concatenate([vt_ref[0, :, c * tk:(c + 1) * tk], ones_rows], axis=0)
            pv = _dot(vt_ext, et)
            acc_ref[mp] = pv if c == 0 else acc_ref[mp] * alpha + pv
            out_m.append(m_new)
        return out_m

    m = [jnp.full((1, tq), -jnp.inf, F32)] * 2
    for c in range(min(SCORE_SLOTS - 1, n_chunks)):
        scores(c)
    for c in range(n_chunks):
        if c + SCORE_SLOTS - 1 < n_chunks:
            scores(c + SCORE_SLOTS - 1)
        m = accumulate(c, m)
    a1, a2 = acc_ref[0], acc_ref[1]
    r1 = 1.0 / a1[HEAD_WIDTH:HEAD_WIDTH + 1]
    r2 = lam / a2[HEAD_WIDTH:HEAD_WIDTH + 1]
    ot = a1[:HEAD_WIDTH] * r1 - a2[:HEAD_WIDTH] * r2
    o_ref[0] = (_rms(ot.T, g_ref[...]) * (1.0 - lam_init)).astype(BF16)


def _attention(q, k1, k2, vt, lamv, g_subln, lam_init, tq=1024, tk=256):
    b, s, _ = q.shape
    qspec = pl.BlockSpec((1, tq, HEAD_WIDTH), lambda bi, h, qi: (bi, qi, h))
    kspec = pl.BlockSpec((1, s, HEAD_WIDTH), lambda bi, h, qi: (bi, 0, h))
    vspec = pl.BlockSpec((1, HEAD_WIDTH, s), lambda bi, h, qi: (bi, h, 0))
    const = lambda bi, h, qi: (0, 0)
    return pl.pallas_call(
        functools.partial(_attn_kernel, lam_init, tk),
        grid=(b, N_HEADS, s // tq),
        in_specs=[qspec, kspec, kspec, vspec,
                  pl.BlockSpec((8, LANES), const), pl.BlockSpec((1, HEAD_WIDTH), const)],
        out_specs=qspec,
        out_shape=jax.ShapeDtypeStruct(k1.shape, BF16),
        scratch_shapes=[pltpu.VMEM((SCORE_SLOTS, 2, tk, tq), F32), pltpu.VMEM((2, HEAD_WIDTH + SUM_ROWS, tq), F32)],
        compiler_params=_params(("parallel", "parallel", "arbitrary")),
        name="attn",
    )(q, k1, k2, vt, lamv, g_subln)


def _mix_kernel(seq, u_ref, up_ref, un_ref, o_ref, gp_ref, ga_ref, x_ref,
                wmix_ref, ps_ref, wpo_ref, wao_ref, wo_ref, gffn_ref, wr_ref,
                x1_ref, hn_ref, lg_ref, upad_ref):
    tm = u_ref.shape[1]
    i = pl.program_id(1)
    nt = pl.num_programs(1)
    main = u_ref[0]
    upad_ref[0:POOL_HALO, :] = jnp.where(i > 0, up_ref[0], 0.0)
    upad_ref[POOL_HALO:POOL_HALO + tm, :] = main
    upad_ref[POOL_HALO + tm:, :] = jnp.where(i < nt - 1, un_ref[0], 0.0)
    t = i * tm + lax.broadcasted_iota(I32, (tm, 1), 0)
    mixed = []
    for g, w in enumerate(POOL_WINDOWS):
        cols = slice(g * POOL_GROUP_DIM, (g + 1) * POOL_GROUP_DIM)
        win = upad_ref[POOL_HALO - w // 2:POOL_HALO - w // 2 + tm, cols]
        for d in range(-w // 2 + 1, w // 2):
            win = win + upad_ref[POOL_HALO + d:POOL_HALO + d + tm, cols]
        cnt = (jnp.minimum(t + w // 2, seq) - jnp.maximum(t - w // 2, 0)).astype(F32)
        pooled = win / cnt - main[:, cols]
        mixed.append((_dot(pooled.astype(BF16), wmix_ref[g]) * ps_ref[:, cols]).astype(BF16))
    y_pool = _dot(jnp.concatenate(mixed, axis=-1), wpo_ref[...])
    y_attn = _dot(o_ref[0], wao_ref[...])
    merged = gp_ref[0] * y_pool + ga_ref[0] * y_attn
    x1 = x_ref[0] + _dot(merged.astype(BF16), wo_ref[...])
    x1_ref[0] = x1
    hn = _rms(x1, gffn_ref[...]).astype(BF16)
    hn_ref[0] = hn
    lg_ref[0] = _dot(hn, wr_ref[...])


def _mix(u, o, gp, ga, x, wmix, ps, wpo, wao, wo, gffn, wr, tm=512):
    b, s, d = x.shape
    hb = tm // POOL_HALO
    n_halo = s // POOL_HALO
    tile = pl.BlockSpec((1, tm, d), lambda bi, i: (bi, i, 0))
    prev = pl.BlockSpec((1, POOL_HALO, d), lambda bi, i: (bi, jnp.maximum(i * hb - 1, 0), 0))
    nxt = pl.BlockSpec((1, POOL_HALO, d), lambda bi, i: (bi, jnp.minimum((i + 1) * hb, n_halo - 1), 0))
    c2 = lambda bi, i: (0, 0)
    c3 = lambda bi, i: (0, 0, 0)
    return pl.pallas_call(
        functools.partial(_mix_kernel, s),
        grid=(b, s // tm),
        in_specs=[tile, prev, nxt, tile, tile, tile, tile,
                  pl.BlockSpec(wmix.shape, c3), pl.BlockSpec((1, d), c2),
                  pl.BlockSpec((d, d), c2), pl.BlockSpec((d, d), c2), pl.BlockSpec((d, d), c2),
                  pl.BlockSpec((1, d), c2), pl.BlockSpec((d, LANES), c2)],
        out_specs=[tile, tile, pl.BlockSpec((1, tm, LANES), lambda bi, i: (bi, i, 0))],
        out_shape=[jax.ShapeDtypeStruct((b, s, d), F32), jax.ShapeDtypeStruct((b, s, d), BF16),
                   jax.ShapeDtypeStruct((b, s, LANES), F32)],
        scratch_shapes=[pltpu.VMEM((tm + 2 * POOL_HALO, d), F32)],
        compiler_params=_params(("parallel", "parallel")),
        name="mix",
    )(u, u, u, o, gp, ga, x, wmix, ps, wpo, wao, wo, gffn, wr)


PREFIX_BLOCK = 256
BF16_ROWS = 16
MIN_NORMAL_BITS = 0x00800000


def _prefix_rows(mask, tri):
    out = []
    carries = [jnp.zeros((1, mask.shape[1]), F32)]
    for blk in range(mask.shape[0] // PREFIX_BLOCK):
        m = mask[blk * PREFIX_BLOCK:(blk + 1) * PREFIX_BLOCK]
        p = _dot(tri, m.astype(BF16)) + carries[-1]
        carries.append(p[PREFIX_BLOCK - 1:PREFIX_BLOCK])
        out.append(p)
    return jnp.concatenate(out, axis=0), jnp.concatenate(carries, axis=0)


def _route_kernel(cap, lg_ref, slot_ref, aff_ref, start_ref):
    lg = lg_ref[0]
    lane = lax.broadcasted_iota(I32, lg.shape, 1)
    valid = lane < N_EXPERTS
    lg = jnp.where(valid, lg, -jnp.inf)
    ex = jnp.exp(lg - jnp.max(lg, axis=-1, keepdims=True))
    aff = ex / jnp.sum(ex, axis=-1, keepdims=True)
    aff_ref[0] = aff
    aff_e = aff.T[:N_EXPERTS, :]

    def step(k, ans):
        cand = ans | jnp.left_shift(jnp.int32(1), 30 - k)
        cnt = jnp.sum(jnp.where(aff_e >= lax.bitcast_convert_type(cand, F32), 1.0, 0.0), axis=1, keepdims=True)
        return jnp.where(cnt >= cap, cand, ans)

    thr_e = lax.fori_loop(0, 31, step, jnp.zeros((N_EXPERTS, 1), I32))
    thr_sq = jnp.concatenate([jnp.broadcast_to(thr_e, (N_EXPERTS, LANES)),
                              jnp.zeros((LANES - N_EXPERTS, LANES), I32)], axis=0)
    thr = thr_sq.T[0:1, :]
    thr = jnp.where(thr < MIN_NORMAL_BITS, 0, thr)
    nxt = jnp.where(thr == 0, MIN_NORMAL_BITS, thr + 1)
    lo = lax.bitcast_convert_type(thr, F32)
    hi = lax.bitcast_convert_type(nxt, F32)
    gt = jnp.where(aff >= hi, 1.0, 0.0)
    eq = jnp.where((aff >= lo) & (aff < hi), 1.0, 0.0)
    r, c = lax.broadcasted_iota(I32, (PREFIX_BLOCK, PREFIX_BLOCK), 0), lax.broadcasted_iota(I32, (PREFIX_BLOCK, PREFIX_BLOCK), 1)
    tri = jnp.where(c <= r, 1.0, 0.0).astype(BF16)
    room = cap - jnp.sum(gt, axis=0, keepdims=True)
    eq_before = _prefix_rows(eq, tri)[0] - eq
    sel = jnp.maximum(gt, jnp.where(eq_before < room, eq, 0.0))
    chosen_upto, block_start = _prefix_rows(sel, tri)
    slot_ref[0] = jnp.where(valid & (sel > 0.0), chosen_upto - 1.0, -1.0).astype(I32)
    start_ref[0] = block_start.astype(I32)


def _route(logits, cap):
    b, s, _ = logits.shape
    n_starts = s // PREFIX_BLOCK + 1
    spec = pl.BlockSpec((1, s, LANES), lambda bi: (bi, 0, 0))
    return pl.pallas_call(
        functools.partial(_route_kernel, cap),
        grid=(b,),
        in_specs=[spec],
        out_specs=[spec, spec, pl.BlockSpec((1, n_starts, LANES), lambda bi: (bi, 0, 0))],
        out_shape=[jax.ShapeDtypeStruct((b, s, LANES), I32), jax.ShapeDtypeStruct((b, s, LANES), F32),
                   jax.ShapeDtypeStruct((b, n_starts, LANES), I32)],
        compiler_params=_params(("parallel",)),
        name="route",
    )(logits)


def _slot_window(start_ref, base, cap):
    first = start_ref[base]
    end = start_ref[base + N_EXPERTS]
    w0 = jnp.minimum((first // BF16_ROWS) * BF16_ROWS, cap - PREFIX_BLOCK)
    return w0, end - w0 <= PREFIX_BLOCK


def _gather_kernel(cap, start_ref, slot_ref, h_ref, xs_ref, acc_ref):
    b = pl.program_id(0)
    e = pl.program_id(1)
    sid = slot_ref[0, 0]
    s = sid.shape[1]
    n_blk = s // PREFIX_BLOCK
    windows = [_slot_window(start_ref, (b * (n_blk + 1) + j) * N_EXPERTS + e, cap) for j in range(n_blk)]
    fits = functools.reduce(jnp.logical_and, [ok for _, ok in windows])

    @pl.when(fits)
    def _():
        acc_ref[...] = jnp.zeros_like(acc_ref)
        rows = lax.broadcasted_iota(I32, (PREFIX_BLOCK, PREFIX_BLOCK), 0)
        for j, (w0, _) in enumerate(windows):
            tok = slice(j * PREFIX_BLOCK, (j + 1) * PREFIX_BLOCK)
            hit = (sid[:, tok] - w0) == rows
            win = pl.ds(pl.multiple_of(w0, BF16_ROWS), PREFIX_BLOCK)
            acc_ref[win, :] += _dot(jnp.where(hit, 1.0, 0.0).astype(BF16), h_ref[0, tok, :])
        xs_ref[0] = acc_ref[...].astype(BF16)

    @pl.when(jnp.logical_not(fits))
    def _():
        hit = sid == lax.broadcasted_iota(I32, (cap, s), 0)
        xs_ref[0] = _dot(jnp.where(hit, 1.0, 0.0).astype(BF16), h_ref[0]).astype(BF16)


def _gather(starts, slot_row, hn, cap):
    b, s, d = hn.shape
    return pl.pallas_call(
        functools.partial(_gather_kernel, cap),
        grid_spec=pltpu.PrefetchScalarGridSpec(
            num_scalar_prefetch=1,
            grid=(b, N_EXPERTS),
            in_specs=[pl.BlockSpec((1, 1, 1, s), lambda bi, e, st: (bi, e, 0, 0)),
                      pl.BlockSpec((1, s, d), lambda bi, e, st: (bi, 0, 0))],
            out_specs=pl.BlockSpec((1, cap, d), lambda bi, e, st: (e, bi, 0)),
            scratch_shapes=[pltpu.VMEM((cap, d), F32)]),
        out_shape=jax.ShapeDtypeStruct((N_EXPERTS, b * cap, d), BF16),
        compiler_params=_params(("arbitrary", "arbitrary")),
        name="gather",
    )(starts, slot_row, hn)


def _ffn_kernel(xs_ref, wg_ref, wu_ref, wd_ref, y_ref, acc_ref):
    f = pl.program_id(1)

    @pl.when(f == 0)
    def _():
        acc_ref[...] = jnp.zeros_like(acc_ref)

    xs = xs_ref[0]
    g = _dot(xs, wg_ref[0].astype(BF16))
    u = _dot(xs, wu_ref[0].astype(BF16))
    act = (g / (1.0 + jnp.exp(-g)) * u).astype(BF16)
    acc_ref[...] += _dot(act, wd_ref[0].astype(BF16))

    @pl.when(f == pl.num_programs(1) - 1)
    def _():
        y_ref[0] = acc_ref[...].astype(BF16)


def _ffn(xs, w_gate, w_up, w_down, tf=256):
    e, m, d = xs.shape
    ff = w_gate.shape[2]
    row = lambda ei, f: (ei, 0, 0)
    return pl.pallas_call(
        _ffn_kernel,
        grid=(e, ff // tf),
        in_specs=[pl.BlockSpec((1, m, d), row),
                  pl.BlockSpec((1, d, tf), lambda ei, f: (ei, 0, f)),
                  pl.BlockSpec((1, d, tf), lambda ei, f: (ei, 0, f)),
                  pl.BlockSpec((1, tf, d), lambda ei, f: (ei, f, 0))],
        out_specs=pl.BlockSpec((1, m, d), row),
        out_shape=jax.ShapeDtypeStruct((e, m, d), BF16),
        scratch_shapes=[pltpu.VMEM((m, d), F32)],
        compiler_params=_params(("parallel", "arbitrary")),
        name="ffn",
    )(xs, w_gate, w_up, w_down)


def _combine_kernel(cap, start_ref, x1_ref, slot_ref, aff_ref, y_ref, g_ref, o_ref):
    b = pl.program_id(0)
    i = pl.program_id(1)
    slot = slot_ref[0]
    aff = aff_ref[0]
    base = (b * (pl.num_programs(1) + 1) + i) * N_EXPERTS
    windows = [_slot_window(start_ref, base + e, cap) for e in range(N_EXPERTS)]
    fits = functools.reduce(jnp.logical_and, [ok for _, ok in windows])

    def finish(acc):
        o_ref[0] = _rms(acc, g_ref[...])

    @pl.when(fits)
    def _():
        cidx = lax.broadcasted_iota(I32, (PREFIX_BLOCK, PREFIX_BLOCK), 1)
        acc = x1_ref[0]
        for e, (w0, _) in enumerate(windows):
            hit = (slot[:, e:e + 1] - w0) == cidx
            rows = y_ref[e, pl.ds(pl.multiple_of(w0, BF16_ROWS), PREFIX_BLOCK), :]
            acc = acc + aff[:, e:e + 1] * _dot(jnp.where(hit, 1.0, 0.0).astype(BF16), rows)
        finish(acc)

    @pl.when(jnp.logical_not(fits))
    def _():
        cidx = lax.broadcasted_iota(I32, (PREFIX_BLOCK, cap), 1)
        acc = x1_ref[0]
        for e in range(N_EXPERTS):
            hit = slot[:, e:e + 1] == cidx
            acc = acc + aff[:, e:e + 1] * _dot(jnp.where(hit, 1.0, 0.0).astype(BF16), y_ref[e])
        finish(acc)


def _combine(starts, x1, slot_t, aff_t, y, g_final, cap):
    b, s, d = x1.shape
    tile = lambda bi, i, st: (bi, i, 0)
    return pl.pallas_call(
        functools.partial(_combine_kernel, cap),
        grid_spec=pltpu.PrefetchScalarGridSpec(
            num_scalar_prefetch=1,
            grid=(b, s // PREFIX_BLOCK),
            in_specs=[pl.BlockSpec((1, PREFIX_BLOCK, d), tile), pl.BlockSpec((1, PREFIX_BLOCK, LANES), tile),
                      pl.BlockSpec((1, PREFIX_BLOCK, LANES), tile),
                      pl.BlockSpec((N_EXPERTS, cap, d), lambda bi, i, st: (0, bi, 0)),
                      pl.BlockSpec((1, d), lambda bi, i, st: (0, 0))],
            out_specs=pl.BlockSpec((1, PREFIX_BLOCK, d), tile)),
        out_shape=jax.ShapeDtypeStruct((b, s, d), F32),
        compiler_params=_params(("arbitrary", "arbitrary")),
        name="combine",
    )(starts, x1, slot_t, aff_t, y, g_final)


def _layer(x, positions, g_mix, w_in, w_pool_mix, pool_scale, w_pool_out, lam_q1, lam_k1, lam_q2,
           lam_k2, g_subln, w_attn_out, w_out, g_ffn, w_router, w_gate, w_up, w_down, lam_init):
    b, s, d = x.shape
    t = b * s
    cap = EC_CAPACITY_FACTOR * s // N_EXPERTS

    pos_row = positions.reshape(1, t).astype(F32)
    inv = (ROPE_THETA ** (-jnp.arange(ROT_HALF, dtype=F32) * 2.0 / ROT_DIM)).reshape(ROT_HALF, 1)

    u, q, k1, k2, v, gp, ga = _inproj(x.reshape(t, d), pos_row, g_mix.reshape(1, d), inv, w_in.astype(BF16))
    sh = lambda a: a.reshape(b, s, d)

    lamv = jnp.zeros((8, LANES), F32).at[0:4, 0:HEAD_DIM].set(jnp.stack([lam_q1, lam_k1, lam_q2, lam_k2]).astype(F32))
    vt = jnp.transpose(sh(v), (0, 2, 1))
    o = _attention(sh(q), sh(k1), sh(k2), vt, lamv, g_subln.reshape(1, HEAD_WIDTH), lam_init)

    wr = jnp.zeros((d, LANES), BF16).at[:, :N_EXPERTS].set(w_router.astype(BF16))
    x1, hn, logits = _mix(sh(u), o, sh(gp), sh(ga), x, w_pool_mix.astype(BF16), pool_scale.reshape(1, d),
                          w_pool_out.astype(BF16), w_attn_out.astype(BF16), w_out.astype(BF16),
                          g_ffn.reshape(1, d), wr)

    slot_t, aff_t, starts = _route(logits, cap)
    starts = starts[:, :, :N_EXPERTS].reshape(-1)
    slot_row = jnp.transpose(slot_t[:, :, :N_EXPERTS], (0, 2, 1)).reshape(b, N_EXPERTS, 1, s)
    xs = _gather(starts, slot_row, hn, cap)
    y = _ffn(xs, w_gate, w_up, w_down)
    return x1, starts, slot_t, aff_t, y, cap


def kernel(x, positions, g_mix, w_in, w_pool_mix, pool_scale, w_pool_out, lam_q1, lam_k1, lam_q2, lam_k2,
           g_subln, w_attn_out, w_out, g_ffn, w_router, w_gate, w_up, w_down, g_final):
    depth = g_mix.shape[0]
    assert depth == 1, "the final norm is fused into the last layer's combine step"
    l = 0
    lam_init = 0.8 - 0.6 * math.exp(-0.3 * l)
    x1, starts, slot_t, aff_t, y, cap = _layer(
        x, positions, g_mix[l], w_in[l], w_pool_mix[l], pool_scale[l], w_pool_out[l], lam_q1[l], lam_k1[l],
        lam_q2[l], lam_k2[l], g_subln[l], w_attn_out[l], w_out[l], g_ffn[l], w_router[l], w_gate[l], w_up[l],
        w_down[l], lam_init)
    return _combine(starts, x1, slot_t, aff_t, y, g_final.reshape(1, D_MODEL), cap)
```

```python
import functools
import math

import jax
import jax.numpy as jnp
from jax import lax
from jax.experimental import pallas as pl
from jax.experimental.pallas import tpu as pltpu

F32 = jnp.float32
BF16 = jnp.bfloat16
I32 = jnp.int32

D_MODEL = 1024
N_HEADS = 8
HEAD_DIM = 64
HEAD_WIDTH = 2 * HEAD_DIM
ROT_DIM = HEAD_DIM // 4
ROT_HALF = ROT_DIM // 2
ROPE_THETA = 500000.0
POOL_WINDOWS = (2, 4, 8, 16)
POOL_GROUP_DIM = D_MODEL // len(POOL_WINDOWS)
POOL_HALO = 16
N_EXPERTS = 16
EC_CAPACITY_FACTOR = 2
RMS_EPS = 1e-6
Q_SCALE = HEAD_DIM ** -0.5 * math.log2(math.e)
LANES = 128

VMEM_LIMIT = 56 * 1024 * 1024


def _params(sem):
    return pltpu.CompilerParams(dimension_semantics=sem, vmem_limit_bytes=VMEM_LIMIT)


def _rms(x, g):
    return (x * lax.rsqrt(jnp.mean(x * x, axis=-1, keepdims=True) + RMS_EPS)) * g


def _dot(a, b):
    return jnp.dot(a, b, preferred_element_type=F32)


def _rotary(z, cos, sin, first_half):
    up = pltpu.roll(z, LANES - ROT_HALF, 1)
    down = pltpu.roll(z, ROT_HALF, 1)
    partner = jnp.where(first_half, -up, down)
    return z * cos + partner * sin


def _inproj_kernel(x_ref, pos_ref, g_ref, inv_ref, w_ref,
                   u_ref, q_ref, k1_ref, k2_ref, v_ref, gp_ref, ga_ref):
    hb = _rms(x_ref[...], g_ref[...]).astype(BF16)
    ang = inv_ref[...] * pos_ref[...]
    cos_f, sin_f = jnp.cos(ang), jnp.sin(ang)
    tm = ang.shape[1]
    rest = HEAD_DIM - ROT_DIM
    cos = jnp.concatenate([cos_f, cos_f, jnp.ones((rest, tm), F32)] * 2, axis=0).T
    sin = jnp.concatenate([sin_f, sin_f, jnp.zeros((rest, tm), F32)] * 2, axis=0).T
    lane = lax.broadcasted_iota(I32, cos.shape, 1)
    first_half = (lane % HEAD_DIM) < ROT_HALF
    map1 = lane < HEAD_DIM

    def col(j):
        return _dot(hb, w_ref[:, j * D_MODEL:(j + 1) * D_MODEL])

    u_ref[...] = col(0)
    zq = col(1)
    zk = col(2)
    for h in range(N_HEADS):
        sl = slice(h * HEAD_WIDTH, (h + 1) * HEAD_WIDTH)
        q_ref[:, sl] = (_rotary(zq[:, sl], cos, sin, first_half) * Q_SCALE).astype(BF16)
        kr = _rotary(zk[:, sl], cos, sin, first_half)
        k1_ref[:, sl] = jnp.where(map1, kr, 0.0).astype(BF16)
        k2_ref[:, sl] = jnp.where(map1, 0.0, kr).astype(BF16)
    v_ref[...] = col(3).astype(BF16)
    gp_ref[...] = 1.0 / (1.0 + jnp.exp(-col(4)))
    ga_ref[...] = 1.0 / (1.0 + jnp.exp(-col(5)))


def _inproj(x2d, pos_row, g_mix, inv_freq, w_in_bf, tm=256):
    t = x2d.shape[0]
    n_cols = w_in_bf.shape[1]
    row = lambda i: (i, 0)
    const = lambda i: (0, 0)
    tile = pl.BlockSpec((tm, D_MODEL), row)
    out_shapes = [jax.ShapeDtypeStruct((t, D_MODEL), dt) for dt in (F32, BF16, BF16, BF16, BF16, F32, F32)]
    return pl.pallas_call(
        _inproj_kernel,
        grid=(t // tm,),
        in_specs=[tile, pl.BlockSpec((1, tm), lambda i: (0, i)), pl.BlockSpec((1, D_MODEL), const),
                  pl.BlockSpec((ROT_HALF, 1), const), pl.BlockSpec((D_MODEL, n_cols), const)],
        out_specs=[tile] * 7,
        out_shape=out_shapes,
        compiler_params=_params(("parallel",)),
        name="inproj",
    )(x2d, pos_row, g_mix, inv_freq, w_in_bf)


SCORE_SLOTS = 3
SUM_ROWS = 16

def _attn_kernel(lam_init, tk, q_ref, k1_ref, k2_ref, vt_ref, lamv_ref, g_ref, o_ref, st_ref, acc_ref):
    lv = lamv_ref[...]
    lam = (jnp.exp(jnp.sum(lv[0:1] * lv[1:2], axis=-1, keepdims=True))
           - jnp.exp(jnp.sum(lv[2:3] * lv[3:4], axis=-1, keepdims=True)) + lam_init)
    q = q_ref[0]
    tq = q.shape[0]
    n_chunks = k1_ref.shape[1] // tk
    nt = (((1,), (1,)), ((), ()))
    k_refs = (k1_ref, k2_ref)

    def scores(c):
        for mp in range(2):
            st_ref[c % SCORE_SLOTS, mp] = lax.dot_general(k_refs[mp][0, c * tk:(c + 1) * tk, :], q, nt,
                                                          preferred_element_type=F32)

    ones_rows = jnp.ones((SUM_ROWS, tk), BF16)

    def accumulate(c, m):
        out_m = []
        for mp in range(2):
            st = st_ref[c % SCORE_SLOTS, mp]
            m_new = jnp.maximum(m[mp], jnp.max(st, axis=0, keepdims=True))
            alpha = jnp.exp2(m[mp] - m_new)
            et = jnp.exp2(st - m_new).astype(BF16)
            vt_ext = jnp.concatenate([vt_ref[0, :, c * tk:(c + 1) * tk], ones_rows], axis=0)
            pv = _dot(vt_ext, et)
            acc_ref[mp] = pv if c == 0 else acc_ref[mp] * alpha + pv
            out_m.append(m_new)
        return out_m

    m = [jnp.full((1, tq), -jnp.inf, F32)] * 2
    for c in range(min(SCORE_SLOTS - 1, n_chunks)):
        scores(c)
    for c in range(n_chunks):
        if c + SCORE_SLOTS - 1 < n_chunks:
            scores(c + SCORE_SLOTS - 1)
        m = accumulate(c, m)
    a1, a2 = acc_ref[0], acc_ref[1]
    r1 = 1.0 / a1[HEAD_WIDTH:HEAD_WIDTH + 1]
    r2 = lam / a2[HEAD_WIDTH:HEAD_WIDTH + 1]
    ot = a1[:HEAD_WIDTH] * r1 - a2[:HEAD_WIDTH] * r2
    o_ref[0] = (_rms(ot.T, g_ref[...]) * (1.0 - lam_init)).astype(BF16)


def _attention(q, k1, k2, vt, lamv, g_subln, lam_init, tq=1024, tk=256):
    b, s, _ = q.shape
    qspec = pl.BlockSpec((1, tq, HEAD_WIDTH), lambda bi, h, qi: (bi, qi, h))
    kspec = pl.BlockSpec((1, s, HEAD_WIDTH), lambda bi, h, qi: (bi, 0, h))
    vspec = pl.BlockSpec((1, HEAD_WIDTH, s), lambda bi, h, qi: (bi, h, 0))
    const = lambda bi, h, qi: (0, 0)
    return pl.pallas_call(
        functools.partial(_attn_kernel, lam_init, tk),
        grid=(b, N_HEADS, s // tq),
        in_specs=[qspec, kspec, kspec, vspec,
                  pl.BlockSpec((8, LANES), const), pl.BlockSpec((1, HEAD_WIDTH), const)],
        out_specs=qspec,
        out_shape=jax.ShapeDtypeStruct(k1.shape, BF16),
        scratch_shapes=[pltpu.VMEM((SCORE_SLOTS, 2, tk, tq), F32), pltpu.VMEM((2, HEAD_WIDTH + SUM_ROWS, tq), F32)],
        compiler_params=_params(("parallel", "parallel", "arbitrary")),
        name="attn",
    )(q, k1, k2, vt, lamv, g_subln)


def _mix_kernel(seq, u_ref, up_ref, un_ref, o_ref, gp_ref, ga_ref, x_ref,
                wmix_ref, ps_ref, wpo_ref, wao_ref, wo_ref, gffn_ref, wr_ref,
                x1_ref, hn_ref, lg_ref, upad_ref):
    tm = u_ref.shape[1]
    i = pl.program_id(1)
    nt = pl.num_programs(1)
    main = u_ref[0]
    upad_ref[0:POOL_HALO, :] = jnp.where(i > 0, up_ref[0], 0.0)
    upad_ref[POOL_HALO:POOL_HALO + tm, :] = main
    upad_ref[POOL_HALO + tm:, :] = jnp.where(i < nt - 1, un_ref[0], 0.0)
    t = i * tm + lax.broadcasted_iota(I32, (tm, 1), 0)
    mixed = []
    for g, w in enumerate(POOL_WINDOWS):
        cols = slice(g * POOL_GROUP_DIM, (g + 1) * POOL_GROUP_DIM)
        win = upad_ref[POOL_HALO - w // 2:POOL_HALO - w // 2 + tm, cols]
        for d in range(-w // 2 + 1, w // 2):
            win = win + upad_ref[POOL_HALO + d:POOL_HALO + d + tm, cols]
        cnt = (jnp.minimum(t + w // 2, seq) - jnp.maximum(t - w // 2, 0)).astype(F32)
        pooled = win / cnt - main[:, cols]
        mixed.append((_dot(pooled.astype(BF16), wmix_ref[g]) * ps_ref[:, cols]).astype(BF16))
    y_pool = _dot(jnp.concatenate(mixed, axis=-1), wpo_ref[...])
    y_attn = _dot(o_ref[0], wao_ref[...])
    merged = gp_ref[0] * y_pool + ga_ref[0] * y_attn
    x1 = x_ref[0] + _dot(merged.astype(BF16), wo_ref[...])
    x1_ref[0] = x1
    hn = _rms(x1, gffn_ref[...]).astype(BF16)
    hn_ref[0] = hn
    lg_ref[0] = _dot(hn, wr_ref[...])


def _mix(u, o, gp, ga, x, wmix, ps, wpo, wao, wo, gffn, wr, tm=512):
    b, s, d = x.shape
    hb = tm // POOL_HALO
    n_halo = s // POOL_HALO
    tile = pl.BlockSpec((1, tm, d), lambda bi, i: (bi, i, 0))
    prev = pl.BlockSpec((1, POOL_HALO, d), lambda bi, i: (bi, jnp.maximum(i * hb - 1, 0), 0))
    nxt = pl.BlockSpec((1, POOL_HALO, d), lambda bi, i: (bi, jnp.minimum((i + 1) * hb, n_halo - 1), 0))
    c2 = lambda bi, i: (0, 0)
    c3 = lambda bi, i: (0, 0, 0)
    return pl.pallas_call(
        functools.partial(_mix_kernel, s),
        grid=(b, s // tm),
        in_specs=[tile, prev, nxt, tile, tile, tile, tile,
                  pl.BlockSpec(wmix.shape, c3), pl.BlockSpec((1, d), c2),
                  pl.BlockSpec((d, d), c2), pl.BlockSpec((d, d), c2), pl.BlockSpec((d, d), c2),
                  pl.BlockSpec((1, d), c2), pl.BlockSpec((d, LANES), c2)],
        out_specs=[tile, tile, pl.BlockSpec((1, tm, LANES), lambda bi, i: (bi, i, 0))],
        out_shape=[jax.ShapeDtypeStruct((b, s, d), F32), jax.ShapeDtypeStruct((b, s, d), BF16),
                   jax.ShapeDtypeStruct((b, s, LANES), F32)],
        scratch_shapes=[pltpu.VMEM((tm + 2 * POOL_HALO, d), F32)],
        compiler_params=_params(("parallel", "parallel")),
        name="mix",
    )(u, u, u, o, gp, ga, x, wmix, ps, wpo, wao, wo, gffn, wr)


PREFIX_BLOCK = 256
BF16_ROWS = 16
MIN_NORMAL_BITS = 0x00800000


def _prefix_rows(mask, tri):
    out = []
    carries = [jnp.zeros((1, mask.shape[1]), F32)]
    for blk in range(mask.shape[0] // PREFIX_BLOCK):
        m = mask[blk * PREFIX_BLOCK:(blk + 1) * PREFIX_BLOCK]
        p = _dot(tri, m.astype(BF16)) + carries[-1]
        carries.append(p[PREFIX_BLOCK - 1:PREFIX_BLOCK])
        out.append(p)
    return jnp.concatenate(out, axis=0), jnp.concatenate(carries, axis=0)


def _route_kernel(cap, lg_ref, slot_ref, aff_ref, start_ref):
    lg = lg_ref[0]
    lane = lax.broadcasted_iota(I32, lg.shape, 1)
    valid = lane < N_EXPERTS
    lg = jnp.where(valid, lg, -jnp.inf)
    ex = jnp.exp(lg - jnp.max(lg, axis=-1, keepdims=True))
    aff = ex / jnp.sum(ex, axis=-1, keepdims=True)
    aff_ref[0] = aff
    aff_e = aff.T[:N_EXPERTS, :]

    def step(k, ans):
        cand = ans | jnp.left_shift(jnp.int32(1), 30 - k)
        cnt = jnp.sum(jnp.where(aff_e >= lax.bitcast_convert_type(cand, F32), 1.0, 0.0), axis=1, keepdims=True)
        return jnp.where(cnt >= cap, cand, ans)

    thr_e = lax.fori_loop(0, 31, step, jnp.zeros((N_EXPERTS, 1), I32))
    thr_sq = jnp.concatenate([jnp.broadcast_to(thr_e, (N_EXPERTS, LANES)),
                              jnp.zeros((LANES - N_EXPERTS, LANES), I32)], axis=0)
    thr = thr_sq.T[0:1, :]
    thr = jnp.where(thr < MIN_NORMAL_BITS, 0, thr)
    nxt = jnp.where(thr == 0, MIN_NORMAL_BITS, thr + 1)
    lo = lax.bitcast_convert_type(thr, F32)
    hi = lax.bitcast_convert_type(nxt, F32)
    gt = jnp.where(aff >= hi, 1.0, 0.0)
    eq = jnp.where((aff >= lo) & (aff < hi), 1.0, 0.0)
    r, c = lax.broadcasted_iota(I32, (PREFIX_BLOCK, PREFIX_BLOCK), 0), lax.broadcasted_iota(I32, (PREFIX_BLOCK, PREFIX_BLOCK), 1)
    tri = jnp.where(c <= r, 1.0, 0.0).astype(BF16)
    room = cap - jnp.sum(gt, axis=0, keepdims=True)
    eq_before = _prefix_rows(eq, tri)[0] - eq
    sel = jnp.maximum(gt, jnp.where(eq_before < room, eq, 0.0))
    chosen_upto, block_start = _prefix_rows(sel, tri)
    slot_ref[0] = jnp.where(valid & (sel > 0.0), chosen_upto - 1.0, -1.0).astype(I32)
    start_ref[0] = block_start.astype(I32)


def _route(logits, cap):
    b, s, _ = logits.shape
    n_starts = s // PREFIX_BLOCK + 1
    spec = pl.BlockSpec((1, s, LANES), lambda bi: (bi, 0, 0))
    return pl.pallas_call(
        functools.partial(_route_kernel, cap),
        grid=(b,),
        in_specs=[spec],
        out_specs=[spec, spec, pl.BlockSpec((1, n_starts, LANES), lambda bi: (bi, 0, 0))],
        out_shape=[jax.ShapeDtypeStruct((b, s, LANES), I32), jax.ShapeDtypeStruct((b, s, LANES), F32),
                   jax.ShapeDtypeStruct((b, n_starts, LANES), I32)],
        compiler_params=_params(("parallel",)),
        name="route",
    )(logits)


def _slot_window(start_ref, base, cap):
    first = start_ref[base]
    end = start_ref[base + N_EXPERTS]
    w0 = jnp.minimum((first // BF16_ROWS) * BF16_ROWS, cap - PREFIX_BLOCK)
    return w0, end - w0 <= PREFIX_BLOCK


def _gather_kernel(cap, start_ref, slot_ref, h_ref, xs_ref, acc_ref):
    b = pl.program_id(0)
    e = pl.program_id(1)
    sid = slot_ref[0, 0]
    s = sid.shape[1]
    n_blk = s // PREFIX_BLOCK
    windows = [_slot_window(start_ref, (b * (n_blk + 1) + j) * N_EXPERTS + e, cap) for j in range(n_blk)]
    fits = functools.reduce(jnp.logical_and, [ok for _, ok in windows])

    @pl.when(fits)
    def _():
        acc_ref[...] = jnp.zeros_like(acc_ref)
        rows = lax.broadcasted_iota(I32, (PREFIX_BLOCK, PREFIX_BLOCK), 0)
        for j, (w0, _) in enumerate(windows):
            tok = slice(j * PREFIX_BLOCK, (j + 1) * PREFIX_BLOCK)
            hit = (sid[:, tok] - w0) == rows
            win = pl.ds(pl.multiple_of(w0, BF16_ROWS), PREFIX_BLOCK)
            acc_ref[win, :] += _dot(jnp.where(hit, 1.0, 0.0).astype(BF16), h_ref[0, tok, :])
        xs_ref[0] = acc_ref[...].astype(BF16)

    @pl.when(jnp.logical_not(fits))
    def _():
        hit = sid == lax.broadcasted_iota(I32, (cap, s), 0)
        xs_ref[0] = _dot(jnp.where(hit, 1.0, 0.0).astype(BF16), h_ref[0]).astype(BF16)


def _gather(starts, slot_row, hn, cap):
    b, s, d = hn.shape
    return pl.pallas_call(
        functools.partial(_gather_kernel, cap),
        grid_spec=pltpu.PrefetchScalarGridSpec(
            num_scalar_prefetch=1,
            grid=(b, N_EXPERTS),
            in_specs=[pl.BlockSpec((1, 1, 1, s), lambda bi, e, st: (bi, e, 0, 0)),
                      pl.BlockSpec((1, s, d), lambda bi, e, st: (bi, 0, 0))],
            out_specs=pl.BlockSpec((1, cap, d), lambda bi, e, st: (e, bi, 0)),
            scratch_shapes=[pltpu.VMEM((cap, d), F32)]),
        out_shape=jax.ShapeDtypeStruct((N_EXPERTS, b * cap, d), BF16),
        compiler_params=_params(("arbitrary", "arbitrary")),
        name="gather",
    )(starts, slot_row, hn)


FFN_ROW_GROUPS = 2


def _ffn_kernel(xs_ref, wg_ref, wu_ref, wd_ref, y_ref, acc_ref):
    f = pl.program_id(1)

    @pl.when(f == 0)
    def _():
        acc_ref[...] = jnp.zeros_like(acc_ref)

    wg, wu, wd = wg_ref[0].astype(BF16), wu_ref[0].astype(BF16), wd_ref[0].astype(BF16)
    rows = xs_ref.shape[1] // FFN_ROW_GROUPS
    for r in range(FFN_ROW_GROUPS):
        sl = slice(r * rows, (r + 1) * rows)
        xs = xs_ref[0, sl, :]
        g = _dot(xs, wg)
        u = _dot(xs, wu)
        act = (g / (1.0 + jnp.exp(-g)) * u).astype(BF16)
        acc_ref[sl, :] += _dot(act, wd)

    @pl.when(f == pl.num_programs(1) - 1)
    def _():
        y_ref[0] = acc_ref[...].astype(BF16)


def _ffn(xs, w_gate, w_up, w_down, tf=256):
    e, m, d = xs.shape
    ff = w_gate.shape[2]
    row = lambda ei, f: (ei, 0, 0)
    return pl.pallas_call(
        _ffn_kernel,
        grid=(e, ff // tf),
        in_specs=[pl.BlockSpec((1, m, d), row),
                  pl.BlockSpec((1, d, tf), lambda ei, f: (ei, 0, f)),
                  pl.BlockSpec((1, d, tf), lambda ei, f: (ei, 0, f)),
                  pl.BlockSpec((1, tf, d), lambda ei, f: (ei, f, 0))],
        out_specs=pl.BlockSpec((1, m, d), row),
        out_shape=jax.ShapeDtypeStruct((e, m, d), BF16),
        scratch_shapes=[pltpu.VMEM((m, d), F32)],
        compiler_params=_params(("parallel", "arbitrary")),
        name="ffn",
    )(xs, w_gate, w_up, w_down)


def _combine_kernel(cap, start_ref, x1_ref, slot_ref, aff_ref, y_ref, g_ref, o_ref):
    b = pl.program_id(0)
    i = pl.program_id(1)
    slot = slot_ref[0]
    aff = aff_ref[0]
    base = (b * (pl.num_programs(1) + 1) + i) * N_EXPERTS
    windows = [_slot_window(start_ref, base + e, cap) for e in range(N_EXPERTS)]
    fits = functools.reduce(jnp.logical_and, [ok for _, ok in windows])

    def finish(acc):
        o_ref[0] = _rms(acc, g_ref[...])

    @pl.when(fits)
    def _():
        cidx = lax.broadcasted_iota(I32, (PREFIX_BLOCK, PREFIX_BLOCK), 1)
        acc = x1_ref[0]
        for e, (w0, _) in enumerate(windows):
            hit = (slot[:, e:e + 1] - w0) == cidx
            rows = y_ref[e, pl.ds(pl.multiple_of(w0, BF16_ROWS), PREFIX_BLOCK), :]
            acc = acc + aff[:, e:e + 1] * _dot(jnp.where(hit, 1.0, 0.0).astype(BF16), rows)
        finish(acc)

    @pl.when(jnp.logical_not(fits))
    def _():
        cidx = lax.broadcasted_iota(I32, (PREFIX_BLOCK, cap), 1)
        acc = x1_ref[0]
        for e in range(N_EXPERTS):
            hit = slot[:, e:e + 1] == cidx
            acc = acc + aff[:, e:e + 1] * _dot(jnp.where(hit, 1.0, 0.0).astype(BF16), y_ref[e])
        finish(acc)


def _combine(starts, x1, slot_t, aff_t, y, g_final, cap):
    b, s, d = x1.shape
    tile = lambda bi, i, st: (bi, i, 0)
    return pl.pallas_call(
        functools.partial(_combine_kernel, cap),
        grid_spec=pltpu.PrefetchScalarGridSpec(
            num_scalar_prefetch=1,
            grid=(b, s // PREFIX_BLOCK),
            in_specs=[pl.BlockSpec((1, PREFIX_BLOCK, d), tile), pl.BlockSpec((1, PREFIX_BLOCK, LANES), tile),
                      pl.BlockSpec((1, PREFIX_BLOCK, LANES), tile),
                      pl.BlockSpec((N_EXPERTS, cap, d), lambda bi, i, st: (0, bi, 0)),
                      pl.BlockSpec((1, d), lambda bi, i, st: (0, 0))],
            out_specs=pl.BlockSpec((1, PREFIX_BLOCK, d), tile)),
        out_shape=jax.ShapeDtypeStruct((b, s, d), F32),
        compiler_params=_params(("arbitrary", "arbitrary")),
        name="combine",
    )(starts, x1, slot_t, aff_t, y, g_final)


def _layer(x, positions, g_mix, w_in, w_pool_mix, pool_scale, w_pool_out, lam_q1, lam_k1, lam_q2,
           lam_k2, g_subln, w_attn_out, w_out, g_ffn, w_router, w_gate, w_up, w_down, lam_init):
    b, s, d = x.shape
    t = b * s
    cap = EC_CAPACITY_FACTOR * s // N_EXPERTS

    pos_row = positions.reshape(1, t).astype(F32)
    inv = (ROPE_THETA ** (-jnp.arange(ROT_HALF, dtype=F32) * 2.0 / ROT_DIM)).reshape(ROT_HALF, 1)

    u, q, k1, k2, v, gp, ga = _inproj(x.reshape(t, d), pos_row, g_mix.reshape(1, d), inv, w_in.astype(BF16))
    sh = lambda a: a.reshape(b, s, d)

    lamv = jnp.zeros((8, LANES), F32).at[0:4, 0:HEAD_DIM].set(jnp.stack([lam_q1, lam_k1, lam_q2, lam_k2]).astype(F32))
    vt = jnp.transpose(sh(v), (0, 2, 1))
    o = _attention(sh(q), sh(k1), sh(k2), vt, lamv, g_subln.reshape(1, HEAD_WIDTH), lam_init)

    wr = jnp.zeros((d, LANES), BF16).at[:, :N_EXPERTS].set(w_router.astype(BF16))
    x1, hn, logits = _mix(sh(u), o, sh(gp), sh(ga), x, w_pool_mix.astype(BF16), pool_scale.reshape(1, d),
                          w_pool_out.astype(BF16), w_attn_out.astype(BF16), w_out.astype(BF16),
                          g_ffn.reshape(1, d), wr)

    slot_t, aff_t, starts = _route(logits, cap)
    starts = starts[:, :, :N_EXPERTS].reshape(-1)
    slot_row = jnp.transpose(slot_t[:, :, :N_EXPERTS], (0, 2, 1)).reshape(b, N_EXPERTS, 1, s)
    xs = _gather(starts, slot_row, hn, cap)
    y = _ffn(xs, w_gate, w_up, w_down)
    return x1, starts, slot_t, aff_t, y, cap


def kernel(x, positions, g_mix, w_in, w_pool_mix, pool_scale, w_pool_out, lam_q1, lam_k1, lam_q2, lam_k2,
           g_subln, w_attn_out, w_out, g_ffn, w_router, w_gate, w_up, w_down, g_final):
    depth = g_mix.shape[0]
    assert depth == 1, "the final norm is fused into the last layer's combine step"
    l = 0
    lam_init = 0.8 - 0.6 * math.exp(-0.3 * l)
    x1, starts, slot_t, aff_t, y, cap = _layer(
        x, positions, g_mix[l], w_in[l], w_pool_mix[l], pool_scale[l], w_pool_out[l], lam_q1[l], lam_k1[l],
        lam_q2[l], lam_k2[l], g_subln[l], w_attn_out[l], w_out[l], g_ffn[l], w_router[l], w_gate[l], w_up[l],
        w_down[l], lam_init)
    return _combine(starts, x1, slot_t, aff_t, y, g_final.reshape(1, D_MODEL), cap)
```

```python
import functools
import math

import jax
import jax.numpy as jnp
from jax import lax
from jax.experimental import pallas as pl
from jax.experimental.pallas import tpu as pltpu

F32 = jnp.float32
BF16 = jnp.bfloat16
I32 = jnp.int32

D_MODEL = 1024
N_HEADS = 8
HEAD_DIM = 64
HEAD_WIDTH = 2 * HEAD_DIM
ROT_DIM = HEAD_DIM // 4
ROT_HALF = ROT_DIM // 2
ROPE_THETA = 500000.0
POOL_WINDOWS = (2, 4, 8, 16)
POOL_GROUP_DIM = D_MODEL // len(POOL_WINDOWS)
POOL_HALO = 16
N_EXPERTS = 16
EC_CAPACITY_FACTOR = 2
RMS_EPS = 1e-6
Q_SCALE = HEAD_DIM ** -0.5 * math.log2(math.e)
LANES = 128

VMEM_LIMIT = 56 * 1024 * 1024


def _params(sem):
    return pltpu.CompilerParams(dimension_semantics=sem, vmem_limit_bytes=VMEM_LIMIT)


def _rms(x, g):
    return (x * lax.rsqrt(jnp.mean(x * x, axis=-1, keepdims=True) + RMS_EPS)) * g


def _dot(a, b):
    return jnp.dot(a, b, preferred_element_type=F32)


def _rotary(z, cos, sin, first_half):
    up = pltpu.roll(z, LANES - ROT_HALF, 1)
    down = pltpu.roll(z, ROT_HALF, 1)
    partner = jnp.where(first_half, -up, down)
    return z * cos + partner * sin


def _inproj_kernel(x_ref, pos_ref, g_ref, inv_ref, w_ref,
                   u_ref, q_ref, k1_ref, k2_ref, v_ref, gp_ref, ga_ref):
    hb = _rms(x_ref[...], g_ref[...]).astype(BF16)
    ang = inv_ref[...] * pos_ref[...]
    cos_f, sin_f = jnp.cos(ang), jnp.sin(ang)
    tm = ang.shape[1]
    rest = HEAD_DIM - ROT_DIM
    cos = jnp.concatenate([cos_f, cos_f, jnp.ones((rest, tm), F32)] * 2, axis=0).T
    sin = jnp.concatenate([sin_f, sin_f, jnp.zeros((rest, tm), F32)] * 2, axis=0).T
    lane = lax.broadcasted_iota(I32, cos.shape, 1)
    first_half = (lane % HEAD_DIM) < ROT_HALF
    map1 = lane < HEAD_DIM

    def col(j):
        return _dot(hb, w_ref[:, j * D_MODEL:(j + 1) * D_MODEL])

    u_ref[...] = col(0)
    zq = col(1)
    zk = col(2)
    for h in range(N_HEADS):
        sl = slice(h * HEAD_WIDTH, (h + 1) * HEAD_WIDTH)
        q_ref[:, sl] = (_rotary(zq[:, sl], cos, sin, first_half) * Q_SCALE).astype(BF16)
        kr = _rotary(zk[:, sl], cos, sin, first_half)
        k1_ref[:, sl] = jnp.where(map1, kr, 0.0).astype(BF16)
        k2_ref[:, sl] = jnp.where(map1, 0.0, kr).astype(BF16)
    v_ref[...] = col(3).astype(BF16)
    gp_ref[...] = 1.0 / (1.0 + jnp.exp(-col(4)))
    ga_ref[...] = 1.0 / (1.0 + jnp.exp(-col(5)))


def _inproj(x2d, pos_row, g_mix, inv_freq, w_in_bf, tm=256):
    t = x2d.shape[0]
    n_cols = w_in_bf.shape[1]
    row = lambda i: (i, 0)
    const = lambda i: (0, 0)
    tile = pl.BlockSpec((tm, D_MODEL), row)
    out_shapes = [jax.ShapeDtypeStruct((t, D_MODEL), dt) for dt in (F32, BF16, BF16, BF16, BF16, F32, F32)]
    return pl.pallas_call(
        _inproj_kernel,
        grid=(t // tm,),
        in_specs=[tile, pl.BlockSpec((1, tm), lambda i: (0, i)), pl.BlockSpec((1, D_MODEL), const),
                  pl.BlockSpec((ROT_HALF, 1), const), pl.BlockSpec((D_MODEL, n_cols), const)],
        out_specs=[tile] * 7,
        out_shape=out_shapes,
        compiler_params=_params(("parallel",)),
        name="inproj",
    )(x2d, pos_row, g_mix, inv_freq, w_in_bf)


SCORE_SLOTS = 3
SUM_ROWS = 16

def _attn_kernel(lam_init, tk, q_ref, k1_ref, k2_ref, vt_ref, lamv_ref, g_ref, o_ref, st_ref, acc_ref):
    lv = lamv_ref[...]
    lam = (jnp.exp(jnp.sum(lv[0:1] * lv[1:2], axis=-1, keepdims=True))
           - jnp.exp(jnp.sum(lv[2:3] * lv[3:4], axis=-1, keepdims=True)) + lam_init)
    q = q_ref[0]
    tq = q.shape[0]
    n_chunks = k1_ref.shape[1] // tk
    nt = (((1,), (1,)), ((), ()))
    k_refs = (k1_ref, k2_ref)

    def scores(c):
        for mp in range(2):
            st_ref[c % SCORE_SLOTS, mp] = lax.dot_general(k_refs[mp][0, c * tk:(c + 1) * tk, :], q, nt,
                                                          preferred_element_type=F32)

    ones_rows = jnp.ones((SUM_ROWS, tk), BF16)

    def fold(c, mp, pv, alpha):
        acc_ref[mp] = pv if c == 0 else acc_ref[mp] * alpha + pv

    def accumulate(c, m, pending):
        out_m, out_pending = [], []
        for mp in range(2):
            if pending:
                fold(c - 1, mp, *pending[mp])
            st = st_ref[c % SCORE_SLOTS, mp]
            m_new = jnp.maximum(m[mp], jnp.max(st, axis=0, keepdims=True))
            alpha = jnp.exp2(m[mp] - m_new)
            et = jnp.exp2(st - m_new).astype(BF16)
            vt_ext = jnp.concatenate([vt_ref[0, :, c * tk:(c + 1) * tk], ones_rows], axis=0)
            out_pending.append((_dot(vt_ext, et), alpha))
            out_m.append(m_new)
        return out_m, out_pending

    m = [jnp.full((1, tq), -jnp.inf, F32)] * 2
    pending = []
    for c in range(min(SCORE_SLOTS - 1, n_chunks)):
        scores(c)
    for c in range(n_chunks):
        if c + SCORE_SLOTS - 1 < n_chunks:
            scores(c + SCORE_SLOTS - 1)
        m, pending = accumulate(c, m, pending)
    for mp in range(2):
        fold(n_chunks - 1, mp, *pending[mp])
    a1, a2 = acc_ref[0], acc_ref[1]
    r1 = 1.0 / a1[HEAD_WIDTH:HEAD_WIDTH + 1]
    r2 = lam / a2[HEAD_WIDTH:HEAD_WIDTH + 1]
    ot = a1[:HEAD_WIDTH] * r1 - a2[:HEAD_WIDTH] * r2
    o_ref[0] = (_rms(ot.T, g_ref[...]) * (1.0 - lam_init)).astype(BF16)


def _attention(q, k1, k2, vt, lamv, g_subln, lam_init, tq=1024, tk=256):
    b, s, _ = q.shape
    qspec = pl.BlockSpec((1, tq, HEAD_WIDTH), lambda bi, h, qi: (bi, qi, h))
    kspec = pl.BlockSpec((1, s, HEAD_WIDTH), lambda bi, h, qi: (bi, 0, h))
    vspec = pl.BlockSpec((1, HEAD_WIDTH, s), lambda bi, h, qi: (bi, h, 0))
    const = lambda bi, h, qi: (0, 0)
    return pl.pallas_call(
        functools.partial(_attn_kernel, lam_init, tk),
        grid=(b, N_HEADS, s // tq),
        in_specs=[qspec, kspec, kspec, vspec,
                  pl.BlockSpec((8, LANES), const), pl.BlockSpec((1, HEAD_WIDTH), const)],
        out_specs=qspec,
        out_shape=jax.ShapeDtypeStruct(k1.shape, BF16),
        scratch_shapes=[pltpu.VMEM((SCORE_SLOTS, 2, tk, tq), F32), pltpu.VMEM((2, HEAD_WIDTH + SUM_ROWS, tq), F32)],
        compiler_params=_params(("parallel", "parallel", "arbitrary")),
        name="attn",
    )(q, k1, k2, vt, lamv, g_subln)


def _mix_kernel(seq, u_ref, up_ref, un_ref, o_ref, gp_ref, ga_ref, x_ref,
                wmix_ref, ps_ref, wpo_ref, wao_ref, wo_ref, gffn_ref, wr_ref,
                x1_ref, hn_ref, lg_ref, upad_ref):
    tm = u_ref.shape[1]
    i = pl.program_id(1)
    nt = pl.num_programs(1)
    main = u_ref[0]
    upad_ref[0:POOL_HALO, :] = jnp.where(i > 0, up_ref[0], 0.0)
    upad_ref[POOL_HALO:POOL_HALO + tm, :] = main
    upad_ref[POOL_HALO + tm:, :] = jnp.where(i < nt - 1, un_ref[0], 0.0)
    t = i * tm + lax.broadcasted_iota(I32, (tm, 1), 0)
    mixed = []
    for g, w in enumerate(POOL_WINDOWS):
        cols = slice(g * POOL_GROUP_DIM, (g + 1) * POOL_GROUP_DIM)
        win = upad_ref[POOL_HALO - w // 2:POOL_HALO - w // 2 + tm, cols]
        for d in range(-w // 2 + 1, w // 2):
            win = win + upad_ref[POOL_HALO + d:POOL_HALO + d + tm, cols]
        cnt = (jnp.minimum(t + w // 2, seq) - jnp.maximum(t - w // 2, 0)).astype(F32)
        pooled = win / cnt - main[:, cols]
        mixed.append((_dot(pooled.astype(BF16), wmix_ref[g]) * ps_ref[:, cols]).astype(BF16))
    y_pool = _dot(jnp.concatenate(mixed, axis=-1), wpo_ref[...])
    y_attn = _dot(o_ref[0], wao_ref[...])
    merged = gp_ref[0] * y_pool + ga_ref[0] * y_attn
    x1 = x_ref[0] + _dot(merged.astype(BF16), wo_ref[...])
    x1_ref[0] = x1
    hn = _rms(x1, gffn_ref[...]).astype(BF16)
    hn_ref[0] = hn
    lg_ref[0] = _dot(hn, wr_ref[...])


def _mix(u, o, gp, ga, x, wmix, ps, wpo, wao, wo, gffn, wr, tm=512):
    b, s, d = x.shape
    hb = tm // POOL_HALO
    n_halo = s // POOL_HALO
    tile = pl.BlockSpec((1, tm, d), lambda bi, i: (bi, i, 0))
    prev = pl.BlockSpec((1, POOL_HALO, d), lambda bi, i: (bi, jnp.maximum(i * hb - 1, 0), 0))
    nxt = pl.BlockSpec((1, POOL_HALO, d), lambda bi, i: (bi, jnp.minimum((i + 1) * hb, n_halo - 1), 0))
    c2 = lambda bi, i: (0, 0)
    c3 = lambda bi, i: (0, 0, 0)
    return pl.pallas_call(
        functools.partial(_mix_kernel, s),
        grid=(b, s // tm),
        in_specs=[tile, prev, nxt, tile, tile, tile, tile,
                  pl.BlockSpec(wmix.shape, c3), pl.BlockSpec((1, d), c2),
                  pl.BlockSpec((d, d), c2), pl.BlockSpec((d, d), c2), pl.BlockSpec((d, d), c2),
                  pl.BlockSpec((1, d), c2), pl.BlockSpec((d, LANES), c2)],
        out_specs=[tile, tile, pl.BlockSpec((1, tm, LANES), lambda bi, i: (bi, i, 0))],
        out_shape=[jax.ShapeDtypeStruct((b, s, d), F32), jax.ShapeDtypeStruct((b, s, d), BF16),
                   jax.ShapeDtypeStruct((b, s, LANES), F32)],
        scratch_shapes=[pltpu.VMEM((tm + 2 * POOL_HALO, d), F32)],
        compiler_params=_params(("parallel", "parallel")),
        name="mix",
    )(u, u, u, o, gp, ga, x, wmix, ps, wpo, wao, wo, gffn, wr)


PREFIX_BLOCK = 256
BF16_ROWS = 16
MIN_NORMAL_BITS = 0x00800000


def _prefix_rows(mask, tri):
    out = []
    carries = [jnp.zeros((1, mask.shape[1]), F32)]
    for blk in range(mask.shape[0] // PREFIX_BLOCK):
        m = mask[blk * PREFIX_BLOCK:(blk + 1) * PREFIX_BLOCK]
        p = _dot(tri, m.astype(BF16)) + carries[-1]
        carries.append(p[PREFIX_BLOCK - 1:PREFIX_BLOCK])
        out.append(p)
    return jnp.concatenate(out, axis=0), jnp.concatenate(carries, axis=0)


def _route_kernel(cap, lg_ref, slot_ref, aff_ref, start_ref):
    lg = lg_ref[0]
    lane = lax.broadcasted_iota(I32, lg.shape, 1)
    valid = lane < N_EXPERTS
    lg = jnp.where(valid, lg, -jnp.inf)
    ex = jnp.exp(lg - jnp.max(lg, axis=-1, keepdims=True))
    aff = ex / jnp.sum(ex, axis=-1, keepdims=True)
    aff_ref[0] = aff
    aff_e = aff.T[:N_EXPERTS, :]

    def step(k, ans):
        cand = ans | jnp.left_shift(jnp.int32(1), 30 - k)
        cnt = jnp.sum(jnp.where(aff_e >= lax.bitcast_convert_type(cand, F32), 1.0, 0.0), axis=1, keepdims=True)
        return jnp.where(cnt >= cap, cand, ans)

    thr_e = lax.fori_loop(0, 31, step, jnp.zeros((N_EXPERTS, 1), I32))
    thr_sq = jnp.concatenate([jnp.broadcast_to(thr_e, (N_EXPERTS, LANES)),
                              jnp.zeros((LANES - N_EXPERTS, LANES), I32)], axis=0)
    thr = thr_sq.T[0:1, :]
    thr = jnp.where(thr < MIN_NORMAL_BITS, 0, thr)
    nxt = jnp.where(thr == 0, MIN_NORMAL_BITS, thr + 1)
    lo = lax.bitcast_convert_type(thr, F32)
    hi = lax.bitcast_convert_type(nxt, F32)
    gt = jnp.where(aff >= hi, 1.0, 0.0)
    eq = jnp.where((aff >= lo) & (aff < hi), 1.0, 0.0)
    r, c = lax.broadcasted_iota(I32, (PREFIX_BLOCK, PREFIX_BLOCK), 0), lax.broadcasted_iota(I32, (PREFIX_BLOCK, PREFIX_BLOCK), 1)
    tri = jnp.where(c <= r, 1.0, 0.0).astype(BF16)
    room = cap - jnp.sum(gt, axis=0, keepdims=True)
    eq_before = _prefix_rows(eq, tri)[0] - eq
    sel = jnp.maximum(gt, jnp.where(eq_before < room, eq, 0.0))
    chosen_upto, block_start = _prefix_rows(sel, tri)
    slot_ref[0] = jnp.where(valid & (sel > 0.0), chosen_upto - 1.0, -1.0).astype(I32)
    start_ref[0] = block_start.astype(I32)


def _route(logits, cap):
    b, s, _ = logits.shape
    n_starts = s // PREFIX_BLOCK + 1
    spec = pl.BlockSpec((1, s, LANES), lambda bi: (bi, 0, 0))
    return pl.pallas_call(
        functools.partial(_route_kernel, cap),
        grid=(b,),
        in_specs=[spec],
        out_specs=[spec, spec, pl.BlockSpec((1, n_starts, LANES), lambda bi: (bi, 0, 0))],
        out_shape=[jax.ShapeDtypeStruct((b, s, LANES), I32), jax.ShapeDtypeStruct((b, s, LANES), F32),
                   jax.ShapeDtypeStruct((b, n_starts, LANES), I32)],
        compiler_params=_params(("parallel",)),
        name="route",
    )(logits)


def _slot_window(start_ref, base, cap):
    first = start_ref[base]
    end = start_ref[base + N_EXPERTS]
    w0 = jnp.minimum((first // BF16_ROWS) * BF16_ROWS, cap - PREFIX_BLOCK)
    return w0, end - w0 <= PREFIX_BLOCK


def _gather_kernel(cap, start_ref, slot_ref, h_ref, xs_ref, acc_ref):
    b = pl.program_id(0)
    e = pl.program_id(1)
    sid = slot_ref[0, 0]
    s = sid.shape[1]
    n_blk = s // PREFIX_BLOCK
    windows = [_slot_window(start_ref, (b * (n_blk + 1) + j) * N_EXPERTS + e, cap) for j in range(n_blk)]
    fits = functools.reduce(jnp.logical_and, [ok for _, ok in windows])

    @pl.when(fits)
    def _():
        acc_ref[...] = jnp.zeros_like(acc_ref)
        rows = lax.broadcasted_iota(I32, (PREFIX_BLOCK, PREFIX_BLOCK), 0)
        for j, (w0, _) in enumerate(windows):
            tok = slice(j * PREFIX_BLOCK, (j + 1) * PREFIX_BLOCK)
            hit = (sid[:, tok] - w0) == rows
            win = pl.ds(pl.multiple_of(w0, BF16_ROWS), PREFIX_BLOCK)
            acc_ref[win, :] += _dot(jnp.where(hit, 1.0, 0.0).astype(BF16), h_ref[0, tok, :])
        xs_ref[0] = acc_ref[...].astype(BF16)

    @pl.when(jnp.logical_not(fits))
    def _():
        hit = sid == lax.broadcasted_iota(I32, (cap, s), 0)
        xs_ref[0] = _dot(jnp.where(hit, 1.0, 0.0).astype(BF16), h_ref[0]).astype(BF16)


def _gather(starts, slot_row, hn, cap):
    b, s, d = hn.shape
    return pl.pallas_call(
        functools.partial(_gather_kernel, cap),
        grid_spec=pltpu.PrefetchScalarGridSpec(
            num_scalar_prefetch=1,
            grid=(b, N_EXPERTS),
            in_specs=[pl.BlockSpec((1, 1, 1, s), lambda bi, e, st: (bi, e, 0, 0)),
                      pl.BlockSpec((1, s, d), lambda bi, e, st: (bi, 0, 0))],
            out_specs=pl.BlockSpec((1, cap, d), lambda bi, e, st: (e, bi, 0)),
            scratch_shapes=[pltpu.VMEM((cap, d), F32)]),
        out_shape=jax.ShapeDtypeStruct((N_EXPERTS, b * cap, d), BF16),
        compiler_params=_params(("arbitrary", "arbitrary")),
        name="gather",
    )(starts, slot_row, hn)


FFN_ROW_GROUPS = 2


def _ffn_kernel(xs_ref, wg_ref, wu_ref, wd_ref, y_ref, acc_ref):
    f = pl.program_id(1)

    @pl.when(f == 0)
    def _():
        acc_ref[...] = jnp.zeros_like(acc_ref)

    wg, wu, wd = wg_ref[0].astype(BF16), wu_ref[0].astype(BF16), wd_ref[0].astype(BF16)
    rows = xs_ref.shape[1] // FFN_ROW_GROUPS
    for r in range(FFN_ROW_GROUPS):
        sl = slice(r * rows, (r + 1) * rows)
        xs = xs_ref[0, sl, :]
        g = _dot(xs, wg)
        u = _dot(xs, wu)
        act = (g / (1.0 + jnp.exp(-g)) * u).astype(BF16)
        acc_ref[sl, :] += _dot(act, wd)

    @pl.when(f == pl.num_programs(1) - 1)
    def _():
        y_ref[0] = acc_ref[...].astype(BF16)


def _ffn(xs, w_gate, w_up, w_down, tf=256):
    e, m, d = xs.shape
    ff = w_gate.shape[2]
    row = lambda ei, f: (ei, 0, 0)
    return pl.pallas_call(
        _ffn_kernel,
        grid=(e, ff // tf),
        in_specs=[pl.BlockSpec((1, m, d), row),
                  pl.BlockSpec((1, d, tf), lambda ei, f: (ei, 0, f)),
                  pl.BlockSpec((1, d, tf), lambda ei, f: (ei, 0, f)),
                  pl.BlockSpec((1, tf, d), lambda ei, f: (ei, f, 0))],
        out_specs=pl.BlockSpec((1, m, d), row),
        out_shape=jax.ShapeDtypeStruct((e, m, d), BF16),
        scratch_shapes=[pltpu.VMEM((m, d), F32)],
        compiler_params=_params(("parallel", "arbitrary")),
        name="ffn",
    )(xs, w_gate, w_up, w_down)


def _combine_kernel(cap, start_ref, x1_ref, slot_ref, aff_ref, y_ref, g_ref, o_ref):
    b = pl.program_id(0)
    i = pl.program_id(1)
    slot = slot_ref[0]
    aff = aff_ref[0]
    base = (b * (pl.num_programs(1) + 1) + i) * N_EXPERTS
    windows = [_slot_window(start_ref, base + e, cap) for e in range(N_EXPERTS)]
    fits = functools.reduce(jnp.logical_and, [ok for _, ok in windows])

    def finish(acc):
        o_ref[0] = _rms(acc, g_ref[...])

    @pl.when(fits)
    def _():
        cidx = lax.broadcasted_iota(I32, (PREFIX_BLOCK, PREFIX_BLOCK), 1)
        acc = x1_ref[0]
        for e, (w0, _) in enumerate(windows):
            hit = (slot[:, e:e + 1] - w0) == cidx
            rows = y_ref[e, pl.ds(pl.multiple_of(w0, BF16_ROWS), PREFIX_BLOCK), :]
            acc = acc + aff[:, e:e + 1] * _dot(jnp.where(hit, 1.0, 0.0).astype(BF16), rows)
        finish(acc)

    @pl.when(jnp.logical_not(fits))
    def _():
        cidx = lax.broadcasted_iota(I32, (PREFIX_BLOCK, cap), 1)
        acc = x1_ref[0]
        for e in range(N_EXPERTS):
            hit = slot[:, e:e + 1] == cidx
            acc = acc + aff[:, e:e + 1] * _dot(jnp.where(hit, 1.0, 0.0).astype(BF16), y_ref[e])
        finish(acc)


def _combine(starts, x1, slot_t, aff_t, y, g_final, cap):
    b, s, d = x1.shape
    tile = lambda bi, i, st: (bi, i, 0)
    return pl.pallas_call(
        functools.partial(_combine_kernel, cap),
        grid_spec=pltpu.PrefetchScalarGridSpec(
            num_scalar_prefetch=1,
            grid=(b, s // PREFIX_BLOCK),
            in_specs=[pl.BlockSpec((1, PREFIX_BLOCK, d), tile), pl.BlockSpec((1, PREFIX_BLOCK, LANES), tile),
                      pl.BlockSpec((1, PREFIX_BLOCK, LANES), tile),
                      pl.BlockSpec((N_EXPERTS, cap, d), lambda bi, i, st: (0, bi, 0)),
                      pl.BlockSpec((1, d), lambda bi, i, st: (0, 0))],
            out_specs=pl.BlockSpec((1, PREFIX_BLOCK, d), tile)),
        out_shape=jax.ShapeDtypeStruct((b, s, d), F32),
        compiler_params=_params(("arbitrary", "arbitrary")),
        name="combine",
    )(starts, x1, slot_t, aff_t, y, g_final)


def _layer(x, positions, g_mix, w_in, w_pool_mix, pool_scale, w_pool_out, lam_q1, lam_k1, lam_q2,
           lam_k2, g_subln, w_attn_out, w_out, g_ffn, w_router, w_gate, w_up, w_down, lam_init):
    b, s, d = x.shape
    t = b * s
    cap = EC_CAPACITY_FACTOR * s // N_EXPERTS

    pos_row = positions.reshape(1, t).astype(F32)
    inv = (ROPE_THETA ** (-jnp.arange(ROT_HALF, dtype=F32) * 2.0 / ROT_DIM)).reshape(ROT_HALF, 1)

    u, q, k1, k2, v, gp, ga = _inproj(x.reshape(t, d), pos_row, g_mix.reshape(1, d), inv, w_in.astype(BF16))
    sh = lambda a: a.reshape(b, s, d)

    lamv = jnp.zeros((8, LANES), F32).at[0:4, 0:HEAD_DIM].set(jnp.stack([lam_q1, lam_k1, lam_q2, lam_k2]).astype(F32))
    vt = jnp.transpose(sh(v), (0, 2, 1))
    o = _attention(sh(q), sh(k1), sh(k2), vt, lamv, g_subln.reshape(1, HEAD_WIDTH), lam_init)

    wr = jnp.zeros((d, LANES), BF16).at[:, :N_EXPERTS].set(w_router.astype(BF16))
    x1, hn, logits = _mix(sh(u), o, sh(gp), sh(ga), x, w_pool_mix.astype(BF16), pool_scale.reshape(1, d),
                          w_pool_out.astype(BF16), w_attn_out.astype(BF16), w_out.astype(BF16),
                          g_ffn.reshape(1, d), wr)

    slot_t, aff_t, starts = _route(logits, cap)
    starts = starts[:, :, :N_EXPERTS].reshape(-1)
    slot_row = jnp.transpose(slot_t[:, :, :N_EXPERTS], (0, 2, 1)).reshape(b, N_EXPERTS, 1, s)
    xs = _gather(starts, slot_row, hn, cap)
    y = _ffn(xs, w_gate, w_up, w_down)
    return x1, starts, slot_t, aff_t, y, cap


def kernel(x, positions, g_mix, w_in, w_pool_mix, pool_scale, w_pool_out, lam_q1, lam_k1, lam_q2, lam_k2,
           g_subln, w_attn_out, w_out, g_ffn, w_router, w_gate, w_up, w_down, g_final):
    depth = g_mix.shape[0]
    assert depth == 1, "the final norm is fused into the last layer's combine step"
    l = 0
    lam_init = 0.8 - 0.6 * math.exp(-0.3 * l)
    x1, starts, slot_t, aff_t, y, cap = _layer(
        x, positions, g_mix[l], w_in[l], w_pool_mix[l], pool_scale[l], w_pool_out[l], lam_q1[l], lam_k1[l],
        lam_q2[l], lam_k2[l], g_subln[l], w_attn_out[l], w_out[l], g_ffn[l], w_router[l], w_gate[l], w_up[l],
        w_down[l], lam_init)
    return _combine(starts, x1, slot_t, aff_t, y, g_final.reshape(1, D_MODEL), cap)
```

```python
import functools
import math

import jax
import jax.numpy as jnp
from jax import lax
from jax.experimental import pallas as pl
from jax.experimental.pallas import tpu as pltpu

F32 = jnp.float32
BF16 = jnp.bfloat16
I32 = jnp.int32

D_MODEL = 1024
N_HEADS = 8
HEAD_DIM = 64
HEAD_WIDTH = 2 * HEAD_DIM
ROT_DIM = HEAD_DIM // 4
ROT_HALF = ROT_DIM // 2
ROPE_THETA = 500000.0
POOL_WINDOWS = (2, 4, 8, 16)
POOL_GROUP_DIM = D_MODEL // len(POOL_WINDOWS)
POOL_HALO = 16
N_EXPERTS = 16
EC_CAPACITY_FACTOR = 2
RMS_EPS = 1e-6
Q_SCALE = HEAD_DIM ** -0.5 * math.log2(math.e)
LANES = 128

VMEM_LIMIT = 56 * 1024 * 1024


def _params(sem, fuse_inputs=None):
    return pltpu.CompilerParams(dimension_semantics=sem, vmem_limit_bytes=VMEM_LIMIT,
                                allow_input_fusion=fuse_inputs)


def _rms(x, g):
    return (x * lax.rsqrt(jnp.mean(x * x, axis=-1, keepdims=True) + RMS_EPS)) * g


def _dot(a, b):
    return jnp.dot(a, b, preferred_element_type=F32)


def _rotary(z, cos, sin, first_half):
    up = pltpu.roll(z, LANES - ROT_HALF, 1)
    down = pltpu.roll(z, ROT_HALF, 1)
    partner = jnp.where(first_half, -up, down)
    return z * cos + partner * sin


def _inproj_kernel(x_ref, pos_ref, g_ref, inv_ref, w_ref,
                   u_ref, q_ref, k1_ref, k2_ref, v_ref, gp_ref, ga_ref):
    hb = _rms(x_ref[...], g_ref[...]).astype(BF16)
    ang = inv_ref[...] * pos_ref[...]
    cos_f, sin_f = jnp.cos(ang), jnp.sin(ang)
    tm = ang.shape[1]
    rest = HEAD_DIM - ROT_DIM
    cos = jnp.concatenate([cos_f, cos_f, jnp.ones((rest, tm), F32)] * 2, axis=0).T
    sin = jnp.concatenate([sin_f, sin_f, jnp.zeros((rest, tm), F32)] * 2, axis=0).T
    lane = lax.broadcasted_iota(I32, cos.shape, 1)
    first_half = (lane % HEAD_DIM) < ROT_HALF
    map1 = lane < HEAD_DIM

    def col(j):
        return _dot(hb, w_ref[:, j * D_MODEL:(j + 1) * D_MODEL])

    u_ref[...] = col(0)
    zq = col(1)
    zk = col(2)
    for h in range(N_HEADS):
        sl = slice(h * HEAD_WIDTH, (h + 1) * HEAD_WIDTH)
        q_ref[:, sl] = (_rotary(zq[:, sl], cos, sin, first_half) * Q_SCALE).astype(BF16)
        kr = _rotary(zk[:, sl], cos, sin, first_half)
        k1_ref[:, sl] = jnp.where(map1, kr, 0.0).astype(BF16)
        k2_ref[:, sl] = jnp.where(map1, 0.0, kr).astype(BF16)
    v_ref[...] = col(3).astype(BF16)
    gp_ref[...] = 1.0 / (1.0 + jnp.exp(-col(4)))
    ga_ref[...] = 1.0 / (1.0 + jnp.exp(-col(5)))


def _inproj(x2d, pos_row, g_mix, inv_freq, w_in_bf, tm=256):
    t = x2d.shape[0]
    n_cols = w_in_bf.shape[1]
    row = lambda i: (i, 0)
    const = lambda i: (0, 0)
    tile = pl.BlockSpec((tm, D_MODEL), row)
    out_shapes = [jax.ShapeDtypeStruct((t, D_MODEL), dt) for dt in (F32, BF16, BF16, BF16, BF16, F32, F32)]
    return pl.pallas_call(
        _inproj_kernel,
        grid=(t // tm,),
        in_specs=[tile, pl.BlockSpec((1, tm), lambda i: (0, i)), pl.BlockSpec((1, D_MODEL), const),
                  pl.BlockSpec((ROT_HALF, 1), const), pl.BlockSpec((D_MODEL, n_cols), const)],
        out_specs=[tile] * 7,
        out_shape=out_shapes,
        compiler_params=_params(("parallel",), [False, False, False, False, True]),
        name="inproj",
    )(x2d, pos_row, g_mix, inv_freq, w_in_bf)


SCORE_SLOTS = 3
SUM_ROWS = 16

def _attn_kernel(lam_init, tk, q_ref, k1_ref, k2_ref, vt_ref, lamv_ref, g_ref, o_ref, st_ref, acc_ref):
    lv = lamv_ref[...]
    lam = (jnp.exp(jnp.sum(lv[0:1] * lv[1:2], axis=-1, keepdims=True))
           - jnp.exp(jnp.sum(lv[2:3] * lv[3:4], axis=-1, keepdims=True)) + lam_init)
    q = q_ref[0]
    tq = q.shape[0]
    n_chunks = k1_ref.shape[1] // tk
    nt = (((1,), (1,)), ((), ()))
    k_refs = (k1_ref, k2_ref)

    def scores(c):
        for mp in range(2):
            st_ref[c % SCORE_SLOTS, mp] = lax.dot_general(k_refs[mp][0, c * tk:(c + 1) * tk, :], q, nt,
                                                          preferred_element_type=F32)

    ones_rows = jnp.ones((SUM_ROWS, tk), BF16)

    def accumulate(c, m):
        out_m = []
        for mp in range(2):
            st = st_ref[c % SCORE_SLOTS, mp]
            m_new = jnp.maximum(m[mp], jnp.max(st, axis=0, keepdims=True))
            alpha = jnp.exp2(m[mp] - m_new)
            et = jnp.exp2(st - m_new).astype(BF16)
            vt_ext = jnp.concatenate([vt_ref[0, :, c * tk:(c + 1) * tk], ones_rows], axis=0)
            pv = _dot(vt_ext, et)
            acc_ref[mp] = pv if c == 0 else acc_ref[mp] * alpha + pv
            out_m.append(m_new)
        return out_m

    m = [jnp.full((1, tq), -jnp.inf, F32)] * 2
    for c in range(min(SCORE_SLOTS - 1, n_chunks)):
        scores(c)
    for c in range(n_chunks):
        if c + SCORE_SLOTS - 1 < n_chunks:
            scores(c + SCORE_SLOTS - 1)
        m = accumulate(c, m)
    a1, a2 = acc_ref[0], acc_ref[1]
    r1 = 1.0 / a1[HEAD_WIDTH:HEAD_WIDTH + 1]
    r2 = lam / a2[HEAD_WIDTH:HEAD_WIDTH + 1]
    ot = a1[:HEAD_WIDTH] * r1 - a2[:HEAD_WIDTH] * r2
    o_ref[0] = (_rms(ot.T, g_ref[...]) * (1.0 - lam_init)).astype(BF16)


def _attention(q, k1, k2, vt, lamv, g_subln, lam_init, tq=1024, tk=256):
    b, s, _ = q.shape
    qspec = pl.BlockSpec((1, tq, HEAD_WIDTH), lambda bi, h, qi: (bi, qi, h))
    kspec = pl.BlockSpec((1, s, HEAD_WIDTH), lambda bi, h, qi: (bi, 0, h))
    vspec = pl.BlockSpec((1, HEAD_WIDTH, s), lambda bi, h, qi: (bi, h, 0))
    const = lambda bi, h, qi: (0, 0)
    return pl.pallas_call(
        functools.partial(_attn_kernel, lam_init, tk),
        grid=(b, N_HEADS, s // tq),
        in_specs=[qspec, kspec, kspec, vspec,
                  pl.BlockSpec((8, LANES), const), pl.BlockSpec((1, HEAD_WIDTH), const)],
        out_specs=qspec,
        out_shape=jax.ShapeDtypeStruct(k1.shape, BF16),
        scratch_shapes=[pltpu.VMEM((SCORE_SLOTS, 2, tk, tq), F32), pltpu.VMEM((2, HEAD_WIDTH + SUM_ROWS, tq), F32)],
        compiler_params=_params(("parallel", "parallel", "arbitrary")),
        name="attn",
    )(q, k1, k2, vt, lamv, g_subln)


def _mix_kernel(seq, u_ref, up_ref, un_ref, o_ref, gp_ref, ga_ref, x_ref,
                wmix_ref, ps_ref, wpo_ref, wao_ref, wo_ref, gffn_ref, wr_ref,
                x1_ref, hn_ref, lg_ref, upad_ref):
    tm = u_ref.shape[1]
    i = pl.program_id(1)
    nt = pl.num_programs(1)
    main = u_ref[0]
    upad_ref[0:POOL_HALO, :] = jnp.where(i > 0, up_ref[0], 0.0)
    upad_ref[POOL_HALO:POOL_HALO + tm, :] = main
    upad_ref[POOL_HALO + tm:, :] = jnp.where(i < nt - 1, un_ref[0], 0.0)
    t = i * tm + lax.broadcasted_iota(I32, (tm, 1), 0)
    mixed = []
    for g, w in enumerate(POOL_WINDOWS):
        cols = slice(g * POOL_GROUP_DIM, (g + 1) * POOL_GROUP_DIM)
        win = upad_ref[POOL_HALO - w // 2:POOL_HALO - w // 2 + tm, cols]
        for d in range(-w // 2 + 1, w // 2):
            win = win + upad_ref[POOL_HALO + d:POOL_HALO + d + tm, cols]
        cnt = (jnp.minimum(t + w // 2, seq) - jnp.maximum(t - w // 2, 0)).astype(F32)
        pooled = win / cnt - main[:, cols]
        mixed.append((_dot(pooled.astype(BF16), wmix_ref[g]) * ps_ref[:, cols]).astype(BF16))
    y_pool = _dot(jnp.concatenate(mixed, axis=-1), wpo_ref[...])
    y_attn = _dot(o_ref[0], wao_ref[...])
    merged = gp_ref[0] * y_pool + ga_ref[0] * y_attn
    x1 = x_ref[0] + _dot(merged.astype(BF16), wo_ref[...])
    x1_ref[0] = x1
    hn = _rms(x1, gffn_ref[...]).astype(BF16)
    hn_ref[0] = hn
    lg_ref[0] = _dot(hn, wr_ref[...])


def _mix(u, o, gp, ga, x, wmix, ps, wpo, wao, wo, gffn, wr, tm=512):
    b, s, d = x.shape
    hb = tm // POOL_HALO
    n_halo = s // POOL_HALO
    tile = pl.BlockSpec((1, tm, d), lambda bi, i: (bi, i, 0))
    prev = pl.BlockSpec((1, POOL_HALO, d), lambda bi, i: (bi, jnp.maximum(i * hb - 1, 0), 0))
    nxt = pl.BlockSpec((1, POOL_HALO, d), lambda bi, i: (bi, jnp.minimum((i + 1) * hb, n_halo - 1), 0))
    c2 = lambda bi, i: (0, 0)
    c3 = lambda bi, i: (0, 0, 0)
    return pl.pallas_call(
        functools.partial(_mix_kernel, s),
        grid=(b, s // tm),
        in_specs=[tile, prev, nxt, tile, tile, tile, tile,
                  pl.BlockSpec(wmix.shape, c3), pl.BlockSpec((1, d), c2),
                  pl.BlockSpec((d, d), c2), pl.BlockSpec((d, d), c2), pl.BlockSpec((d, d), c2),
                  pl.BlockSpec((1, d), c2), pl.BlockSpec((d, LANES), c2)],
        out_specs=[tile, tile, pl.BlockSpec((1, tm, LANES), lambda bi, i: (bi, i, 0))],
        out_shape=[jax.ShapeDtypeStruct((b, s, d), F32), jax.ShapeDtypeStruct((b, s, d), BF16),
                   jax.ShapeDtypeStruct((b, s, LANES), F32)],
        scratch_shapes=[pltpu.VMEM((tm + 2 * POOL_HALO, d), F32)],
        compiler_params=_params(("parallel", "parallel"), [False] * 7 + [True, False, True, True, True, False, True]),
        name="mix",
    )(u, u, u, o, gp, ga, x, wmix, ps, wpo, wao, wo, gffn, wr)


PREFIX_BLOCK = 256
BF16_ROWS = 16
MIN_NORMAL_BITS = 0x00800000


def _prefix_rows(mask, tri):
    out = []
    carries = [jnp.zeros((1, mask.shape[1]), F32)]
    for blk in range(mask.shape[0] // PREFIX_BLOCK):
        m = mask[blk * PREFIX_BLOCK:(blk + 1) * PREFIX_BLOCK]
        p = _dot(tri, m.astype(BF16)) + carries[-1]
        carries.append(p[PREFIX_BLOCK - 1:PREFIX_BLOCK])
        out.append(p)
    return jnp.concatenate(out, axis=0), jnp.concatenate(carries, axis=0)


def _route_kernel(cap, lg_ref, slot_ref, aff_ref, start_ref):
    lg = lg_ref[0]
    lane = lax.broadcasted_iota(I32, lg.shape, 1)
    valid = lane < N_EXPERTS
    lg = jnp.where(valid, lg, -jnp.inf)
    ex = jnp.exp(lg - jnp.max(lg, axis=-1, keepdims=True))
    aff = ex / jnp.sum(ex, axis=-1, keepdims=True)
    aff_ref[0] = aff
    aff_e = aff.T[:N_EXPERTS, :]

    def step(k, ans):
        cand = ans | jnp.left_shift(jnp.int32(1), 30 - k)
        cnt = jnp.sum(jnp.where(aff_e >= lax.bitcast_convert_type(cand, F32), 1.0, 0.0), axis=1, keepdims=True)
        return jnp.where(cnt >= cap, cand, ans)

    thr_e = lax.fori_loop(0, 31, step, jnp.zeros((N_EXPERTS, 1), I32))
    thr_sq = jnp.concatenate([jnp.broadcast_to(thr_e, (N_EXPERTS, LANES)),
                              jnp.zeros((LANES - N_EXPERTS, LANES), I32)], axis=0)
    thr = thr_sq.T[0:1, :]
    thr = jnp.where(thr < MIN_NORMAL_BITS, 0, thr)
    nxt = jnp.where(thr == 0, MIN_NORMAL_BITS, thr + 1)
    lo = lax.bitcast_convert_type(thr, F32)
    hi = lax.bitcast_convert_type(nxt, F32)
    gt = jnp.where(aff >= hi, 1.0, 0.0)
    eq = jnp.where((aff >= lo) & (aff < hi), 1.0, 0.0)
    r, c = lax.broadcasted_iota(I32, (PREFIX_BLOCK, PREFIX_BLOCK), 0), lax.broadcasted_iota(I32, (PREFIX_BLOCK, PREFIX_BLOCK), 1)
    tri = jnp.where(c <= r, 1.0, 0.0).astype(BF16)
    room = cap - jnp.sum(gt, axis=0, keepdims=True)
    eq_before = _prefix_rows(eq, tri)[0] - eq
    sel = jnp.maximum(gt, jnp.where(eq_before < room, eq, 0.0))
    chosen_upto, block_start = _prefix_rows(sel, tri)
    slot_ref[0] = jnp.where(valid & (sel > 0.0), chosen_upto - 1.0, -1.0).astype(I32)
    start_ref[0] = block_start.astype(I32)


def _route(logits, cap):
    b, s, _ = logits.shape
    n_starts = s // PREFIX_BLOCK + 1
    spec = pl.BlockSpec((1, s, LANES), lambda bi: (bi, 0, 0))
    return pl.pallas_call(
        functools.partial(_route_kernel, cap),
        grid=(b,),
        in_specs=[spec],
        out_specs=[spec, spec, pl.BlockSpec((1, n_starts, LANES), lambda bi: (bi, 0, 0))],
        out_shape=[jax.ShapeDtypeStruct((b, s, LANES), I32), jax.ShapeDtypeStruct((b, s, LANES), F32),
                   jax.ShapeDtypeStruct((b, n_starts, LANES), I32)],
        compiler_params=_params(("parallel",)),
        name="route",
    )(logits)


def _slot_window(start_ref, base, cap):
    first = start_ref[base]
    end = start_ref[base + N_EXPERTS]
    w0 = jnp.minimum((first // BF16_ROWS) * BF16_ROWS, cap - PREFIX_BLOCK)
    return w0, end - w0 <= PREFIX_BLOCK


def _gather_kernel(cap, start_ref, slot_ref, h_ref, xs_ref, acc_ref):
    b = pl.program_id(0)
    e = pl.program_id(1)
    sid = slot_ref[0, 0]
    s = sid.shape[1]
    n_blk = s // PREFIX_BLOCK
    windows = [_slot_window(start_ref, (b * (n_blk + 1) + j) * N_EXPERTS + e, cap) for j in range(n_blk)]
    fits = functools.reduce(jnp.logical_and, [ok for _, ok in windows])

    @pl.when(fits)
    def _():
        acc_ref[...] = jnp.zeros_like(acc_ref)
        rows = lax.broadcasted_iota(I32, (PREFIX_BLOCK, PREFIX_BLOCK), 0)
        for j, (w0, _) in enumerate(windows):
            tok = slice(j * PREFIX_BLOCK, (j + 1) * PREFIX_BLOCK)
            hit = (sid[:, tok] - w0) == rows
            win = pl.ds(pl.multiple_of(w0, BF16_ROWS), PREFIX_BLOCK)
            acc_ref[win, :] += _dot(jnp.where(hit, 1.0, 0.0).astype(BF16), h_ref[0, tok, :])
        xs_ref[0] = acc_ref[...].astype(BF16)

    @pl.when(jnp.logical_not(fits))
    def _():
        hit = sid == lax.broadcasted_iota(I32, (cap, s), 0)
        xs_ref[0] = _dot(jnp.where(hit, 1.0, 0.0).astype(BF16), h_ref[0]).astype(BF16)


def _gather(starts, slot_row, hn, cap):
    b, s, d = hn.shape
    return pl.pallas_call(
        functools.partial(_gather_kernel, cap),
        grid_spec=pltpu.PrefetchScalarGridSpec(
            num_scalar_prefetch=1,
            grid=(b, N_EXPERTS),
            in_specs=[pl.BlockSpec((1, 1, 1, s), lambda bi, e, st: (bi, e, 0, 0)),
                      pl.BlockSpec((1, s, d), lambda bi, e, st: (bi, 0, 0))],
            out_specs=pl.BlockSpec((1, cap, d), lambda bi, e, st: (e, bi, 0)),
            scratch_shapes=[pltpu.VMEM((cap, d), F32)]),
        out_shape=jax.ShapeDtypeStruct((N_EXPERTS, b * cap, d), BF16),
        compiler_params=_params(("arbitrary", "arbitrary")),
        name="gather",
    )(starts, slot_row, hn)


FFN_ROW_GROUPS = 2


def _ffn_kernel(xs_ref, wg_ref, wu_ref, wd_ref, y_ref, acc_ref):
    f = pl.program_id(1)

    @pl.when(f == 0)
    def _():
        acc_ref[...] = jnp.zeros_like(acc_ref)

    wg, wu, wd = wg_ref[0].astype(BF16), wu_ref[0].astype(BF16), wd_ref[0].astype(BF16)
    rows = xs_ref.shape[1] // FFN_ROW_GROUPS
    for r in range(FFN_ROW_GROUPS):
        sl = slice(r * rows, (r + 1) * rows)
        xs = xs_ref[0, sl, :]
        g = _dot(xs, wg)
        u = _dot(xs, wu)
        act = (g / (1.0 + jnp.exp(-g)) * u).astype(BF16)
        acc_ref[sl, :] += _dot(act, wd)

    @pl.when(f == pl.num_programs(1) - 1)
    def _():
        y_ref[0] = acc_ref[...].astype(BF16)


def _ffn(xs, w_gate, w_up, w_down, tf=256):
    e, m, d = xs.shape
    ff = w_gate.shape[2]
    row = lambda ei, f: (ei, 0, 0)
    return pl.pallas_call(
        _ffn_kernel,
        grid=(e, ff // tf),
        in_specs=[pl.BlockSpec((1, m, d), row),
                  pl.BlockSpec((1, d, tf), lambda ei, f: (ei, 0, f)),
                  pl.BlockSpec((1, d, tf), lambda ei, f: (ei, 0, f)),
                  pl.BlockSpec((1, tf, d), lambda ei, f: (ei, f, 0))],
        out_specs=pl.BlockSpec((1, m, d), row),
        out_shape=jax.ShapeDtypeStruct((e, m, d), BF16),
        scratch_shapes=[pltpu.VMEM((m, d), F32)],
        compiler_params=_params(("parallel", "arbitrary")),
        name="ffn",
    )(xs, w_gate, w_up, w_down)


def _combine_kernel(cap, start_ref, x1_ref, slot_ref, aff_ref, y_ref, g_ref, o_ref):
    b = pl.program_id(0)
    i = pl.program_id(1)
    slot = slot_ref[0]
    aff = aff_ref[0]
    base = (b * (pl.num_programs(1) + 1) + i) * N_EXPERTS
    windows = [_slot_window(start_ref, base + e, cap) for e in range(N_EXPERTS)]
    fits = functools.reduce(jnp.logical_and, [ok for _, ok in windows])

    def finish(acc):
        o_ref[0] = _rms(acc, g_ref[...])

    @pl.when(fits)
    def _():
        cidx = lax.broadcasted_iota(I32, (PREFIX_BLOCK, PREFIX_BLOCK), 1)
        acc = x1_ref[0]
        for e, (w0, _) in enumerate(windows):
            hit = (slot[:, e:e + 1] - w0) == cidx
            rows = y_ref[e, pl.ds(pl.multiple_of(w0, BF16_ROWS), PREFIX_BLOCK), :]
            acc = acc + aff[:, e:e + 1] * _dot(jnp.where(hit, 1.0, 0.0).astype(BF16), rows)
        finish(acc)

    @pl.when(jnp.logical_not(fits))
    def _():
        cidx = lax.broadcasted_iota(I32, (PREFIX_BLOCK, cap), 1)
        acc = x1_ref[0]
        for e in range(N_EXPERTS):
            hit = slot[:, e:e + 1] == cidx
            acc = acc + aff[:, e:e + 1] * _dot(jnp.where(hit, 1.0, 0.0).astype(BF16), y_ref[e])
        finish(acc)


def _combine(starts, x1, slot_t, aff_t, y, g_final, cap):
    b, s, d = x1.shape
    tile = lambda bi, i, st: (bi, i, 0)
    return pl.pallas_call(
        functools.partial(_combine_kernel, cap),
        grid_spec=pltpu.PrefetchScalarGridSpec(
            num_scalar_prefetch=1,
            grid=(b, s // PREFIX_BLOCK),
            in_specs=[pl.BlockSpec((1, PREFIX_BLOCK, d), tile), pl.BlockSpec((1, PREFIX_BLOCK, LANES), tile),
                      pl.BlockSpec((1, PREFIX_BLOCK, LANES), tile),
                      pl.BlockSpec((N_EXPERTS, cap, d), lambda bi, i, st: (0, bi, 0)),
                      pl.BlockSpec((1, d), lambda bi, i, st: (0, 0))],
            out_specs=pl.BlockSpec((1, PREFIX_BLOCK, d), tile)),
        out_shape=jax.ShapeDtypeStruct((b, s, d), F32),
        compiler_params=_params(("arbitrary", "arbitrary")),
        name="combine",
    )(starts, x1, slot_t, aff_t, y, g_final)


def _layer(x, positions, g_mix, w_in, w_pool_mix, pool_scale, w_pool_out, lam_q1, lam_k1, lam_q2,
           lam_k2, g_subln, w_attn_out, w_out, g_ffn, w_router, w_gate, w_up, w_down, lam_init):
    b, s, d = x.shape
    t = b * s
    cap = EC_CAPACITY_FACTOR * s // N_EXPERTS

    pos_row = positions.reshape(1, t).astype(F32)
    inv = (ROPE_THETA ** (-jnp.arange(ROT_HALF, dtype=F32) * 2.0 / ROT_DIM)).reshape(ROT_HALF, 1)

    u, q, k1, k2, v, gp, ga = _inproj(x.reshape(t, d), pos_row, g_mix.reshape(1, d), inv, w_in.astype(BF16))
    sh = lambda a: a.reshape(b, s, d)

    lamv = jnp.zeros((8, LANES), F32).at[0:4, 0:HEAD_DIM].set(jnp.stack([lam_q1, lam_k1, lam_q2, lam_k2]).astype(F32))
    vt = jnp.transpose(sh(v), (0, 2, 1))
    o = _attention(sh(q), sh(k1), sh(k2), vt, lamv, g_subln.reshape(1, HEAD_WIDTH), lam_init)

    wr = jnp.zeros((d, LANES), BF16).at[:, :N_EXPERTS].set(w_router.astype(BF16))
    x1, hn, logits = _mix(sh(u), o, sh(gp), sh(ga), x, w_pool_mix.astype(BF16), pool_scale.reshape(1, d),
                          w_pool_out.astype(BF16), w_attn_out.astype(BF16), w_out.astype(BF16),
                          g_ffn.reshape(1, d), wr)

    slot_t, aff_t, starts = _route(logits, cap)
    starts = starts[:, :, :N_EXPERTS].reshape(-1)
    slot_row = jnp.transpose(slot_t[:, :, :N_EXPERTS], (0, 2, 1)).reshape(b, N_EXPERTS, 1, s)
    xs = _gather(starts, slot_row, hn, cap)
    y = _ffn(xs, w_gate, w_up, w_down)
    return x1, starts, slot_t, aff_t, y, cap


def kernel(x, positions, g_mix, w_in, w_pool_mix, pool_scale, w_pool_out, lam_q1, lam_k1, lam_q2, lam_k2,
           g_subln, w_attn_out, w_out, g_ffn, w_router, w_gate, w_up, w_down, g_final):
    depth = g_mix.shape[0]
    assert depth == 1, "the final norm is fused into the last layer's combine step"
    l = 0
    lam_init = 0.8 - 0.6 * math.exp(-0.3 * l)
    x1, starts, slot_t, aff_t, y, cap = _layer(
        x, positions, g_mix[l], w_in[l], w_pool_mix[l], pool_scale[l], w_pool_out[l], lam_q1[l], lam_k1[l],
        lam_q2[l], lam_k2[l], g_subln[l], w_attn_out[l], w_out[l], g_ffn[l], w_router[l], w_gate[l], w_up[l],
        w_down[l], lam_init)
    return _combine(starts, x1, slot_t, aff_t, y, g_final.reshape(1, D_MODEL), cap)
```

```python
import functools
import math

import jax
import jax.numpy as jnp
from jax import lax
from jax.experimental import pallas as pl
from jax.experimental.pallas import tpu as pltpu

F32 = jnp.float32
BF16 = jnp.bfloat16
I32 = jnp.int32

D_MODEL = 1024
N_HEADS = 8
HEAD_DIM = 64
HEAD_WIDTH = 2 * HEAD_DIM
ROT_DIM = HEAD_DIM // 4
ROT_HALF = ROT_DIM // 2
ROPE_THETA = 500000.0
POOL_WINDOWS = (2, 4, 8, 16)
POOL_GROUP_DIM = D_MODEL // len(POOL_WINDOWS)
POOL_HALO = 16
N_EXPERTS = 16
EC_CAPACITY_FACTOR = 2
RMS_EPS = 1e-6
Q_SCALE = HEAD_DIM ** -0.5 * math.log2(math.e)
LANES = 128
SUBLANES = 8

VMEM_LIMIT = 56 * 1024 * 1024


def _params(sem):
    return pltpu.CompilerParams(dimension_semantics=sem, vmem_limit_bytes=VMEM_LIMIT)


def _rms(x, g):
    return (x * lax.rsqrt(jnp.mean(x * x, axis=-1, keepdims=True) + RMS_EPS)) * g


def _dot(a, b):
    return jnp.dot(a, b, preferred_element_type=F32)


def _rotary(z, cos, sin, first_half):
    up = pltpu.roll(z, LANES - ROT_HALF, 1)
    down = pltpu.roll(z, ROT_HALF, 1)
    partner = jnp.where(first_half, -up, down)
    return z * cos + partner * sin


def _inproj_kernel(x_ref, pos_ref, g_ref, inv_ref, w_ref,
                   u_ref, q_ref, k1_ref, k2_ref, v_ref, gp_ref, ga_ref):
    hb = _rms(x_ref[...], g_ref[...]).astype(BF16)
    ang = inv_ref[...] * pos_ref[...]
    cos_f, sin_f = jnp.cos(ang), jnp.sin(ang)
    tm = ang.shape[1]
    rest = HEAD_DIM - ROT_DIM
    cos = jnp.concatenate([cos_f, cos_f, jnp.ones((rest, tm), F32)] * 2, axis=0).T
    sin = jnp.concatenate([sin_f, sin_f, jnp.zeros((rest, tm), F32)] * 2, axis=0).T
    lane = lax.broadcasted_iota(I32, cos.shape, 1)
    first_half = (lane % HEAD_DIM) < ROT_HALF
    map1 = lane < HEAD_DIM

    def col(j):
        return _dot(hb, w_ref[:, j * D_MODEL:(j + 1) * D_MODEL])

    u_ref[...] = col(0)
    zq = col(1)
    zk = col(2)
    for h in range(N_HEADS):
        sl = slice(h * HEAD_WIDTH, (h + 1) * HEAD_WIDTH)
        q_ref[:, sl] = (_rotary(zq[:, sl], cos, sin, first_half) * Q_SCALE).astype(BF16)
        kr = _rotary(zk[:, sl], cos, sin, first_half)
        k1_ref[:, sl] = jnp.where(map1, kr, 0.0).astype(BF16)
        k2_ref[:, sl] = jnp.where(map1, 0.0, kr).astype(BF16)
    v_ref[...] = col(3).astype(BF16)
    gp_ref[...] = 1.0 / (1.0 + jnp.exp(-col(4)))
    ga_ref[...] = 1.0 / (1.0 + jnp.exp(-col(5)))


def _inproj(x2d, pos_row, g_mix, inv_freq, w_in_bf, tm=256):
    t = x2d.shape[0]
    n_cols = w_in_bf.shape[1]
    row = lambda i: (i, 0)
    const = lambda i: (0, 0)
    tile = pl.BlockSpec((tm, D_MODEL), row)
    out_shapes = [jax.ShapeDtypeStruct((t, D_MODEL), dt) for dt in (F32, BF16, BF16, BF16, BF16, F32, F32)]
    return pl.pallas_call(
        _inproj_kernel,
        grid=(t // tm,),
        in_specs=[tile, pl.BlockSpec((1, tm), lambda i: (0, i)), pl.BlockSpec((1, D_MODEL), const),
                  pl.BlockSpec((ROT_HALF, 1), const), pl.BlockSpec((D_MODEL, n_cols), const)],
        out_specs=[tile] * 7,
        out_shape=out_shapes,
        compiler_params=_params(("parallel",)),
        name="inproj",
    )(x2d, pos_row, g_mix, inv_freq, w_in_bf)


SCORE_SLOTS = 3
SUM_ROWS = 16

def _attn_kernel(lam_init, tk, q_ref, k1_ref, k2_ref, vt_ref, lamv_ref, g_ref, o_ref, st_ref, acc_ref):
    lv = lamv_ref[...]
    lam = (jnp.exp(jnp.sum(lv[0:1] * lv[1:2], axis=-1, keepdims=True))
           - jnp.exp(jnp.sum(lv[2:3] * lv[3:4], axis=-1, keepdims=True)) + lam_init)
    q = q_ref[0]
    tq = q.shape[0]
    n_chunks = k1_ref.shape[1] // tk
    nt = (((1,), (1,)), ((), ()))
    k_refs = (k1_ref, k2_ref)

    def scores(c):
        for mp in range(2):
            st_ref[c % SCORE_SLOTS, mp] = lax.dot_general(k_refs[mp][0, c * tk:(c + 1) * tk, :], q, nt,
                                                          preferred_element_type=F32)

    ones_rows = jnp.ones((SUM_ROWS, tk), BF16)

    def accumulate(c, m):
        out_m = []
        for mp in range(2):
            st = st_ref[c % SCORE_SLOTS, mp]
            m_new = jnp.maximum(m[mp], jnp.max(st, axis=0, keepdims=True))
            alpha = jnp.exp2(m[mp] - m_new)
            et = jnp.exp2(st - m_new).astype(BF16)
            vt_ext = jnp.concatenate([vt_ref[0, :, c * tk:(c + 1) * tk], ones_rows], axis=0)
            pv = _dot(vt_ext, et)
            acc_ref[mp] = pv if c == 0 else acc_ref[mp] * alpha + pv
            out_m.append(m_new)
        return out_m

    m = [jnp.full((1, tq), -jnp.inf, F32)] * 2
    for c in range(min(SCORE_SLOTS - 1, n_chunks)):
        scores(c)
    for c in range(n_chunks):
        if c + SCORE_SLOTS - 1 < n_chunks:
            scores(c + SCORE_SLOTS - 1)
        m = accumulate(c, m)
    a1, a2 = acc_ref[0], acc_ref[1]
    r1 = 1.0 / a1[HEAD_WIDTH:HEAD_WIDTH + 1]
    r2 = lam / a2[HEAD_WIDTH:HEAD_WIDTH + 1]
    ot = a1[:HEAD_WIDTH] * r1 - a2[:HEAD_WIDTH] * r2
    o_ref[0] = (_rms(ot.T, g_ref[...]) * (1.0 - lam_init)).astype(BF16)


def _attention(q, k1, k2, vt, lamv, g_subln, lam_init, tq=1024, tk=256):
    b, s, _ = q.shape
    qspec = pl.BlockSpec((1, tq, HEAD_WIDTH), lambda bi, h, qi: (bi, qi, h))
    kspec = pl.BlockSpec((1, s, HEAD_WIDTH), lambda bi, h, qi: (bi, 0, h))
    vspec = pl.BlockSpec((1, HEAD_WIDTH, s), lambda bi, h, qi: (bi, h, 0))
    const = lambda bi, h, qi: (0, 0)
    return pl.pallas_call(
        functools.partial(_attn_kernel, lam_init, tk),
        grid=(b, N_HEADS, s // tq),
        in_specs=[qspec, kspec, kspec, vspec,
                  pl.BlockSpec((SUBLANES, LANES), const), pl.BlockSpec((1, HEAD_WIDTH), const)],
        out_specs=qspec,
        out_shape=jax.ShapeDtypeStruct(k1.shape, BF16),
        scratch_shapes=[pltpu.VMEM((SCORE_SLOTS, 2, tk, tq), F32), pltpu.VMEM((2, HEAD_WIDTH + SUM_ROWS, tq), F32)],
        compiler_params=_params(("parallel", "parallel", "arbitrary")),
        name="attn",
    )(q, k1, k2, vt, lamv, g_subln)


def _mix_kernel(seq, u_ref, up_ref, un_ref, o_ref, gp_ref, ga_ref, x_ref,
                wmix_ref, ps_ref, wpo_ref, wao_ref, wo_ref, gffn_ref, wr_ref,
                x1_ref, hn_ref, lg_ref, upad_ref):
    tm = u_ref.shape[1]
    i = pl.program_id(1)
    nt = pl.num_programs(1)
    main = u_ref[0]
    upad_ref[0:POOL_HALO, :] = jnp.where(i > 0, up_ref[0], 0.0)
    upad_ref[POOL_HALO:POOL_HALO + tm, :] = main
    upad_ref[POOL_HALO + tm:, :] = jnp.where(i < nt - 1, un_ref[0], 0.0)
    t = i * tm + lax.broadcasted_iota(I32, (tm, 1), 0)
    mixed = []
    for g, w in enumerate(POOL_WINDOWS):
        cols = slice(g * POOL_GROUP_DIM, (g + 1) * POOL_GROUP_DIM)
        win = upad_ref[POOL_HALO - w // 2:POOL_HALO - w // 2 + tm, cols]
        for d in range(-w // 2 + 1, w // 2):
            win = win + upad_ref[POOL_HALO + d:POOL_HALO + d + tm, cols]
        cnt = (jnp.minimum(t + w // 2, seq) - jnp.maximum(t - w // 2, 0)).astype(F32)
        pooled = win / cnt - main[:, cols]
        mixed.append((_dot(pooled.astype(BF16), wmix_ref[g]) * ps_ref[:, cols]).astype(BF16))
    y_pool = _dot(jnp.concatenate(mixed, axis=-1), wpo_ref[...])
    y_attn = _dot(o_ref[0], wao_ref[...])
    merged = gp_ref[0] * y_pool + ga_ref[0] * y_attn
    x1 = x_ref[0] + _dot(merged.astype(BF16), wo_ref[...])
    x1_ref[0] = x1
    hn = _rms(x1, gffn_ref[...]).astype(BF16)
    hn_ref[0] = hn
    lg_ref[0] = _dot(hn, wr_ref[...])


def _mix(u, o, gp, ga, x, wmix, ps, wpo, wao, wo, gffn, wr, tm=512):
    b, s, d = x.shape
    hb = tm // POOL_HALO
    n_halo = s // POOL_HALO
    tile = pl.BlockSpec((1, tm, d), lambda bi, i: (bi, i, 0))
    prev = pl.BlockSpec((1, POOL_HALO, d), lambda bi, i: (bi, jnp.maximum(i * hb - 1, 0), 0))
    nxt = pl.BlockSpec((1, POOL_HALO, d), lambda bi, i: (bi, jnp.minimum((i + 1) * hb, n_halo - 1), 0))
    c2 = lambda bi, i: (0, 0)
    c3 = lambda bi, i: (0, 0, 0)
    return pl.pallas_call(
        functools.partial(_mix_kernel, s),
        grid=(b, s // tm),
        in_specs=[tile, prev, nxt, tile, tile, tile, tile,
                  pl.BlockSpec(wmix.shape, c3), pl.BlockSpec((1, d), c2),
                  pl.BlockSpec((d, d), c2), pl.BlockSpec((d, d), c2), pl.BlockSpec((d, d), c2),
                  pl.BlockSpec((1, d), c2), pl.BlockSpec((d, LANES), c2)],
        out_specs=[tile, tile, pl.BlockSpec((1, tm, LANES), lambda bi, i: (bi, i, 0))],
        out_shape=[jax.ShapeDtypeStruct((b, s, d), F32), jax.ShapeDtypeStruct((b, s, d), BF16),
                   jax.ShapeDtypeStruct((b, s, LANES), F32)],
        scratch_shapes=[pltpu.VMEM((tm + 2 * POOL_HALO, d), F32)],
        compiler_params=_params(("parallel", "parallel")),
        name="mix",
    )(u, u, u, o, gp, ga, x, wmix, ps, wpo, wao, wo, gffn, wr)


PREFIX_BLOCK = 256
BF16_ROWS = 16
MIN_NORMAL_BITS = 0x00800000
F32_MAGNITUDE_BITS = 31


def _prefix_rows(mask, tri):
    out = []
    carries = [jnp.zeros((1, mask.shape[1]), F32)]
    for blk in range(mask.shape[0] // PREFIX_BLOCK):
        m = mask[blk * PREFIX_BLOCK:(blk + 1) * PREFIX_BLOCK]
        p = _dot(tri, m.astype(BF16)) + carries[-1]
        carries.append(p[PREFIX_BLOCK - 1:PREFIX_BLOCK])
        out.append(p)
    return jnp.concatenate(out, axis=0), jnp.concatenate(carries, axis=0)


def _route_kernel(cap, lg_ref, slot_ref, aff_ref, start_ref):
    lg = lg_ref[0]
    lane = lax.broadcasted_iota(I32, lg.shape, 1)
    valid = lane < N_EXPERTS
    lg = jnp.where(valid, lg, -jnp.inf)
    ex = jnp.exp(lg - jnp.max(lg, axis=-1, keepdims=True))
    aff = ex / jnp.sum(ex, axis=-1, keepdims=True)
    aff_ref[0] = aff
    aff_e = aff.T[:N_EXPERTS, :]

    def step(k, ans):
        cand = ans | jnp.left_shift(jnp.int32(1), F32_MAGNITUDE_BITS - 1 - k)
        cnt = jnp.sum(jnp.where(aff_e >= lax.bitcast_convert_type(cand, F32), 1.0, 0.0), axis=1, keepdims=True)
        return jnp.where(cnt >= cap, cand, ans)

    thr_e = lax.fori_loop(0, F32_MAGNITUDE_BITS, step, jnp.zeros((N_EXPERTS, 1), I32))
    thr_sq = jnp.concatenate([jnp.broadcast_to(thr_e, (N_EXPERTS, LANES)),
                              jnp.zeros((LANES - N_EXPERTS, LANES), I32)], axis=0)
    thr = thr_sq.T[0:1, :]
    thr = jnp.where(thr < MIN_NORMAL_BITS, 0, thr)
    nxt = jnp.where(thr == 0, MIN_NORMAL_BITS, thr + 1)
    lo = lax.bitcast_convert_type(thr, F32)
    hi = lax.bitcast_convert_type(nxt, F32)
    gt = jnp.where(aff >= hi, 1.0, 0.0)
    eq = jnp.where((aff >= lo) & (aff < hi), 1.0, 0.0)
    r, c = lax.broadcasted_iota(I32, (PREFIX_BLOCK, PREFIX_BLOCK), 0), lax.broadcasted_iota(I32, (PREFIX_BLOCK, PREFIX_BLOCK), 1)
    tri = jnp.where(c <= r, 1.0, 0.0).astype(BF16)
    room = cap - jnp.sum(gt, axis=0, keepdims=True)
    eq_before = _prefix_rows(eq, tri)[0] - eq
    sel = jnp.maximum(gt, jnp.where(eq_before < room, eq, 0.0))
    chosen_upto, block_start = _prefix_rows(sel, tri)
    slot_ref[0] = jnp.where(valid & (sel > 0.0), chosen_upto - 1.0, -1.0).astype(I32)
    start_ref[0] = block_start.astype(I32)


def _route(logits, cap):
    b, s, _ = logits.shape
    n_starts = s // PREFIX_BLOCK + 1
    spec = pl.BlockSpec((1, s, LANES), lambda bi: (bi, 0, 0))
    return pl.pallas_call(
        functools.partial(_route_kernel, cap),
        grid=(b,),
        in_specs=[spec],
        out_specs=[spec, spec, pl.BlockSpec((1, n_starts, LANES), lambda bi: (bi, 0, 0))],
        out_shape=[jax.ShapeDtypeStruct((b, s, LANES), I32), jax.ShapeDtypeStruct((b, s, LANES), F32),
                   jax.ShapeDtypeStruct((b, n_starts, LANES), I32)],
        compiler_params=_params(("parallel",)),
        name="route",
    )(logits)


def _slot_window(start_ref, base, cap):
    first = start_ref[base]
    end = start_ref[base + N_EXPERTS]
    w0 = jnp.minimum((first // BF16_ROWS) * BF16_ROWS, cap - PREFIX_BLOCK)
    return w0, end - w0 <= PREFIX_BLOCK


def _gather_kernel(cap, start_ref, slot_ref, h_ref, xs_ref, acc_ref):
    b = pl.program_id(0)
    e = pl.program_id(1)
    sid = slot_ref[0, 0]
    s = sid.shape[1]
    n_blk = s // PREFIX_BLOCK
    windows = [_slot_window(start_ref, (b * (n_blk + 1) + j) * N_EXPERTS + e, cap) for j in range(n_blk)]
    fits = functools.reduce(jnp.logical_and, [ok for _, ok in windows])

    @pl.when(fits)
    def _():
        acc_ref[...] = jnp.zeros_like(acc_ref)
        rows = lax.broadcasted_iota(I32, (PREFIX_BLOCK, PREFIX_BLOCK), 0)
        for j, (w0, _) in enumerate(windows):
            tok = slice(j * PREFIX_BLOCK, (j + 1) * PREFIX_BLOCK)
            hit = (sid[:, tok] - w0) == rows
            win = pl.ds(pl.multiple_of(w0, BF16_ROWS), PREFIX_BLOCK)
            acc_ref[win, :] += _dot(jnp.where(hit, 1.0, 0.0).astype(BF16), h_ref[0, tok, :])
        xs_ref[0] = acc_ref[...].astype(BF16)

    @pl.when(jnp.logical_not(fits))
    def _():
        hit = sid == lax.broadcasted_iota(I32, (cap, s), 0)
        xs_ref[0] = _dot(jnp.where(hit, 1.0, 0.0).astype(BF16), h_ref[0]).astype(BF16)


def _gather(starts, slot_row, hn, cap):
    b, s, d = hn.shape
    return pl.pallas_call(
        functools.partial(_gather_kernel, cap),
        grid_spec=pltpu.PrefetchScalarGridSpec(
            num_scalar_prefetch=1,
            grid=(b, N_EXPERTS),
            in_specs=[pl.BlockSpec((1, 1, 1, s), lambda bi, e, st: (bi, e, 0, 0)),
                      pl.BlockSpec((1, s, d), lambda bi, e, st: (bi, 0, 0))],
            out_specs=pl.BlockSpec((1, cap, d), lambda bi, e, st: (e, bi, 0)),
            scratch_shapes=[pltpu.VMEM((cap, d), F32)]),
        out_shape=jax.ShapeDtypeStruct((N_EXPERTS, b * cap, d), BF16),
        compiler_params=_params(("arbitrary", "arbitrary")),
        name="gather",
    )(starts, slot_row, hn)


FFN_ROW_GROUPS = 2


def _ffn_kernel(xs_ref, wg_ref, wu_ref, wd_ref, y_ref, acc_ref):
    f = pl.program_id(1)

    @pl.when(f == 0)
    def _():
        acc_ref[...] = jnp.zeros_like(acc_ref)

    wg, wu, wd = wg_ref[0].astype(BF16), wu_ref[0].astype(BF16), wd_ref[0].astype(BF16)
    rows = xs_ref.shape[1] // FFN_ROW_GROUPS
    for r in range(FFN_ROW_GROUPS):
        sl = slice(r * rows, (r + 1) * rows)
        xs = xs_ref[0, sl, :]
        g = _dot(xs, wg)
        u = _dot(xs, wu)
        act = (g / (1.0 + jnp.exp(-g)) * u).astype(BF16)
        acc_ref[sl, :] += _dot(act, wd)

    @pl.when(f == pl.num_programs(1) - 1)
    def _():
        y_ref[0] = acc_ref[...].astype(BF16)


def _ffn(xs, w_gate, w_up, w_down, tf=256):
    e, m, d = xs.shape
    ff = w_gate.shape[2]
    row = lambda ei, f: (ei, 0, 0)
    return pl.pallas_call(
        _ffn_kernel,
        grid=(e, ff // tf),
        in_specs=[pl.BlockSpec((1, m, d), row),
                  pl.BlockSpec((1, d, tf), lambda ei, f: (ei, 0, f)),
                  pl.BlockSpec((1, d, tf), lambda ei, f: (ei, 0, f)),
                  pl.BlockSpec((1, tf, d), lambda ei, f: (ei, f, 0))],
        out_specs=pl.BlockSpec((1, m, d), row),
        out_shape=jax.ShapeDtypeStruct((e, m, d), BF16),
        scratch_shapes=[pltpu.VMEM((m, d), F32)],
        compiler_params=_params(("parallel", "arbitrary")),
        name="ffn",
    )(xs, w_gate, w_up, w_down)


def _combine_kernel(cap, start_ref, x1_ref, slot_ref, aff_ref, y_ref, g_ref, o_ref):
    b = pl.program_id(0)
    i = pl.program_id(1)
    slot = slot_ref[0]
    aff = aff_ref[0]
    base = (b * (pl.num_programs(1) + 1) + i) * N_EXPERTS
    windows = [_slot_window(start_ref, base + e, cap) for e in range(N_EXPERTS)]
    fits = functools.reduce(jnp.logical_and, [ok for _, ok in windows])

    def finish(acc):
        o_ref[0] = _rms(acc, g_ref[...])

    @pl.when(fits)
    def _():
        cidx = lax.broadcasted_iota(I32, (PREFIX_BLOCK, PREFIX_BLOCK), 1)
        acc = x1_ref[0]
        for e, (w0, _) in enumerate(windows):
            hit = (slot[:, e:e + 1] - w0) == cidx
            rows = y_ref[e, pl.ds(pl.multiple_of(w0, BF16_ROWS), PREFIX_BLOCK), :]
            acc = acc + aff[:, e:e + 1] * _dot(jnp.where(hit, 1.0, 0.0).astype(BF16), rows)
        finish(acc)

    @pl.when(jnp.logical_not(fits))
    def _():
        cidx = lax.broadcasted_iota(I32, (PREFIX_BLOCK, cap), 1)
        acc = x1_ref[0]
        for e in range(N_EXPERTS):
            hit = slot[:, e:e + 1] == cidx
            acc = acc + aff[:, e:e + 1] * _dot(jnp.where(hit, 1.0, 0.0).astype(BF16), y_ref[e])
        finish(acc)


def _combine(starts, x1, slot_t, aff_t, y, g_final, cap):
    b, s, d = x1.shape
    tile = lambda bi, i, st: (bi, i, 0)
    return pl.pallas_call(
        functools.partial(_combine_kernel, cap),
        grid_spec=pltpu.PrefetchScalarGridSpec(
            num_scalar_prefetch=1,
            grid=(b, s // PREFIX_BLOCK),
            in_specs=[pl.BlockSpec((1, PREFIX_BLOCK, d), tile), pl.BlockSpec((1, PREFIX_BLOCK, LANES), tile),
                      pl.BlockSpec((1, PREFIX_BLOCK, LANES), tile),
                      pl.BlockSpec((N_EXPERTS, cap, d), lambda bi, i, st: (0, bi, 0)),
                      pl.BlockSpec((1, d), lambda bi, i, st: (0, 0))],
            out_specs=pl.BlockSpec((1, PREFIX_BLOCK, d), tile)),
        out_shape=jax.ShapeDtypeStruct((b, s, d), F32),
        compiler_params=_params(("arbitrary", "arbitrary")),
        name="combine",
    )(starts, x1, slot_t, aff_t, y, g_final)


def _layer(x, positions, g_mix, w_in, w_pool_mix, pool_scale, w_pool_out, lam_q1, lam_k1, lam_q2,
           lam_k2, g_subln, w_attn_out, w_out, g_ffn, w_router, w_gate, w_up, w_down, lam_init):
    b, s, d = x.shape
    t = b * s
    cap = EC_CAPACITY_FACTOR * s // N_EXPERTS

    pos_row = positions.reshape(1, t).astype(F32)
    inv = (ROPE_THETA ** (-jnp.arange(ROT_HALF, dtype=F32) * 2.0 / ROT_DIM)).reshape(ROT_HALF, 1)

    u, q, k1, k2, v, gp, ga = _inproj(x.reshape(t, d), pos_row, g_mix.reshape(1, d), inv, w_in.astype(BF16))
    sh = lambda a: a.reshape(b, s, d)

    lamv = jnp.zeros((SUBLANES, LANES), F32).at[0:4, 0:HEAD_DIM].set(jnp.stack([lam_q1, lam_k1, lam_q2, lam_k2]).astype(F32))
    vt = jnp.transpose(sh(v), (0, 2, 1))
    o = _attention(sh(q), sh(k1), sh(k2), vt, lamv, g_subln.reshape(1, HEAD_WIDTH), lam_init)

    wr = jnp.zeros((d, LANES), BF16).at[:, :N_EXPERTS].set(w_router.astype(BF16))
    x1, hn, logits = _mix(sh(u), o, sh(gp), sh(ga), x, w_pool_mix.astype(BF16), pool_scale.reshape(1, d),
                          w_pool_out.astype(BF16), w_attn_out.astype(BF16), w_out.astype(BF16),
                          g_ffn.reshape(1, d), wr)

    slot_t, aff_t, starts = _route(logits, cap)
    starts = starts[:, :, :N_EXPERTS].reshape(-1)
    slot_row = jnp.transpose(slot_t[:, :, :N_EXPERTS], (0, 2, 1)).reshape(b, N_EXPERTS, 1, s)
    xs = _gather(starts, slot_row, hn, cap)
    y = _ffn(xs, w_gate, w_up, w_down)
    return x1, starts, slot_t, aff_t, y, cap


def kernel(x, positions, g_mix, w_in, w_pool_mix, pool_scale, w_pool_out, lam_q1, lam_k1, lam_q2, lam_k2,
           g_subln, w_attn_out, w_out, g_ffn, w_router, w_gate, w_up, w_down, g_final):
    depth = g_mix.shape[0]
    assert depth == 1, "the final norm is fused into the last layer's combine step"
    l = 0
    lam_init = 0.8 - 0.6 * math.exp(-0.3 * l)
    x1, starts, slot_t, aff_t, y, cap = _layer(
        x, positions, g_mix[l], w_in[l], w_pool_mix[l], pool_scale[l], w_pool_out[l], lam_q1[l], lam_k1[l],
        lam_q2[l], lam_k2[l], g_subln[l], w_attn_out[l], w_out[l], g_ffn[l], w_router[l], w_gate[l], w_up[l],
        w_down[l], lam_init)
    return _combine(starts, x1, slot_t, aff_t, y, g_final.reshape(1, D_MODEL), cap)
```

```python
import functools
import math

import jax
import jax.numpy as jnp
from jax import lax
from jax.experimental import pallas as pl
from jax.experimental.pallas import tpu as pltpu

F32 = jnp.float32
BF16 = jnp.bfloat16
I32 = jnp.int32

D_MODEL = 1024
N_HEADS = 8
HEAD_DIM = 64
HEAD_WIDTH = 2 * HEAD_DIM
ROT_DIM = HEAD_DIM // 4
ROT_HALF = ROT_DIM // 2
ROPE_THETA = 500000.0
POOL_WINDOWS = (2, 4, 8, 16)
POOL_GROUP_DIM = D_MODEL // len(POOL_WINDOWS)
POOL_HALO = 16
N_EXPERTS = 16
EC_CAPACITY_FACTOR = 2
RMS_EPS = 1e-6
Q_SCALE = HEAD_DIM ** -0.5 * math.log2(math.e)
LANES = 128
SUBLANES = 8

VMEM_LIMIT = 56 * 1024 * 1024


def _params(sem):
    return pltpu.CompilerParams(dimension_semantics=sem, vmem_limit_bytes=VMEM_LIMIT)


def _rms(x, g):
    return (x * lax.rsqrt(jnp.mean(x * x, axis=-1, keepdims=True) + RMS_EPS)) * g


def _dot(a, b):
    return jnp.dot(a, b, preferred_element_type=F32)


def _rotary(z, cos, sin, first_half):
    up = pltpu.roll(z, LANES - ROT_HALF, 1)
    down = pltpu.roll(z, ROT_HALF, 1)
    partner = jnp.where(first_half, -up, down)
    return z * cos + partner * sin


def _inproj_kernel(x_ref, pos_ref, g_ref, inv_ref, w_ref,
                   u_ref, q_ref, k1_ref, k2_ref, v_ref, gp_ref, ga_ref):
    hb = _rms(x_ref[...], g_ref[...]).astype(BF16)
    ang = inv_ref[...] * pos_ref[...]
    cos_f, sin_f = jnp.cos(ang), jnp.sin(ang)
    tm = ang.shape[1]
    rest = HEAD_DIM - ROT_DIM
    cos = jnp.concatenate([cos_f, cos_f, jnp.ones((rest, tm), F32)] * 2, axis=0).T
    sin = jnp.concatenate([sin_f, sin_f, jnp.zeros((rest, tm), F32)] * 2, axis=0).T
    lane = lax.broadcasted_iota(I32, cos.shape, 1)
    first_half = (lane % HEAD_DIM) < ROT_HALF
    map1 = lane < HEAD_DIM

    def col(j):
        return _dot(hb, w_ref[:, j * D_MODEL:(j + 1) * D_MODEL])

    u_ref[...] = col(0)
    zq = col(1)
    zk = col(2)
    for h in range(N_HEADS):
        sl = slice(h * HEAD_WIDTH, (h + 1) * HEAD_WIDTH)
        q_ref[:, sl] = (_rotary(zq[:, sl], cos, sin, first_half) * Q_SCALE).astype(BF16)
        kr = _rotary(zk[:, sl], cos, sin, first_half)
        k1_ref[:, sl] = jnp.where(map1, kr, 0.0).astype(BF16)
        k2_ref[:, sl] = jnp.where(map1, 0.0, kr).astype(BF16)
    v_ref[0] = col(3).T.astype(BF16)
    gp_ref[...] = 1.0 / (1.0 + jnp.exp(-col(4)))
    ga_ref[...] = 1.0 / (1.0 + jnp.exp(-col(5)))


def _inproj(x2d, pos_row, g_mix, inv_freq, w_in_bf, seq, tm=256):
    t = x2d.shape[0]
    nt = seq // tm
    n_cols = w_in_bf.shape[1]
    row = lambda i: (i, 0)
    const = lambda i: (0, 0)
    tile = pl.BlockSpec((tm, D_MODEL), row)
    out_shapes = [jax.ShapeDtypeStruct((t, D_MODEL), dt) for dt in (F32, BF16, BF16, BF16, BF16, F32, F32)]
    out_shapes[4] = jax.ShapeDtypeStruct((t // seq, D_MODEL, seq), BF16)
    out_specs = [tile] * 7
    out_specs[4] = pl.BlockSpec((1, D_MODEL, tm), lambda i: (i // nt, 0, i % nt))
    return pl.pallas_call(
        _inproj_kernel,
        grid=(t // tm,),
        in_specs=[tile, pl.BlockSpec((1, tm), lambda i: (0, i)), pl.BlockSpec((1, D_MODEL), const),
                  pl.BlockSpec((ROT_HALF, 1), const), pl.BlockSpec((D_MODEL, n_cols), const)],
        out_specs=out_specs,
        out_shape=out_shapes,
        compiler_params=_params(("parallel",)),
        name="inproj",
    )(x2d, pos_row, g_mix, inv_freq, w_in_bf)


SCORE_SLOTS = 3
SUM_ROWS = 16

def _attn_kernel(lam_init, tk, q_ref, k1_ref, k2_ref, vt_ref, lamv_ref, g_ref, o_ref, st_ref, acc_ref):
    lv = lamv_ref[...]
    lam = (jnp.exp(jnp.sum(lv[0:1] * lv[1:2], axis=-1, keepdims=True))
           - jnp.exp(jnp.sum(lv[2:3] * lv[3:4], axis=-1, keepdims=True)) + lam_init)
    q = q_ref[0]
    tq = q.shape[0]
    n_chunks = k1_ref.shape[1] // tk
    nt = (((1,), (1,)), ((), ()))
    k_refs = (k1_ref, k2_ref)

    def scores(c):
        for mp in range(2):
            st_ref[c % SCORE_SLOTS, mp] = lax.dot_general(k_refs[mp][0, c * tk:(c + 1) * tk, :], q, nt,
                                                          preferred_element_type=F32)

    ones_rows = jnp.ones((SUM_ROWS, tk), BF16)

    def accumulate(c, m):
        out_m = []
        for mp in range(2):
            st = st_ref[c % SCORE_SLOTS, mp]
            m_new = jnp.maximum(m[mp], jnp.max(st, axis=0, keepdims=True))
            alpha = jnp.exp2(m[mp] - m_new)
            et = jnp.exp2(st - m_new).astype(BF16)
            vt_ext = jnp.concatenate([vt_ref[0, :, c * tk:(c + 1) * tk], ones_rows], axis=0)
            pv = _dot(vt_ext, et)
            acc_ref[mp] = pv if c == 0 else acc_ref[mp] * alpha + pv
            out_m.append(m_new)
        return out_m

    m = [jnp.full((1, tq), -jnp.inf, F32)] * 2
    for c in range(min(SCORE_SLOTS - 1, n_chunks)):
        scores(c)
    for c in range(n_chunks):
        if c + SCORE_SLOTS - 1 < n_chunks:
            scores(c + SCORE_SLOTS - 1)
        m = accumulate(c, m)
    a1, a2 = acc_ref[0], acc_ref[1]
    r1 = 1.0 / a1[HEAD_WIDTH:HEAD_WIDTH + 1]
    r2 = lam / a2[HEAD_WIDTH:HEAD_WIDTH + 1]
    ot = a1[:HEAD_WIDTH] * r1 - a2[:HEAD_WIDTH] * r2
    o_ref[0] = (_rms(ot.T, g_ref[...]) * (1.0 - lam_init)).astype(BF16)


def _attention(q, k1, k2, vt, lamv, g_subln, lam_init, tq=1024, tk=256):
    b, s, _ = q.shape
    qspec = pl.BlockSpec((1, tq, HEAD_WIDTH), lambda bi, h, qi: (bi, qi, h))
    kspec = pl.BlockSpec((1, s, HEAD_WIDTH), lambda bi, h, qi: (bi, 0, h))
    vspec = pl.BlockSpec((1, HEAD_WIDTH, s), lambda bi, h, qi: (bi, h, 0))
    const = lambda bi, h, qi: (0, 0)
    return pl.pallas_call(
        functools.partial(_attn_kernel, lam_init, tk),
        grid=(b, N_HEADS, s // tq),
        in_specs=[qspec, kspec, kspec, vspec,
                  pl.BlockSpec((SUBLANES, LANES), const), pl.BlockSpec((1, HEAD_WIDTH), const)],
        out_specs=qspec,
        out_shape=jax.ShapeDtypeStruct(k1.shape, BF16),
        scratch_shapes=[pltpu.VMEM((SCORE_SLOTS, 2, tk, tq), F32), pltpu.VMEM((2, HEAD_WIDTH + SUM_ROWS, tq), F32)],
        compiler_params=_params(("parallel", "parallel", "arbitrary")),
        name="attn",
    )(q, k1, k2, vt, lamv, g_subln)


def _mix_kernel(seq, u_ref, up_ref, un_ref, o_ref, gp_ref, ga_ref, x_ref,
                wmix_ref, ps_ref, wpo_ref, wao_ref, wo_ref, gffn_ref, wr_ref,
                x1_ref, hn_ref, lg_ref, upad_ref):
    tm = u_ref.shape[1]
    i = pl.program_id(1)
    nt = pl.num_programs(1)
    main = u_ref[0]
    upad_ref[0:POOL_HALO, :] = jnp.where(i > 0, up_ref[0], 0.0)
    upad_ref[POOL_HALO:POOL_HALO + tm, :] = main
    upad_ref[POOL_HALO + tm:, :] = jnp.where(i < nt - 1, un_ref[0], 0.0)
    t = i * tm + lax.broadcasted_iota(I32, (tm, 1), 0)
    mixed = []
    for g, w in enumerate(POOL_WINDOWS):
        cols = slice(g * POOL_GROUP_DIM, (g + 1) * POOL_GROUP_DIM)
        win = upad_ref[POOL_HALO - w // 2:POOL_HALO - w // 2 + tm, cols]
        for d in range(-w // 2 + 1, w // 2):
            win = win + upad_ref[POOL_HALO + d:POOL_HALO + d + tm, cols]
        cnt = (jnp.minimum(t + w // 2, seq) - jnp.maximum(t - w // 2, 0)).astype(F32)
        pooled = win / cnt - main[:, cols]
        mixed.append((_dot(pooled.astype(BF16), wmix_ref[g]) * ps_ref[:, cols]).astype(BF16))
    y_pool = _dot(jnp.concatenate(mixed, axis=-1), wpo_ref[...])
    y_attn = _dot(o_ref[0], wao_ref[...])
    merged = gp_ref[0] * y_pool + ga_ref[0] * y_attn
    x1 = x_ref[0] + _dot(merged.astype(BF16), wo_ref[...])
    x1_ref[0] = x1
    hn = _rms(x1, gffn_ref[...]).astype(BF16)
    hn_ref[0] = hn
    lg_ref[0] = _dot(hn, wr_ref[...])


def _mix(u, o, gp, ga, x, wmix, ps, wpo, wao, wo, gffn, wr, tm=512):
    b, s, d = x.shape
    hb = tm // POOL_HALO
    n_halo = s // POOL_HALO
    tile = pl.BlockSpec((1, tm, d), lambda bi, i: (bi, i, 0))
    prev = pl.BlockSpec((1, POOL_HALO, d), lambda bi, i: (bi, jnp.maximum(i * hb - 1, 0), 0))
    nxt = pl.BlockSpec((1, POOL_HALO, d), lambda bi, i: (bi, jnp.minimum((i + 1) * hb, n_halo - 1), 0))
    c2 = lambda bi, i: (0, 0)
    c3 = lambda bi, i: (0, 0, 0)
    return pl.pallas_call(
        functools.partial(_mix_kernel, s),
        grid=(b, s // tm),
        in_specs=[tile, prev, nxt, tile, tile, tile, tile,
                  pl.BlockSpec(wmix.shape, c3), pl.BlockSpec((1, d), c2),
                  pl.BlockSpec((d, d), c2), pl.BlockSpec((d, d), c2), pl.BlockSpec((d, d), c2),
                  pl.BlockSpec((1, d), c2), pl.BlockSpec((d, LANES), c2)],
        out_specs=[tile, tile, pl.BlockSpec((1, tm, LANES), lambda bi, i: (bi, i, 0))],
        out_shape=[jax.ShapeDtypeStruct((b, s, d), F32), jax.ShapeDtypeStruct((b, s, d), BF16),
                   jax.ShapeDtypeStruct((b, s, LANES), F32)],
        scratch_shapes=[pltpu.VMEM((tm + 2 * POOL_HALO, d), F32)],
        compiler_params=_params(("parallel", "parallel")),
        name="mix",
    )(u, u, u, o, gp, ga, x, wmix, ps, wpo, wao, wo, gffn, wr)


PREFIX_BLOCK = 256
BF16_ROWS = 16
MIN_NORMAL_BITS = 0x00800000
F32_MAGNITUDE_BITS = 31


def _prefix_rows(mask, tri):
    out = []
    carries = [jnp.zeros((1, mask.shape[1]), F32)]
    for blk in range(mask.shape[0] // PREFIX_BLOCK):
        m = mask[blk * PREFIX_BLOCK:(blk + 1) * PREFIX_BLOCK]
        p = _dot(tri, m.astype(BF16)) + carries[-1]
        carries.append(p[PREFIX_BLOCK - 1:PREFIX_BLOCK])
        out.append(p)
    return jnp.concatenate(out, axis=0), jnp.concatenate(carries, axis=0)


def _route_kernel(cap, lg_ref, slot_ref, aff_ref, start_ref):
    lg = lg_ref[0]
    lane = lax.broadcasted_iota(I32, lg.shape, 1)
    valid = lane < N_EXPERTS
    lg = jnp.where(valid, lg, -jnp.inf)
    ex = jnp.exp(lg - jnp.max(lg, axis=-1, keepdims=True))
    aff = ex / jnp.sum(ex, axis=-1, keepdims=True)
    aff_ref[0] = aff
    aff_e = aff.T[:N_EXPERTS, :]

    def step(k, ans):
        cand = ans | jnp.left_shift(jnp.int32(1), F32_MAGNITUDE_BITS - 1 - k)
        cnt = jnp.sum(jnp.where(aff_e >= lax.bitcast_convert_type(cand, F32), 1.0, 0.0), axis=1, keepdims=True)
        return jnp.where(cnt >= cap, cand, ans)

    thr_e = lax.fori_loop(0, F32_MAGNITUDE_BITS, step, jnp.zeros((N_EXPERTS, 1), I32))
    thr_sq = jnp.concatenate([jnp.broadcast_to(thr_e, (N_EXPERTS, LANES)),
                              jnp.zeros((LANES - N_EXPERTS, LANES), I32)], axis=0)
    thr = thr_sq.T[0:1, :]
    thr = jnp.where(thr < MIN_NORMAL_BITS, 0, thr)
    nxt = jnp.where(thr == 0, MIN_NORMAL_BITS, thr + 1)
    lo = lax.bitcast_convert_type(thr, F32)
    hi = lax.bitcast_convert_type(nxt, F32)
    gt = jnp.where(aff >= hi, 1.0, 0.0)
    eq = jnp.where((aff >= lo) & (aff < hi), 1.0, 0.0)
    r, c = lax.broadcasted_iota(I32, (PREFIX_BLOCK, PREFIX_BLOCK), 0), lax.broadcasted_iota(I32, (PREFIX_BLOCK, PREFIX_BLOCK), 1)
    tri = jnp.where(c <= r, 1.0, 0.0).astype(BF16)
    room = cap - jnp.sum(gt, axis=0, keepdims=True)
    eq_before = _prefix_rows(eq, tri)[0] - eq
    sel = jnp.maximum(gt, jnp.where(eq_before < room, eq, 0.0))
    chosen_upto, block_start = _prefix_rows(sel, tri)
    slot_ref[0] = jnp.where(valid & (sel > 0.0), chosen_upto - 1.0, -1.0).astype(I32)
    start_ref[0] = block_start.astype(I32)


def _route(logits, cap):
    b, s, _ = logits.shape
    n_starts = s // PREFIX_BLOCK + 1
    spec = pl.BlockSpec((1, s, LANES), lambda bi: (bi, 0, 0))
    return pl.pallas_call(
        functools.partial(_route_kernel, cap),
        grid=(b,),
        in_specs=[spec],
        out_specs=[spec, spec, pl.BlockSpec((1, n_starts, LANES), lambda bi: (bi, 0, 0))],
        out_shape=[jax.ShapeDtypeStruct((b, s, LANES), I32), jax.ShapeDtypeStruct((b, s, LANES), F32),
                   jax.ShapeDtypeStruct((b, n_starts, LANES), I32)],
        compiler_params=_params(("parallel",)),
        name="route",
    )(logits)


def _slot_window(start_ref, base, cap):
    first = start_ref[base]
    end = start_ref[base + N_EXPERTS]
    w0 = jnp.minimum((first // BF16_ROWS) * BF16_ROWS, cap - PREFIX_BLOCK)
    return w0, end - w0 <= PREFIX_BLOCK


def _gather_kernel(cap, start_ref, slot_ref, h_ref, xs_ref, acc_ref):
    b = pl.program_id(0)
    e = pl.program_id(1)
    sid = slot_ref[0, 0]
    s = sid.shape[1]
    n_blk = s // PREFIX_BLOCK
    windows = [_slot_window(start_ref, (b * (n_blk + 1) + j) * N_EXPERTS + e, cap) for j in range(n_blk)]
    fits = functools.reduce(jnp.logical_and, [ok for _, ok in windows])

    @pl.when(fits)
    def _():
        acc_ref[...] = jnp.zeros_like(acc_ref)
        rows = lax.broadcasted_iota(I32, (PREFIX_BLOCK, PREFIX_BLOCK), 0)
        for j, (w0, _) in enumerate(windows):
            tok = slice(j * PREFIX_BLOCK, (j + 1) * PREFIX_BLOCK)
            hit = (sid[:, tok] - w0) == rows
            win = pl.ds(pl.multiple_of(w0, BF16_ROWS), PREFIX_BLOCK)
            acc_ref[win, :] += _dot(jnp.where(hit, 1.0, 0.0).astype(BF16), h_ref[0, tok, :])
        xs_ref[0] = acc_ref[...].astype(BF16)

    @pl.when(jnp.logical_not(fits))
    def _():
        hit = sid == lax.broadcasted_iota(I32, (cap, s), 0)
        xs_ref[0] = _dot(jnp.where(hit, 1.0, 0.0).astype(BF16), h_ref[0]).astype(BF16)


def _gather(starts, slot_row, hn, cap):
    b, s, d = hn.shape
    return pl.pallas_call(
        functools.partial(_gather_kernel, cap),
        grid_spec=pltpu.PrefetchScalarGridSpec(
            num_scalar_prefetch=1,
            grid=(b, N_EXPERTS),
            in_specs=[pl.BlockSpec((1, 1, 1, s), lambda bi, e, st: (bi, e, 0, 0)),
                      pl.BlockSpec((1, s, d), lambda bi, e, st: (bi, 0, 0))],
            out_specs=pl.BlockSpec((1, cap, d), lambda bi, e, st: (e, bi, 0)),
            scratch_shapes=[pltpu.VMEM((cap, d), F32)]),
        out_shape=jax.ShapeDtypeStruct((N_EXPERTS, b * cap, d), BF16),
        compiler_params=_params(("arbitrary", "arbitrary")),
        name="gather",
    )(starts, slot_row, hn)


FFN_ROW_GROUPS = 2


def _ffn_kernel(xs_ref, wg_ref, wu_ref, wd_ref, y_ref, acc_ref):
    f = pl.program_id(1)

    @pl.when(f == 0)
    def _():
        acc_ref[...] = jnp.zeros_like(acc_ref)

    wg, wu, wd = wg_ref[0].astype(BF16), wu_ref[0].astype(BF16), wd_ref[0].astype(BF16)
    rows = xs_ref.shape[1] // FFN_ROW_GROUPS
    for r in range(FFN_ROW_GROUPS):
        sl = slice(r * rows, (r + 1) * rows)
        xs = xs_ref[0, sl, :]
        g = _dot(xs, wg)
        u = _dot(xs, wu)
        act = (g / (1.0 + jnp.exp(-g)) * u).astype(BF16)
        acc_ref[sl, :] += _dot(act, wd)

    @pl.when(f == pl.num_programs(1) - 1)
    def _():
        y_ref[0] = acc_ref[...].astype(BF16)


def _ffn(xs, w_gate, w_up, w_down, tf=256):
    e, m, d = xs.shape
    ff = w_gate.shape[2]
    row = lambda ei, f: (ei, 0, 0)
    return pl.pallas_call(
        _ffn_kernel,
        grid=(e, ff // tf),
        in_specs=[pl.BlockSpec((1, m, d), row),
                  pl.BlockSpec((1, d, tf), lambda ei, f: (ei, 0, f)),
                  pl.BlockSpec((1, d, tf), lambda ei, f: (ei, 0, f)),
                  pl.BlockSpec((1, tf, d), lambda ei, f: (ei, f, 0))],
        out_specs=pl.BlockSpec((1, m, d), row),
        out_shape=jax.ShapeDtypeStruct((e, m, d), BF16),
        scratch_shapes=[pltpu.VMEM((m, d), F32)],
        compiler_params=_params(("parallel", "arbitrary")),
        name="ffn",
    )(xs, w_gate, w_up, w_down)


def _combine_kernel(cap, start_ref, x1_ref, slot_ref, aff_ref, y_ref, g_ref, o_ref):
    b = pl.program_id(0)
    i = pl.program_id(1)
    slot = slot_ref[0]
    aff = aff_ref[0]
    base = (b * (pl.num_programs(1) + 1) + i) * N_EXPERTS
    windows = [_slot_window(start_ref, base + e, cap) for e in range(N_EXPERTS)]
    fits = functools.reduce(jnp.logical_and, [ok for _, ok in windows])

    def finish(acc):
        o_ref[0] = _rms(acc, g_ref[...])

    @pl.when(fits)
    def _():
        cidx = lax.broadcasted_iota(I32, (PREFIX_BLOCK, PREFIX_BLOCK), 1)
        acc = x1_ref[0]
        for e, (w0, _) in enumerate(windows):
            hit = (slot[:, e:e + 1] - w0) == cidx
            rows = y_ref[e, pl.ds(pl.multiple_of(w0, BF16_ROWS), PREFIX_BLOCK), :]
            acc = acc + aff[:, e:e + 1] * _dot(jnp.where(hit, 1.0, 0.0).astype(BF16), rows)
        finish(acc)

    @pl.when(jnp.logical_not(fits))
    def _():
        cidx = lax.broadcasted_iota(I32, (PREFIX_BLOCK, cap), 1)
        acc = x1_ref[0]
        for e in range(N_EXPERTS):
            hit = slot[:, e:e + 1] == cidx
            acc = acc + aff[:, e:e + 1] * _dot(jnp.where(hit, 1.0, 0.0).astype(BF16), y_ref[e])
        finish(acc)


def _combine(starts, x1, slot_t, aff_t, y, g_final, cap):
    b, s, d = x1.shape
    tile = lambda bi, i, st: (bi, i, 0)
    return pl.pallas_call(
        functools.partial(_combine_kernel, cap),
        grid_spec=pltpu.PrefetchScalarGridSpec(
            num_scalar_prefetch=1,
            grid=(b, s // PREFIX_BLOCK),
            in_specs=[pl.BlockSpec((1, PREFIX_BLOCK, d), tile), pl.BlockSpec((1, PREFIX_BLOCK, LANES), tile),
                      pl.BlockSpec((1, PREFIX_BLOCK, LANES), tile),
                      pl.BlockSpec((N_EXPERTS, cap, d), lambda bi, i, st: (0, bi, 0)),
                      pl.BlockSpec((1, d), lambda bi, i, st: (0, 0))],
            out_specs=pl.BlockSpec((1, PREFIX_BLOCK, d), tile)),
        out_shape=jax.ShapeDtypeStruct((b, s, d), F32),
        compiler_params=_params(("arbitrary", "arbitrary")),
        name="combine",
    )(starts, x1, slot_t, aff_t, y, g_final)


def _layer(x, positions, g_mix, w_in, w_pool_mix, pool_scale, w_pool_out, lam_q1, lam_k1, lam_q2,
           lam_k2, g_subln, w_attn_out, w_out, g_ffn, w_router, w_gate, w_up, w_down, lam_init):
    b, s, d = x.shape
    t = b * s
    cap = EC_CAPACITY_FACTOR * s // N_EXPERTS

    pos_row = positions.reshape(1, t).astype(F32)
    inv = (ROPE_THETA ** (-jnp.arange(ROT_HALF, dtype=F32) * 2.0 / ROT_DIM)).reshape(ROT_HALF, 1)

    u, q, k1, k2, vt, gp, ga = _inproj(x.reshape(t, d), pos_row, g_mix.reshape(1, d), inv, w_in.astype(BF16), s)
    sh = lambda a: a.reshape(b, s, d)

    lamv = jnp.zeros((SUBLANES, LANES), F32).at[0:4, 0:HEAD_DIM].set(jnp.stack([lam_q1, lam_k1, lam_q2, lam_k2]).astype(F32))
    o = _attention(sh(q), sh(k1), sh(k2), vt, lamv, g_subln.reshape(1, HEAD_WIDTH), lam_init)

    wr = jnp.zeros((d, LANES), BF16).at[:, :N_EXPERTS].set(w_router.astype(BF16))
    x1, hn, logits = _mix(sh(u), o, sh(gp), sh(ga), x, w_pool_mix.astype(BF16), pool_scale.reshape(1, d),
                          w_pool_out.astype(BF16), w_attn_out.astype(BF16), w_out.astype(BF16),
                          g_ffn.reshape(1, d), wr)

    slot_t, aff_t, starts = _route(logits, cap)
    starts = starts[:, :, :N_EXPERTS].reshape(-1)
    slot_row = jnp.transpose(slot_t[:, :, :N_EXPERTS], (0, 2, 1)).reshape(b, N_EXPERTS, 1, s)
    xs = _gather(starts, slot_row, hn, cap)
    y = _ffn(xs, w_gate, w_up, w_down)
    return x1, starts, slot_t, aff_t, y, cap


def kernel(x, positions, g_mix, w_in, w_pool_mix, pool_scale, w_pool_out, lam_q1, lam_k1, lam_q2, lam_k2,
           g_subln, w_attn_out, w_out, g_ffn, w_router, w_gate, w_up, w_down, g_final):
    depth = g_mix.shape[0]
    assert depth == 1, "the final norm is fused into the last layer's combine step"
    l = 0
    lam_init = 0.8 - 0.6 * math.exp(-0.3 * l)
    x1, starts, slot_t, aff_t, y, cap = _layer(
        x, positions, g_mix[l], w_in[l], w_pool_mix[l], pool_scale[l], w_pool_out[l], lam_q1[l], lam_k1[l],
        lam_q2[l], lam_k2[l], g_subln[l], w_attn_out[l], w_out[l], g_ffn[l], w_router[l], w_gate[l], w_up[l],
        w_down[l], lam_init)
    return _combine(starts, x1, slot_t, aff_t, y, g_final.reshape(1, D_MODEL), cap)
```
